```python
import jax, jax.numpy as jnp
from jax import lax
import numpy as np

D_MODEL = 1024
BATCH = 32
SEQ = 2048
DEPTH = 4
DEC_BATCH = 16
DEC_SEQ = 64
PAST_LEN = 2048

CHUNK = 64
MLA_HEADS = 8
Q_RANK = 256
KV_RANK = 256
NOPE_DIM = 64
ROPE_DIM = 32
V_DIM = 64
ROPE_THETA = 10000.0
Q_BLOCK = 128
ATTN_SCALE = (NOPE_DIM + ROPE_DIM) ** -0.5
MLSTM_HEADS = 4
MLSTM_DH = 128
MLSTM_WIDTH = MLSTM_HEADS * MLSTM_DH
GMLP_GROUPS = 4
GMLP_DG = 128
GMLP_WIDTH = GMLP_GROUPS * GMLP_DG
GMLP_CHUNK = 128
BRANCH_WIDTH = 512
N_BRANCH = 3
N_EXPERTS = 16
N_GROUPS = 4
EXPERTS_PER_GROUP = N_EXPERTS // N_GROUPS
TOP_K = 2
D_EXPERT = 256
D_SHARED = 256
DN_ALPHA = (2 * DEPTH) ** 0.25
DN_BETA = (8 * DEPTH) ** -0.25
EPS = 1e-5
IN_SIZES = (Q_RANK, KV_RANK, ROPE_DIM, MLSTM_WIDTH, MLSTM_WIDTH, MLSTM_WIDTH,
            MLSTM_HEADS, MLSTM_HEADS, MLSTM_WIDTH, GMLP_WIDTH, GMLP_WIDTH, N_BRANCH * D_MODEL)
D_IN = sum(IN_SIZES)
FGATE_OFF = sum(IN_SIZES[:7])

kernel_name = 'streaming_hybrid_mla_mlstm_gmlp_moe'


def _split(z):
    offs, acc = [], 0
    for s in IN_SIZES[:-1]:
        acc += s
        offs.append(acc)
    return jnp.split(z, offs, axis=-1)


def _rmsnorm(x, g):
    xf = x.astype(jnp.float32)
    y = xf * lax.rsqrt(jnp.mean(xf * xf, -1, keepdims=True) + EPS)
    return (y * g.astype(jnp.float32)).astype(x.dtype)


def _layernorm(x, g, b):
    xf = x.astype(jnp.float32)
    mu = jnp.mean(xf, -1, keepdims=True)
    var = jnp.mean(jnp.square(xf - mu), -1, keepdims=True)
    y = (xf - mu) * lax.rsqrt(var + EPS)
    return (y * g.astype(jnp.float32) + b.astype(jnp.float32)).astype(x.dtype)


def _rope(x, cos, sin):
    xf = x.astype(jnp.float32)
    x1, x2 = jnp.split(xf, 2, axis=-1)
    return jnp.concatenate([x1 * cos - x2 * sin, x2 * cos + x1 * sin], -1).astype(x.dtype)


def _attend_block(qn, qr, qpos, kn, kr, v, kpos):
    s = jnp.einsum('bqhd,bkhd->bhqk', qn, kn) + jnp.einsum('bqhd,bkd->bhqk', qr, kr)
    s = s.astype(jnp.float32) * ATTN_SCALE
    allowed = (kpos // CHUNK)[None, :] <= (qpos // CHUNK)[:, None]
    s = jnp.where(allowed[None, None], s, -jnp.inf)
    p = jax.nn.softmax(s, axis=-1)
    return jnp.einsum('bhqk,bkhd->bqhd', p.astype(v.dtype), v)


def _mla_attention(qn, qr, qpos, kn, kr, v, kpos):
    B, T = qn.shape[0], qn.shape[1]
    if T > Q_BLOCK and T % Q_BLOCK == 0:
        nb = T // Q_BLOCK
        def blk(a):
            return jnp.moveaxis(a.reshape((B, nb, Q_BLOCK) + a.shape[2:]), 1, 0)
        out = lax.map(lambda a: _attend_block(a[0], a[1], a[2], kn, kr, v, kpos),
                      (blk(qn), blk(qr), qpos.reshape(nb, Q_BLOCK)))
        out = jnp.moveaxis(out, 0, 1).reshape(B, T, MLA_HEADS, V_DIM)
    else:
        out = _attend_block(qn, qr, qpos, kn, kr, v, kpos)
    return out.reshape(B, T, MLA_HEADS * V_DIM)


def _mlstm_chunk(carry, inp):
    C, n, m = carry
    q, k, v, ig, lf = inp
    L = q.shape[2]
    b = jnp.cumsum(lf, axis=-1)
    causal = jnp.tril(jnp.ones((L, L), dtype=bool))
    d = jnp.where(causal, b[..., :, None] - b[..., None, :] + ig[..., None, :], -jnp.inf)
    g = b + m[..., None]
    mt = jnp.maximum(g, jnp.max(d, axis=-1))
    inter = jnp.exp(g - mt)
    w = jnp.exp(d - mt[..., None])
    qk = jnp.einsum('bhtd,bhsd->bhts', q, k) * w
    num = inter[..., None] * jnp.einsum('bhed,bhtd->bhte', C, q) + jnp.einsum('bhts,bhse->bhte', qk, v)
    den = inter * jnp.einsum('bhd,bhtd->bht', n, q) + jnp.sum(qk, axis=-1)
    h = num / jnp.maximum(jnp.abs(den), jnp.exp(-mt))[..., None]
    mL = mt[..., -1]
    bL = b[..., -1:]
    ws = jnp.exp(bL - b + ig - mL[..., None])
    decay = jnp.exp(bL[..., 0] + m - mL)
    C_new = decay[..., None, None] * C + jnp.einsum('bhs,bhse,bhsd->bhed', ws, v, k)
    n_new = decay[..., None] * n + jnp.einsum('bhs,bhsd->bhd', ws, k)
    return (C_new, n_new, mL), h


def _mlstm(q, k, v, ipre, fpre, C0, n0, m0):
    B, T, _ = q.shape
    L = CHUNK if T % CHUNK == 0 else T
    nc = T // L
    def heads(a):
        a = a.astype(jnp.float32).reshape(B, nc, L, MLSTM_HEADS, MLSTM_DH)
        return jnp.transpose(a, (1, 0, 3, 2, 4))
    def gates(a):
        return jnp.transpose(a.astype(jnp.float32).reshape(B, nc, L, MLSTM_HEADS), (1, 0, 3, 2))
    ig = gates(ipre)
    lf = jax.nn.log_sigmoid(gates(fpre))
    (C, n, m), h = lax.scan(_mlstm_chunk, (C0, n0, m0),
                            (heads(q), heads(k) * (MLSTM_DH ** -0.5), heads(v), ig, lf))
    h = jnp.transpose(h, (1, 0, 3, 2, 4)).reshape(B, T, MLSTM_WIDTH)
    return h, C, n, m


def _gmlp(u, v, ln_g, ln_b, ws, bs):
    B, T, _ = u.shape
    u = jax.nn.gelu(u)
    v = _layernorm(jax.nn.gelu(v), ln_g, ln_b)
    L = GMLP_CHUNK if T % GMLP_CHUNK == 0 else T
    nc = T // L
    w = jnp.where(jnp.tril(jnp.ones((L, L), dtype=bool)), ws[:, :L, :L], 0).astype(v.dtype)
    vc = v.reshape(B, nc, L, GMLP_GROUPS, GMLP_DG)
    sp = jnp.einsum('gts,bcsgd->bctgd', w, vc) + bs[:, :L].T[:, :, None].astype(v.dtype)
    return u * sp.reshape(B, T, GMLP_WIDTH), v


def _swiglu(x, wg, wu, wd):
    return (jax.nn.silu(x @ wg) * (x @ wu)) @ wd


def _moe(x, router_w, router_b, wg, wu, wd, sg, su, sd):
    B, T, D = x.shape
    xf = x.reshape(B * T, D)
    s = jax.nn.sigmoid((xf @ router_w).astype(jnp.float32))
    sb = s + router_b.astype(jnp.float32)
    gscore = jnp.sum(lax.top_k(sb.reshape(-1, N_GROUPS, EXPERTS_PER_GROUP), TOP_K)[0], axis=-1)
    gsel = jnp.argmax(gscore, axis=-1)
    emask = (jnp.arange(N_EXPERTS) // EXPERTS_PER_GROUP)[None, :] == gsel[:, None]
    _, idx = lax.top_k(jnp.where(emask, sb, -jnp.inf), TOP_K)
    wsel = jnp.take_along_axis(s, idx, axis=-1)
    wsel = wsel / jnp.sum(wsel, axis=-1, keepdims=True)
    gate = jnp.sum(jax.nn.one_hot(idx, N_EXPERTS, dtype=jnp.float32) * wsel[..., None], axis=1)
    out = _swiglu(xf, sg, su, sd).astype(jnp.float32)
    for e in range(N_EXPERTS):
        out = out + gate[:, e:e + 1] * _swiglu(xf, wg[e], wu[e], wd[e]).astype(jnp.float32)
    return out.astype(x.dtype).reshape(B, T, D)


def _trunk(x, ckv_past, kr_past, c0, n0, m0, w):
    (w_in, b_in, q_norm_g, kv_norm_g, w_uq, w_ukv, gmlp_ln_g, gmlp_ln_b, gmlp_ws, gmlp_bs,
     w_branch, w_out, ln1_g, ln1_b, router_w, router_b, moe_w_gate, moe_w_up, moe_w_down,
     shared_w_gate, shared_w_up, shared_w_down, ln2_g, ln2_b) = w
    B, T, _ = x.shape
    past = 0 if ckv_past is None else ckv_past.shape[2]
    positions = past + jnp.arange(T)
    kpos = jnp.arange(past + T)
    half = ROPE_DIM // 2
    inv = ROPE_THETA ** (-jnp.arange(half, dtype=jnp.float32) / half)
    ang = positions.astype(jnp.float32)[:, None] * inv[None, :]
    cos, sin = jnp.cos(ang), jnp.sin(ang)
    ckvs, krs, Cs, ns, ms, gvs = [], [], [], [], [], []
    for l in range(DEPTH):
        z = x @ w_in[l] + b_in[l]
        cq, ckv, kr, mq, mk, mv, mi, mf, mo, gu, gv, gt = _split(z)
        q = (_rmsnorm(cq, q_norm_g[l]) @ w_uq[l]).reshape(B, T, MLA_HEADS, NOPE_DIM + ROPE_DIM)
        qn = q[..., :NOPE_DIM]
        qr = _rope(q[..., NOPE_DIM:], cos[:, None], sin[:, None])
        ckv = _rmsnorm(ckv, kv_norm_g[l])
        kr = _rope(kr, cos, sin)
        if ckv_past is None:
            ckv_all, kr_all = ckv, kr
        else:
            ckv_all = jnp.concatenate([ckv_past[l].astype(ckv.dtype), ckv], axis=1)
            kr_all = jnp.concatenate([kr_past[l].astype(kr.dtype), kr], axis=1)
        kv = (ckv_all @ w_ukv[l]).reshape(B, past + T, MLA_HEADS, NOPE_DIM + V_DIM)
        a_out = _mla_attention(qn, qr, positions, kv[..., :NOPE_DIM], kr_all, kv[..., NOPE_DIM:], kpos)
        if c0 is None:
            C_init = jnp.zeros((B, MLSTM_HEADS, MLSTM_DH, MLSTM_DH), jnp.float32)
            n_init = jnp.zeros((B, MLSTM_HEADS, MLSTM_DH), jnp.float32)
            m_init = jnp.zeros((B, MLSTM_HEADS), jnp.float32)
        else:
            C_init = c0[l].astype(jnp.float32)
            n_init = n0[l].astype(jnp.float32)
            m_init = m0[l].astype(jnp.float32)
        h, C, n, m = _mlstm(mq, mk, mv, mi, mf, C_init, n_init, m_init)
        b_out = (jax.nn.sigmoid(mo.astype(jnp.float32)) * h).astype(x.dtype)
        c_out, v_rows = _gmlp(gu, gv, gmlp_ln_g[l], gmlp_ln_b[l], gmlp_ws[l], gmlp_bs[l])
        merged = jax.nn.sigmoid(gt[..., :D_MODEL]) * (a_out @ w_branch[l, 0])
        merged = merged + jax.nn.sigmoid(gt[..., D_MODEL:2 * D_MODEL]) * (b_out @ w_branch[l, 1])
        merged = merged + jax.nn.sigmoid(gt[..., 2 * D_MODEL:]) * (c_out @ w_branch[l, 2])
        x = _layernorm(DN_ALPHA * x + merged @ w_out[l], ln1_g[l], ln1_b[l])
        moe_out = _moe(x, router_w, router_b, moe_w_gate[l], moe_w_up[l], moe_w_down[l],
                       shared_w_gate[l], shared_w_up[l], shared_w_down[l])
        x = _layernorm(DN_ALPHA * x + moe_out, ln2_g[l], ln2_b[l])
        ckvs.append(ckv); krs.append(kr); Cs.append(C); ns.append(n); ms.append(m); gvs.append(v_rows)
    return (x, jnp.stack(ckvs), jnp.stack(krs), jnp.stack(Cs), jnp.stack(ns), jnp.stack(ms), jnp.stack(gvs))


def setup_inputs(seed: int = 0) -> dict:
    key = jax.random.key(seed)
    ks = jax.random.split(key, 32)
    f32 = jnp.float32
    def nrm(k, shape, scale):
        return jax.random.normal(k, shape, f32) * scale
    def gain(k, shape):
        return 1.0 + 0.01 * jax.random.normal(k, shape, f32)
    b_in = nrm(ks[8], (DEPTH, D_IN), 0.01)
    b_in = b_in.at[:, FGATE_OFF:FGATE_OFF + MLSTM_HEADS].add(3.0)
    return {
        'x_prompt': nrm(ks[0], (BATCH, SEQ, D_MODEL), 1.0),
        'x_sample': nrm(ks[1], (DEC_BATCH, DEC_SEQ, D_MODEL), 1.0),
        'cache_mla_ckv': nrm(ks[2], (DEPTH, DEC_BATCH, PAST_LEN, KV_RANK), 1.0),
        'cache_mla_krope': nrm(ks[3], (DEPTH, DEC_BATCH, PAST_LEN, ROPE_DIM), 1.0),
        'state_mlstm_c': nrm(ks[4], (DEPTH, DEC_BATCH, MLSTM_HEADS, MLSTM_DH, MLSTM_DH), 0.3),
        'state_mlstm_n': jnp.abs(nrm(ks[5], (DEPTH, DEC_BATCH, MLSTM_HEADS, MLSTM_DH), 0.5)),
        'state_mlstm_m': nrm(ks[6], (DEPTH, DEC_BATCH, MLSTM_HEADS), 0.5),
        'w_in': nrm(ks[7], (DEPTH, D_MODEL, D_IN), D_MODEL ** -0.5),
        'b_in': b_in,
        'q_norm_g': gain(ks[9], (DEPTH, Q_RANK)),
        'kv_norm_g': gain(ks[10], (DEPTH, KV_RANK)),
        'w_uq': nrm(ks[11], (DEPTH, Q_RANK, MLA_HEADS * (NOPE_DIM + ROPE_DIM)), Q_RANK ** -0.5),
        'w_ukv': nrm(ks[12], (DEPTH, KV_RANK, MLA_HEADS * (NOPE_DIM + V_DIM)), KV_RANK ** -0.5),
        'gmlp_ln_g': gain(ks[13], (DEPTH, GMLP_WIDTH)),
        'gmlp_ln_b': nrm(ks[14], (DEPTH, GMLP_WIDTH), 0.01),
        'gmlp_ws': nrm(ks[15], (DEPTH, GMLP_GROUPS, GMLP_CHUNK, GMLP_CHUNK), GMLP_CHUNK ** -0.5),
        'gmlp_bs': gain(ks[16], (DEPTH, GMLP_GROUPS, GMLP_CHUNK)),
        'w_branch': nrm(ks[17], (DEPTH, N_BRANCH, BRANCH_WIDTH, D_MODEL), DN_BETA * BRANCH_WIDTH ** -0.5),
        'w_out': nrm(ks[18], (DEPTH, D_MODEL, D_MODEL), DN_BETA * D_MODEL ** -0.5),
        'ln1_g': gain(ks[19], (DEPTH, D_MODEL)),
        'ln1_b': nrm(ks[20], (DEPTH, D_MODEL), 0.01),
        'router_w': nrm(ks[21], (D_MODEL, N_EXPERTS), D_MODEL ** -0.5),
        'router_b': nrm(ks[22], (N_EXPERTS,), 0.01),
        'moe_w_gate': nrm(ks[23], (DEPTH, N_EXPERTS, D_MODEL, D_EXPERT), D_MODEL ** -0.5),
        'moe_w_up': nrm(ks[24], (DEPTH, N_EXPERTS, D_MODEL, D_EXPERT), D_MODEL ** -0.5),
        'moe_w_down': nrm(ks[25], (DEPTH, N_EXPERTS, D_EXPERT, D_MODEL), DN_BETA * D_EXPERT ** -0.5),
        'shared_w_gate': nrm(ks[26], (DEPTH, D_MODEL, D_SHARED), D_MODEL ** -0.5),
        'shared_w_up': nrm(ks[27], (DEPTH, D_MODEL, D_SHARED), D_MODEL ** -0.5),
        'shared_w_down': nrm(ks[28], (DEPTH, D_SHARED, D_MODEL), DN_BETA * D_SHARED ** -0.5),
        'ln2_g': gain(ks[29], (DEPTH, D_MODEL)),
        'ln2_b': nrm(ks[30], (DEPTH, D_MODEL), 0.01),
    }


def reference(x_prompt, x_sample, cache_mla_ckv, cache_mla_krope, state_mlstm_c, state_mlstm_n,
              state_mlstm_m, w_in, b_in, q_norm_g, kv_norm_g, w_uq, w_ukv, gmlp_ln_g, gmlp_ln_b,
              gmlp_ws, gmlp_bs, w_branch, w_out, ln1_g, ln1_b, router_w, router_b, moe_w_gate,
              moe_w_up, moe_w_down, shared_w_gate, shared_w_up, shared_w_down, ln2_g, ln2_b):
    w = (w_in, b_in, q_norm_g, kv_norm_g, w_uq, w_ukv, gmlp_ln_g, gmlp_ln_b, gmlp_ws, gmlp_bs,
         w_branch, w_out, ln1_g, ln1_b, router_w, router_b, moe_w_gate, moe_w_up, moe_w_down,
         shared_w_gate, shared_w_up, shared_w_down, ln2_g, ln2_b)
    y_prompt, p_ckv, p_kr, p_c, p_n, p_m, _ = _trunk(x_prompt, None, None, None, None, None, w)
    y_sample, s_ckv, s_kr, s_c, s_n, s_m, s_gv = _trunk(
        x_sample, cache_mla_ckv, cache_mla_krope, state_mlstm_c, state_mlstm_n, state_mlstm_m, w)
    return (y_prompt, y_sample, p_ckv, p_kr, p_c, p_n, p_m, s_ckv, s_kr, s_c, s_n, s_m, s_gv)
```

```python
import functools

import jax
import jax.numpy as jnp
import numpy as np
from jax import lax
from jax.experimental import pallas as pl
from jax.experimental.pallas import tpu as pltpu

F32 = jnp.float32
BF16 = jnp.bfloat16

D_MODEL = 1024
CHUNK = 64
MLA_HEADS = 8
Q_RANK = 256
KV_RANK = 256
NOPE_DIM = 64
ROPE_DIM = 32
V_DIM = 64
ROPE_THETA = 10000.0
ATTN_SCALE = (NOPE_DIM + ROPE_DIM) ** -0.5
MLSTM_HEADS = 4
MLSTM_DH = 128
MLSTM_WIDTH = MLSTM_HEADS * MLSTM_DH
GMLP_GROUPS = 4
GMLP_DG = 128
GMLP_WIDTH = GMLP_GROUPS * GMLP_DG
GMLP_CHUNK = 128
N_BRANCH = 3
N_EXPERTS = 16
N_GROUPS = 4
EXPERTS_PER_GROUP = N_EXPERTS // N_GROUPS
D_EXPERT = 256
D_SHARED = 256
DEPTH = 4
DN_ALPHA = (2 * DEPTH) ** 0.25
EPS = 1e-5

HEAD_PAD = 128
QK_PAD = MLA_HEADS * HEAD_PAD
V_ALL = MLA_HEADS * V_DIM
P_CQ, P_CKV, P_KR, P_KRR, P_MQ, P_MK, P_MV, P_G, P_MO, P_GU, P_GV = (
    0, 256, 512, 640, 768, 1280, 1792, 2304, 2432, 2944, 3456)
D_PROJ = 3968
O_CQ, O_CKV, O_KR, O_MQ, O_MK, O_MV, O_MI, O_MF, O_MO, O_GU, O_GV, O_GT = (
    0, 256, 512, 544, 1056, 1568, 2080, 2084, 2088, 2600, 3112, 3624)
KR_LANE = NOPE_DIM
VMEM_LIMIT = 56 * 1024 * 1024
NEG_INF = float("-inf")


def _dot(a, b):
    return jnp.dot(a, b, preferred_element_type=F32)


def _dot_nt(a, b):
    return lax.dot_general(a, b, (((1,), (1,)), ((), ())), preferred_element_type=F32)


def _const_spec(shape):
    nd = len(shape)
    return pl.BlockSpec(shape, lambda *_: (0,) * nd, pipeline_mode=pl.Buffered(1))


def _layernorm(x, g, b):
    mu = jnp.mean(x, axis=-1, keepdims=True)
    xc = x - mu
    var = jnp.mean(xc * xc, axis=-1, keepdims=True)
    return xc * lax.rsqrt(var + EPS) * g + b


def _rmsnorm(x, g):
    return x * lax.rsqrt(jnp.mean(x * x, axis=-1, keepdims=True) + EPS) * g


def _gelu(x):
    return 0.5 * x * (1.0 + jnp.tanh(np.sqrt(2.0 / np.pi).astype(np.float32) * (x + 0.044715 * (x * x * x))))


def _sigmoid(x):
    return 1.0 / (1.0 + jnp.exp(-x))


def _log_sigmoid(x):
    return jnp.minimum(x, 0.0) - jnp.log(1.0 + jnp.exp(-jnp.abs(x)))


def _proj_kernel(x_ref, w1_ref, b1_ref, qg_ref, kvg_ref, wq_ref, wkv_ref, cq_ref, sq_ref, ck_ref, sk_ref,
                 lng_ref, lnb_ref,
                 q_out, k_out, v_out, ckv_out, kr_out, mq_out, mk_out, mv_out, g_out, so_out, gu_out,
                 gv_out, *rest):
    xb = x_ref[...].astype(BF16)

    def z(lo, hi):
        return _dot(xb, w1_ref[:, lo:hi]) + b1_ref[:, lo:hi]

    zc = z(P_CQ, P_MQ)
    cqn = _rmsnorm(zc[:, P_CQ:P_CKV], qg_ref[...])
    ckvn = _rmsnorm(zc[:, P_CKV:P_KR], kvg_ref[...])
    kr = zc[:, P_KR:P_KRR] * ck_ref[...] + zc[:, P_KRR:P_MQ] * sk_ref[...]
    ckv_out[...] = ckvn
    kr_out[...] = kr
    qq = _dot(cqn.astype(BF16), wq_ref[...])
    cos8 = jnp.concatenate([cq_ref[...]] * MLA_HEADS, axis=1)
    sin8 = jnp.concatenate([sq_ref[...]] * MLA_HEADS, axis=1)
    q_out[...] = (qq[:, :QK_PAD] * cos8 + qq[:, QK_PAD:] * sin8).astype(BF16)
    kk = _dot(ckvn.astype(BF16), wkv_ref[...])
    kr8 = jnp.concatenate([kr] * MLA_HEADS, axis=1)
    k_out[...] = (kk[:, :QK_PAD] + kr8).astype(BF16)
    v_out[...] = kk[:, QK_PAD:].astype(BF16)

    zm = z(P_MQ, P_MO)
    mq_out[...] = zm[:, 0:512].astype(BF16)
    mk_out[...] = (zm[:, 512:1024] * (MLSTM_DH ** -0.5)).astype(BF16)
    mv_out[...] = zm[:, 1024:1536].astype(BF16)
    zg = zm[:, 1536:1664]
    lane = lax.broadcasted_iota(jnp.int32, zg.shape, 1)
    g_out[...] = jnp.where(lane < MLSTM_HEADS, zg, _log_sigmoid(zg))
    so_out[...] = _sigmoid(z(P_MO, P_GU)).astype(BF16)

    zu = z(P_GU, D_PROJ)
    gu_out[...] = _gelu(zu[:, :GMLP_WIDTH]).astype(BF16)
    vrows = _layernorm(_gelu(zu[:, GMLP_WIDTH:]), lng_ref[...], lnb_ref[...])
    gv_out[...] = vrows.astype(BF16)
    if rest:
        rest[0][...] = vrows


def _proj_call(x2d, pw, tabs, T, want_vrows):
    N = x2d.shape[0]
    TM = min(512, N)
    nt = max(1, T // TM)
    grid = (N // TM,)
    row = lambda w: pl.BlockSpec((TM, w), lambda i: (i, 0))
    tab = pl.BlockSpec((TM, HEAD_PAD), lambda i: (i % nt, 0))
    in_specs = [
        row(D_MODEL),
        _const_spec((D_MODEL, D_PROJ)), _const_spec((1, D_PROJ)),
        _const_spec((1, Q_RANK)), _const_spec((1, KV_RANK)),
        _const_spec((Q_RANK, 2 * QK_PAD)), _const_spec((KV_RANK, QK_PAD + V_ALL)),
        tab, tab, tab, tab,
        _const_spec((1, GMLP_WIDTH)), _const_spec((1, GMLP_WIDTH)),
    ]
    out_shapes = [
        ((N, QK_PAD), BF16), ((N, QK_PAD), BF16), ((N, V_ALL), BF16), ((N, KV_RANK), F32),
        ((N, HEAD_PAD), F32), ((N, MLSTM_WIDTH), BF16), ((N, MLSTM_WIDTH), BF16), ((N, MLSTM_WIDTH), BF16),
        ((N, HEAD_PAD), F32), ((N, MLSTM_WIDTH), BF16), ((N, GMLP_WIDTH), BF16), ((N, GMLP_WIDTH), BF16),
    ]
    if want_vrows:
        out_shapes.append(((N, GMLP_WIDTH), F32))
    return pl.pallas_call(
        _proj_kernel,
        grid=grid,
        in_specs=in_specs,
        out_specs=[row(s[1]) for s, _ in out_shapes],
        out_shape=[jax.ShapeDtypeStruct(s, d) for s, d in out_shapes],
        compiler_params=pltpu.CompilerParams(dimension_semantics=("parallel",), vmem_limit_bytes=VMEM_LIMIT),
        name="proj",
    )(x2d, pw["w1"], pw["b1"], pw["qg"], pw["kvg"], pw["wq"], pw["wkv"],
      tabs["cq"], tabs["sq"], tabs["ck"], tabs["sk"], pw["lng"], pw["lnb"])


def _pastkv_kernel(c_ref, kr_ref, wkv_ref, k_out, v_out):
    kk = _dot(c_ref[...].astype(BF16), wkv_ref[...])
    kr8 = jnp.concatenate([kr_ref[...]] * MLA_HEADS, axis=1)
    k_out[...] = (kk[:, :QK_PAD] + kr8).astype(BF16)
    v_out[...] = kk[:, QK_PAD:].astype(BF16)


def _pastkv_call(ckv2d, kr2d, wkv):
    N = ckv2d.shape[0]
    TM = min(1024, N)
    row = lambda w: pl.BlockSpec((TM, w), lambda i: (i, 0))
    return pl.pallas_call(
        _pastkv_kernel,
        grid=(N // TM,),
        in_specs=[row(KV_RANK), row(HEAD_PAD), _const_spec((KV_RANK, QK_PAD + V_ALL))],
        out_specs=[row(QK_PAD), row(V_ALL)],
        out_shape=[jax.ShapeDtypeStruct((N, QK_PAD), BF16), jax.ShapeDtypeStruct((N, V_ALL), BF16)],
        compiler_params=pltpu.CompilerParams(dimension_semantics=("parallel",), vmem_limit_bytes=VMEM_LIMIT),
        name="pastkv",
    )(ckv2d, kr2d, wkv)


def _attn_kernel(q_ref, k_ref, v_ref, o_ref, s_buf, *, TQ):
    i = pl.program_id(1)
    rc = lax.broadcasted_iota(jnp.int32, (TQ, TQ), 0) // CHUNK
    cc = lax.broadcasted_iota(jnp.int32, (TQ, TQ), 1) // CHUNK
    visible = cc <= rc
    lane = lax.broadcasted_iota(jnp.int32, (TQ, HEAD_PAD), 1)
    diag0 = pl.multiple_of(i * TQ, TQ)
    nlane = TQ // 128

    def fold(a, op):
        r = a[:, 0:128]
        for t in range(1, nlane):
            r = op(r, a[:, 128 * t:128 * (t + 1)])
        return r

    for p in range(MLA_HEADS // 2):
        pair = []
        for hh in range(2):
            h = 2 * p + hh
            hs = slice(HEAD_PAD * h, HEAD_PAD * (h + 1))
            qh = q_ref[0, :, hs]
            sd = _dot_nt(qh, k_ref[0, pl.ds(diag0, TQ), hs])
            sd = jnp.where(visible, sd, NEG_INF)

            def pass1(j, mx):
                r0 = pl.multiple_of(j * TQ, TQ)
                s = _dot_nt(qh, k_ref[0, pl.ds(r0, TQ), hs])
                s_buf[j] = s
                return jnp.maximum(mx, fold(s, jnp.maximum))

            mx = lax.fori_loop(0, i, pass1, fold(sd, jnp.maximum))
            m = jnp.max(mx, axis=-1, keepdims=True)
            vs = slice(HEAD_PAD * p, HEAD_PAD * (p + 1))
            pd = jnp.exp(sd - m)
            acc0 = _dot(pd.astype(BF16), v_ref[0, pl.ds(diag0, TQ), vs])

            def pass2(j, carry):
                ls, acc = carry
                r0 = pl.multiple_of(j * TQ, TQ)
                pj = jnp.exp(s_buf[j] - m)
                return (ls + fold(pj, jnp.add),
                        acc + _dot(pj.astype(BF16), v_ref[0, pl.ds(r0, TQ), vs]))

            ls, acc = lax.fori_loop(0, i, pass2, (fold(pd, jnp.add), acc0))
            l = jnp.sum(ls, axis=-1, keepdims=True)
            pair.append(acc / l)
        o_ref[0, :, HEAD_PAD * p:HEAD_PAD * (p + 1)] = jnp.where(lane < V_DIM, pair[0], pair[1]).astype(BF16)


def _attn_call(q, k, v):
    B, T, _ = q.shape
    TQ = min(256, T)
    nq = T // TQ
    return pl.pallas_call(
        functools.partial(_attn_kernel, TQ=TQ),
        grid=(B, nq),
        in_specs=[pl.BlockSpec((1, TQ, QK_PAD), lambda b, i: (b, i, 0)),
                  pl.BlockSpec((1, T, QK_PAD), lambda b, i: (b, 0, 0)),
                  pl.BlockSpec((1, T, V_ALL), lambda b, i: (b, 0, 0))],
        out_specs=pl.BlockSpec((1, TQ, V_ALL), lambda b, i: (b, i, 0)),
        out_shape=jax.ShapeDtypeStruct((B, T, V_ALL), BF16),
        scratch_shapes=[pltpu.VMEM((nq, TQ, TQ), F32)],
        compiler_params=pltpu.CompilerParams(dimension_semantics=("parallel", "arbitrary"),
                                             vmem_limit_bytes=VMEM_LIMIT),
        name="attn",
    )(q, k, v)


def _attn_hist_kernel(q_ref, kp_ref, vp_ref, kn_ref, vn_ref, o_ref):
    T = q_ref.shape[1]
    lane = lax.broadcasted_iota(jnp.int32, (T, HEAD_PAD), 1)
    for p in range(MLA_HEADS // 2):
        pair = []
        vs = slice(HEAD_PAD * p, HEAD_PAD * (p + 1))
        for hh in range(2):
            h = 2 * p + hh
            hs = slice(HEAD_PAD * h, HEAD_PAD * (h + 1))
            qh = q_ref[0, :, hs]
            s1 = _dot_nt(qh, kp_ref[0, :, hs])
            s2 = _dot_nt(qh, kn_ref[0, :, hs])
            m = jnp.maximum(jnp.max(s1, axis=-1, keepdims=True), jnp.max(s2, axis=-1, keepdims=True))
            p1 = jnp.exp(s1 - m)
            p2 = jnp.exp(s2 - m)
            l = jnp.sum(p1, axis=-1, keepdims=True) + jnp.sum(p2, axis=-1, keepdims=True)
            acc = _dot(p1.astype(BF16), vp_ref[0, :, vs]) + _dot(p2.astype(BF16), vn_ref[0, :, vs])
            pair.append(acc / l)
        o_ref[0, :, vs] = jnp.where(lane < V_DIM, pair[0], pair[1]).astype(BF16)


def _attn_hist_call(q, kp, vp, kn, vn):
    B, T, _ = q.shape
    P = kp.shape[1]
    blk = lambda t, w: pl.BlockSpec((1, t, w), lambda b: (b, 0, 0))
    return pl.pallas_call(
        _attn_hist_kernel,
        grid=(B,),
        in_specs=[blk(T, QK_PAD), blk(P, QK_PAD), blk(P, V_ALL), blk(T, QK_PAD), blk(T, V_ALL)],
        out_specs=blk(T, V_ALL),
        out_shape=jax.ShapeDtypeStruct((B, T, V_ALL), BF16),
        compiler_params=pltpu.CompilerParams(dimension_semantics=("parallel",), vmem_limit_bytes=VMEM_LIMIT),
        name="attn_hist",
    )(q, kp, vp, kn, vn)


def _mlstm_kernel(q_ref, k_ref, v_ref, g_ref, so_ref, s0_ref, m0_ref, h_out, sfin_out, mfin_out,
                  s_scr, m_scr, *, TL):
    t = pl.program_id(1)
    L = CHUNK
    DH = MLSTM_DH

    @pl.when(t == 0)
    def _():
        s_scr[...] = s0_ref[0]
        m_scr[...] = m0_ref[0]

    r_i = lax.broadcasted_iota(jnp.int32, (L, L), 0)
    c_i = lax.broadcasted_iota(jnp.int32, (L, L), 1)
    causal = c_i <= r_i
    tri = jnp.where(causal, 1.0, 0.0).astype(BF16)
    lane = lax.broadcasted_iota(jnp.int32, (L, 128), 1)
    ones_blk = jnp.ones((L, DH), BF16)

    def chunk(c, carry):
        r0 = pl.multiple_of(c * L, L)
        G = g_ref[0, pl.ds(r0, L), :]
        g_hi = G.astype(BF16)
        g_r1 = G - g_hi.astype(F32)
        g_mid = g_r1.astype(BF16)
        g_lo = (g_r1 - g_mid.astype(F32)).astype(BF16)
        cum = _dot(tri, g_hi) + _dot(tri, g_mid) + _dot(tri, g_lo)
        VT = jnp.where(lane < MLSTM_HEADS, G, cum).T
        for h in range(MLSTM_HEADS):
            hs = slice(DH * h, DH * (h + 1))
            bcol = jnp.sum(jnp.where(lane == MLSTM_HEADS + h, cum, 0.0), axis=1, keepdims=True)
            igcol = jnp.sum(jnp.where(lane == h, G, 0.0), axis=1, keepdims=True)
            brow = VT[MLSTM_HEADS + h:MLSTM_HEADS + h + 1, :]
            igrow = VT[h:h + 1, :]
            m_prev = m_scr[h, 0:1, 0:1]
            d = jnp.where(causal, bcol - brow + igrow, NEG_INF)
            g = bcol + m_prev
            mt = jnp.maximum(g, jnp.max(d, axis=1, keepdims=True))
            inter = jnp.exp(g - mt)
            w = jnp.exp(d - mt)
            qh = q_ref[0, pl.ds(r0, L), hs]
            kh = k_ref[0, pl.ds(r0, L), hs]
            vh = v_ref[0, pl.ds(r0, L), hs]
            qk = _dot_nt(qh, kh) * w
            S = s_scr[h]
            sq = _dot_nt(qh, S.astype(BF16))
            vext = jnp.concatenate([vh, ones_blk], axis=1)
            numext = inter * sq + _dot(qk.astype(BF16), vext)
            den = jnp.maximum(jnp.abs(numext[:, DH:]), jnp.exp(-mt))
            hval = numext[:, :DH] / den
            h_out[0, pl.ds(r0, L), hs] = (so_ref[0, pl.ds(r0, L), hs].astype(F32) * hval).astype(BF16)
            mL = mt[L - 1:L, :]
            bL = bcol[L - 1:L, :]
            wscol = jnp.exp(bL - bcol + igcol - mL)
            decay = jnp.exp(bL + m_prev - mL)
            wvT = (wscol * vext.astype(F32)).T.astype(BF16)
            s_scr[h] = decay * S + _dot(wvT, kh)
            m_scr[h] = jnp.broadcast_to(mL, (8, 128))
        return carry

    lax.fori_loop(0, TL // L, chunk, 0)

    @pl.when(t == pl.num_programs(1) - 1)
    def _():
        sfin_out[0] = s_scr[...]
        mfin_out[0] = m_scr[...]


def _mlstm_call(mq, mk, mv, gates, so, s0, m0):
    B, T, _ = mq.shape
    TL = min(512, T)
    seq = lambda w: pl.BlockSpec((1, TL, w), lambda b, t: (b, t, 0))
    st = pl.BlockSpec((1, MLSTM_HEADS, 2 * MLSTM_DH, MLSTM_DH), lambda b, t: (b, 0, 0, 0))
    mst = pl.BlockSpec((1, MLSTM_HEADS, 8, 128), lambda b, t: (b, 0, 0, 0))
    return pl.pallas_call(
        functools.partial(_mlstm_kernel, TL=TL),
        grid=(B, T // TL),
        in_specs=[seq(MLSTM_WIDTH), seq(MLSTM_WIDTH), seq(MLSTM_WIDTH), seq(128), seq(MLSTM_WIDTH), st, mst],
        out_specs=[seq(MLSTM_WIDTH), st, mst],
        out_shape=[jax.ShapeDtypeStruct((B, T, MLSTM_WIDTH), BF16),
                   jax.ShapeDtypeStruct((B, MLSTM_HEADS, 2 * MLSTM_DH, MLSTM_DH), F32),
                   jax.ShapeDtypeStruct((B, MLSTM_HEADS, 8, 128), F32)],
        scratch_shapes=[pltpu.VMEM((MLSTM_HEADS, 2 * MLSTM_DH, MLSTM_DH), F32),
                        pltpu.VMEM((MLSTM_HEADS, 8, 128), F32)],
        compiler_params=pltpu.CompilerParams(dimension_semantics=("parallel", "arbitrary"),
                                             vmem_limit_bytes=VMEM_LIMIT),
        name="mlstm",
    )(mq, mk, mv, gates, so, s0, m0)


def _merge_kernel(x_ref, a_ref, b_ref, gu_ref, gv_ref, wgt_ref, bgt_ref, ws_ref, bsf_ref, wb_ref, wo_ref,
                  g1_ref, b1_ref, x1_out, c_scr, *, L, TM):
    x = x_ref[...]
    xb = x.astype(BF16)
    r_i = lax.broadcasted_iota(jnp.int32, (L, L), 0)
    c_i = lax.broadcasted_iota(jnp.int32, (L, L), 1)
    for g in range(GMLP_GROUPS):
        gs = slice(GMLP_DG * g, GMLP_DG * (g + 1))
        wsg = jnp.where(c_i <= r_i, ws_ref[g], 0.0).astype(BF16)
        for c in range(TM // L):
            rs = slice(L * c, L * (c + 1))
            sp = _dot(wsg, gv_ref[rs, gs]) + bsf_ref[:, gs]
            c_scr[rs, gs] = (gu_ref[rs, gs].astype(F32) * sp).astype(BF16)
    merged = None
    for kb, br in enumerate((a_ref, b_ref, c_scr)):
        cs = slice(D_MODEL * kb, D_MODEL * (kb + 1))
        gate = _sigmoid(_dot(xb, wgt_ref[:, cs]) + bgt_ref[:, cs])
        term = gate * _dot(br[...], wb_ref[kb])
        merged = term if merged is None else merged + term
    y = _dot(merged.astype(BF16), wo_ref[...])
    x1_out[...] = _layernorm(DN_ALPHA * x + y, g1_ref[...], b1_ref[...])


def _merge_call(x2d, a2d, b2d, gu, gv, mw, L):
    N = x2d.shape[0]
    TM = min(512, N)
    row = lambda w: pl.BlockSpec((TM, w), lambda i: (i, 0))
    return pl.pallas_call(
        functools.partial(_merge_kernel, L=L, TM=TM),
        grid=(N // TM,),
        in_specs=[row(D_MODEL), row(V_ALL), row(MLSTM_WIDTH), row(GMLP_WIDTH), row(GMLP_WIDTH),
                  _const_spec((D_MODEL, N_BRANCH * D_MODEL)), _const_spec((1, N_BRANCH * D_MODEL)),
                  _const_spec((GMLP_GROUPS, L, L)), _const_spec((L, GMLP_WIDTH)),
                  _const_spec((N_BRANCH, 512, D_MODEL)), _const_spec((D_MODEL, D_MODEL)),
                  _const_spec((1, D_MODEL)), _const_spec((1, D_MODEL))],
        out_specs=row(D_MODEL),
        out_shape=jax.ShapeDtypeStruct((N, D_MODEL), F32),
        scratch_shapes=[pltpu.VMEM((TM, GMLP_WIDTH), BF16)],
        compiler_params=pltpu.CompilerParams(dimension_semantics=("parallel",), vmem_limit_bytes=VMEM_LIMIT),
        name="merge",
    )(x2d, a2d, b2d, gu, gv, mw["wgt"], mw["bgt"], mw["ws"], mw["bsf"], mw["wb"], mw["wo"], mw["g1"], mw["b1"])


def _moe_kernel(x_ref, rwt_ref, rb_ref, wg_ref, wu_ref, wd_ref, g2_ref, b2_ref, x2_out, h_scr, *, TM):
    x = x_ref[...]
    xb = x.astype(BF16)
    s = _sigmoid(_dot_nt(rwt_ref[...], xb))
    sb = s + rb_ref[...]
    rows = [sb[e:e + 1, :] for e in range(N_EXPERTS)]
    srow = [s[e:e + 1, :] for e in range(N_EXPERTS)]
    gscore = []
    for g in range(N_GROUPS):
        mem = rows[EXPERTS_PER_GROUP * g:EXPERTS_PER_GROUP * (g + 1)]
        best = None
        for a in range(EXPERTS_PER_GROUP):
            for b in range(a + 1, EXPERTS_PER_GROUP):
                pr = mem[a] + mem[b]
                best = pr if best is None else jnp.maximum(best, pr)
        gscore.append(best)
    gmax = functools.reduce(jnp.maximum, gscore)
    taken = None
    gsel = []
    for g in range(N_GROUPS):
        hit = gscore[g] == gmax
        if taken is None:
            gsel.append(hit)
            taken = hit
        else:
            gsel.append(jnp.logical_and(hit, jnp.logical_not(taken)))
            taken = jnp.logical_or(taken, hit)
    sel_w = []
    for e in range(N_EXPERTS):
        g = e // EXPERTS_PER_GROUP
        rank = None
        for o in range(EXPERTS_PER_GROUP * g, EXPERTS_PER_GROUP * (g + 1)):
            if o == e:
                continue
            ahead = (rows[o] >= rows[e]) if o < e else (rows[o] > rows[e])
            ahead = jnp.where(ahead, 1.0, 0.0)
            rank = ahead if rank is None else rank + ahead
        chosen = jnp.logical_and(gsel[g], rank < 1.5)
        sel_w.append(jnp.where(chosen, srow[e], 0.0))
    den = functools.reduce(jnp.add, sel_w)
    gate_rows = [w_ / den for w_ in sel_w]
    gate_t = jnp.concatenate(gate_rows + [jnp.zeros((128 - N_EXPERTS, TM), F32)], axis=0)
    gate = gate_t.T
    for e in range(N_EXPERTS + 1):
        hcol = slice(D_EXPERT * e, D_EXPERT * (e + 1))
        hg = _dot(xb, wg_ref[:, hcol])
        hu = _dot(xb, wu_ref[:, hcol])
        hh = hg * _sigmoid(hg) * hu
        if e < N_EXPERTS:
            hh = hh * gate[:, e:e + 1]
        h_scr[:, hcol] = hh.astype(BF16)
    out = _dot(h_scr[...], wd_ref[...])
    x2_out[...] = _layernorm(DN_ALPHA * x + out, g2_ref[...], b2_ref[...])


def _moe_call(x2d, ew):
    N = x2d.shape[0]
    TM = min(512, N)
    HW = (N_EXPERTS + 1) * D_EXPERT
    row = lambda w: pl.BlockSpec((TM, w), lambda i: (i, 0))
    return pl.pallas_call(
        functools.partial(_moe_kernel, TM=TM),
        grid=(N // TM,),
        in_specs=[row(D_MODEL), _const_spec((N_EXPERTS, D_MODEL)), _const_spec((N_EXPERTS, 1)),
                  _const_spec((D_MODEL, HW)), _const_spec((D_MODEL, HW)), _const_spec((HW, D_MODEL)),
                  _const_spec((1, D_MODEL)), _const_spec((1, D_MODEL))],
        out_specs=row(D_MODEL),
        out_shape=jax.ShapeDtypeStruct((N, D_MODEL), F32),
        scratch_shapes=[pltpu.VMEM((TM, HW), BF16)],
        compiler_params=pltpu.CompilerParams(dimension_semantics=("parallel",), vmem_limit_bytes=VMEM_LIMIT),
        name="moe",
    )(x2d, ew["rwt"], ew["rb"], ew["wg"], ew["wu"], ew["wd"], ew["g2"], ew["b2"])


def _rot_cols(w):
    half = w.shape[-1] // 2
    return jnp.concatenate([-w[..., half:], w[..., :half]], axis=-1)


def _prep_layer(l, w_in, b_in, q_norm_g, kv_norm_g, w_uq, w_ukv, gmlp_ln_g, gmlp_ln_b, gmlp_ws, gmlp_bs,
                w_branch, w_out, ln1_g, ln1_b, moe_w_gate, moe_w_up, moe_w_down, shared_w_gate,
                shared_w_up, shared_w_down, ln2_g, ln2_b):
    wi = w_in[l]
    bi = b_in[l][None, :]

    def proj_cols(m):
        rows = m.shape[0]
        zero = lambda n: jnp.zeros((rows, n), m.dtype)
        kr = m[:, O_KR:O_MQ]
        kr128 = jnp.concatenate([zero(KR_LANE), kr, zero(HEAD_PAD - KR_LANE - ROPE_DIM)], axis=1)
        krr128 = jnp.concatenate([zero(KR_LANE), _rot_cols(kr), zero(HEAD_PAD - KR_LANE - ROPE_DIM)], axis=1)
        gates = jnp.concatenate([m[:, O_MI:O_MO], zero(HEAD_PAD - 2 * MLSTM_HEADS)], axis=1)
        return jnp.concatenate([m[:, O_CQ:O_KR], kr128, krr128, m[:, O_MQ:O_MI], gates, m[:, O_MO:O_GT]], axis=1)

    uq = w_uq[l].reshape(Q_RANK, MLA_HEADS, NOPE_DIM + ROPE_DIM)
    zq = lambda n: jnp.zeros((Q_RANK, MLA_HEADS, n), F32)
    pad = HEAD_PAD - NOPE_DIM - ROPE_DIM
    wq_a = jnp.concatenate([uq, zq(pad)], axis=-1).reshape(Q_RANK, QK_PAD)
    wq_b = jnp.concatenate([zq(NOPE_DIM), _rot_cols(uq[..., NOPE_DIM:]), zq(pad)], axis=-1).reshape(Q_RANK, QK_PAD)
    ukv = w_ukv[l].reshape(KV_RANK, MLA_HEADS, NOPE_DIM + V_DIM)
    wk = jnp.concatenate([ukv[..., :NOPE_DIM], jnp.zeros((KV_RANK, MLA_HEADS, HEAD_PAD - NOPE_DIM), F32)],
                         axis=-1).reshape(KV_RANK, QK_PAD)
    wv = ukv[..., NOPE_DIM:].reshape(KV_RANK, V_ALL)
    pw = dict(
        w1=proj_cols(wi).astype(BF16), b1=proj_cols(bi),
        qg=q_norm_g[l][None, :], kvg=kv_norm_g[l][None, :],
        wq=jnp.concatenate([wq_a, wq_b], axis=1).astype(BF16),
        wkv=jnp.concatenate([wk, wv], axis=1).astype(BF16),
        lng=gmlp_ln_g[l][None, :], lnb=gmlp_ln_b[l][None, :],
    )
    mw = dict(
        wgt=wi[:, O_GT:].astype(BF16), bgt=bi[:, O_GT:],
        ws_full=gmlp_ws[l], bs_full=gmlp_bs[l],
        wb=w_branch[l].astype(BF16), wo=w_out[l].astype(BF16),
        g1=ln1_g[l][None, :], b1=ln1_b[l][None, :],
    )
    cat_in = lambda we, ws_: jnp.concatenate(
        [jnp.transpose(we[l], (1, 0, 2)).reshape(D_MODEL, N_EXPERTS * D_EXPERT), ws_[l]], axis=1).astype(BF16)
    ew = dict(
        wg=cat_in(moe_w_gate, shared_w_gate), wu=cat_in(moe_w_up, shared_w_up),
        wd=jnp.concatenate([moe_w_down[l].reshape(N_EXPERTS * D_EXPERT, D_MODEL), shared_w_down[l]],
                           axis=0).astype(BF16),
        g2=ln2_g[l][None, :], b2=ln2_b[l][None, :],
    )
    return pw, mw, ew


def _rope_tables(T, past, rows):
    half = ROPE_DIM // 2
    pos = (past + jnp.arange(T)).astype(F32)
    inv = ROPE_THETA ** (-jnp.arange(half, dtype=F32) / half)
    ang = pos[:, None] * inv[None, :]
    cos = jnp.cos(ang)
    sin = jnp.sin(ang)
    c2 = jnp.concatenate([cos, cos], axis=1)
    s2 = jnp.concatenate([sin, sin], axis=1)
    z = lambda n: jnp.zeros((T, n), F32)
    tail = HEAD_PAD - NOPE_DIM - ROPE_DIM
    tabs = dict(
        cq=jnp.concatenate([jnp.ones((T, NOPE_DIM), F32), c2, z(tail)], axis=1) * ATTN_SCALE,
        sq=jnp.concatenate([z(NOPE_DIM), s2, z(tail)], axis=1) * ATTN_SCALE,
        ck=jnp.concatenate([z(KR_LANE), c2, z(tail)], axis=1),
        sk=jnp.concatenate([z(KR_LANE), s2, z(tail)], axis=1),
    )
    if rows > T:
        tabs = {k: jnp.tile(v, (rows // T, 1)) for k, v in tabs.items()}
    return tabs


def _trunk(x, hist, layers, rwt, rb):
    B, T, _ = x.shape
    N = B * T
    past = 0 if hist is None else hist[0].shape[2]
    tabs = _rope_tables(T, past, min(512, N))
    L = GMLP_CHUNK if T % GMLP_CHUNK == 0 else T
    x2d = x.reshape(N, D_MODEL)
    ckvs, krs, Cs, ns, ms, gvs = [], [], [], [], [], []
    for l, (pw, mw, ew) in enumerate(layers):
        outs = _proj_call(x2d, pw, tabs, T, hist is not None)
        q, k, v, ckvn, kr, mq, mk, mv, gates, so, gu, gv = outs[:12]
        seq = lambda a: a.reshape(B, T, a.shape[-1])
        if hist is None:
            a_out = _attn_call(seq(q), seq(k), seq(v))
            s0 = jnp.zeros((B, MLSTM_HEADS, 2 * MLSTM_DH, MLSTM_DH), F32)
            m0 = jnp.zeros((B, MLSTM_HEADS, 8, 128), F32)
        else:
            ckv_past, kr_past, c0, n0, m0_in = hist
            P = ckv_past.shape[2]
            krp = jnp.pad(kr_past[l].reshape(B * P, ROPE_DIM),
                          ((0, 0), (KR_LANE, HEAD_PAD - KR_LANE - ROPE_DIM)))
            kp, vp = _pastkv_call(ckv_past[l].reshape(B * P, KV_RANK), krp, pw["wkv"])
            a_out = _attn_hist_call(seq(q), kp.reshape(B, P, QK_PAD), vp.reshape(B, P, V_ALL), seq(k), seq(v))
            s0 = jnp.concatenate(
                [c0[l], jnp.broadcast_to(n0[l][:, :, None, :], (B, MLSTM_HEADS, MLSTM_DH, MLSTM_DH))], axis=2)
            m0 = jnp.broadcast_to(m0_in[l][:, :, None, None], (B, MLSTM_HEADS, 8, 128))
        b_out, sfin, mfin = _mlstm_call(seq(mq), seq(mk), seq(mv), seq(gates), seq(so), s0, m0)
        mwl = dict(mw)
        mwl["ws"] = mw["ws_full"][:, :L, :L]
        mwl["bsf"] = jnp.repeat(mw["bs_full"][:, :L].T, GMLP_DG, axis=1)
        x1 = _merge_call(x2d, a_out.reshape(N, V_ALL), b_out.reshape(N, MLSTM_WIDTH), gu, gv, mwl, L)
        x2d = _moe_call(x1, dict(ew, rwt=rwt, rb=rb))
        ckvs.append(ckvn.reshape(B, T, KV_RANK))
        krs.append(kr[:, KR_LANE:KR_LANE + ROPE_DIM].reshape(B, T, ROPE_DIM))
        Cs.append(sfin[:, :, :MLSTM_DH, :])
        ns.append(sfin[:, :, MLSTM_DH, :])
        ms.append(mfin[:, :, 0, 0])
        if hist is not None:
            gvs.append(outs[12].reshape(B, T, GMLP_WIDTH))
    res = [x2d.reshape(B, T, D_MODEL), jnp.stack(ckvs), jnp.stack(krs), jnp.stack(Cs), jnp.stack(ns), jnp.stack(ms)]
    if hist is not None:
        res.append(jnp.stack(gvs))
    return res


def kernel(x_prompt, x_sample, cache_mla_ckv, cache_mla_krope, state_mlstm_c, state_mlstm_n, state_mlstm_m,
           w_in, b_in, q_norm_g, kv_norm_g, w_uq, w_ukv, gmlp_ln_g, gmlp_ln_b, gmlp_ws, gmlp_bs, w_branch,
           w_out, ln1_g, ln1_b, router_w, router_b, moe_w_gate, moe_w_up, moe_w_down, shared_w_gate,
           shared_w_up, shared_w_down, ln2_g, ln2_b):
    depth = w_in.shape[0]
    layers = [_prep_layer(l, w_in, b_in, q_norm_g, kv_norm_g, w_uq, w_ukv, gmlp_ln_g, gmlp_ln_b, gmlp_ws,
                          gmlp_bs, w_branch, w_out, ln1_g, ln1_b, moe_w_gate, moe_w_up, moe_w_down,
                          shared_w_gate, shared_w_up, shared_w_down, ln2_g, ln2_b) for l in range(depth)]
    rwt = router_w.T.astype(BF16)
    rb = router_b[:, None]
    yp, p_ckv, p_kr, p_c, p_n, p_m = _trunk(x_prompt, None, layers, rwt, rb)
    ys, s_ckv, s_kr, s_c, s_n, s_m, s_gv = _trunk(
        x_sample, (cache_mla_ckv, cache_mla_krope, state_mlstm_c, state_mlstm_n, state_mlstm_m), layers, rwt, rb)
    return (yp, ys, p_ckv, p_kr, p_c, p_n, p_m, s_ckv, s_kr, s_c, s_n, s_m, s_gv)
```

```python
import functools

import jax
import jax.numpy as jnp
import numpy as np
from jax import lax
from jax.experimental import pallas as pl
from jax.experimental.pallas import tpu as pltpu

F32 = jnp.float32
BF16 = jnp.bfloat16

D_MODEL = 1024
CHUNK = 64
MLA_HEADS = 8
Q_RANK = 256
KV_RANK = 256
NOPE_DIM = 64
ROPE_DIM = 32
V_DIM = 64
ROPE_THETA = 10000.0
ATTN_SCALE = (NOPE_DIM + ROPE_DIM) ** -0.5
Q_SCALE = ATTN_SCALE * float(np.log2(np.e))
MLSTM_HEADS = 4
MLSTM_DH = 128
MLSTM_WIDTH = MLSTM_HEADS * MLSTM_DH
GMLP_GROUPS = 4
GMLP_DG = 128
GMLP_WIDTH = GMLP_GROUPS * GMLP_DG
GMLP_CHUNK = 128
N_BRANCH = 3
N_EXPERTS = 16
N_GROUPS = 4
EXPERTS_PER_GROUP = N_EXPERTS // N_GROUPS
D_EXPERT = 256
D_SHARED = 256
DEPTH = 4
DN_ALPHA = (2 * DEPTH) ** 0.25
EPS = 1e-5

HEAD_PAD = 128
QK_PAD = MLA_HEADS * HEAD_PAD
V_ALL = MLA_HEADS * V_DIM
P_CQ, P_CKV, P_KR, P_KRR, P_MQ, P_MK, P_MV, P_G, P_MO, P_GU, P_GV = (
    0, 256, 512, 640, 768, 1280, 1792, 2304, 2432, 2944, 3456)
D_PROJ = 3968
O_CQ, O_CKV, O_KR, O_MQ, O_MK, O_MV, O_MI, O_MF, O_MO, O_GU, O_GV, O_GT = (
    0, 256, 512, 544, 1056, 1568, 2080, 2084, 2088, 2600, 3112, 3624)
KR_LANE = NOPE_DIM
VMEM_LIMIT = 56 * 1024 * 1024
NEG_INF = float("-inf")


def _dot(a, b):
    return jnp.dot(a, b, preferred_element_type=F32)


def _dot_nt(a, b):
    return lax.dot_general(a, b, (((1,), (1,)), ((), ())), preferred_element_type=F32)


def _const_spec(shape):
    nd = len(shape)
    return pl.BlockSpec(shape, lambda *_: (0,) * nd, pipeline_mode=pl.Buffered(1))


def _layernorm(x, g, b):
    mu = jnp.mean(x, axis=-1, keepdims=True)
    xc = x - mu
    var = jnp.mean(xc * xc, axis=-1, keepdims=True)
    return xc * lax.rsqrt(var + EPS) * g + b


def _rmsnorm(x, g):
    return x * lax.rsqrt(jnp.mean(x * x, axis=-1, keepdims=True) + EPS) * g


def _gelu(x):
    return 0.5 * x * (1.0 + jnp.tanh(np.sqrt(2.0 / np.pi).astype(np.float32) * (x + 0.044715 * (x * x * x))))


def _sigmoid(x):
    return 1.0 / (1.0 + jnp.exp(-x))


def _log_sigmoid(x):
    return jnp.minimum(x, 0.0) - jnp.log(1.0 + jnp.exp(-jnp.abs(x)))


def _proj_kernel(x_ref, w1_ref, b1_ref, qg_ref, kvg_ref, wq_ref, wkv_ref, cq_ref, sq_ref, ck_ref, sk_ref,
                 lng_ref, lnb_ref,
                 q_out, k_out, v_out, ckv_out, kr_out, mq_out, mk_out, mv_out, g_out, so_out, gu_out,
                 gv_out, *rest):
    xb = x_ref[...].astype(BF16)

    def z(lo, hi):
        return _dot(xb, w1_ref[:, lo:hi]) + b1_ref[:, lo:hi]

    zc = z(P_CQ, P_MQ)
    cqn = _rmsnorm(zc[:, P_CQ:P_CKV], qg_ref[...])
    ckvn = _rmsnorm(zc[:, P_CKV:P_KR], kvg_ref[...])
    kr = zc[:, P_KR:P_KRR] * ck_ref[...] + zc[:, P_KRR:P_MQ] * sk_ref[...]
    ckv_out[...] = ckvn
    kr_out[...] = kr
    qq = _dot(cqn.astype(BF16), wq_ref[...])
    cos8 = jnp.concatenate([cq_ref[...]] * MLA_HEADS, axis=1)
    sin8 = jnp.concatenate([sq_ref[...]] * MLA_HEADS, axis=1)
    q_out[...] = (qq[:, :QK_PAD] * cos8 + qq[:, QK_PAD:] * sin8).astype(BF16)
    kk = _dot(ckvn.astype(BF16), wkv_ref[...])
    kr8 = jnp.concatenate([kr] * MLA_HEADS, axis=1)
    k_out[...] = (kk[:, :QK_PAD] + kr8).astype(BF16)
    v_out[...] = kk[:, QK_PAD:].astype(BF16)

    zm = z(P_MQ, P_MO)
    mq_out[...] = zm[:, 0:512].astype(BF16)
    mk_out[...] = (zm[:, 512:1024] * (MLSTM_DH ** -0.5)).astype(BF16)
    mv_out[...] = zm[:, 1024:1536].astype(BF16)
    zg = zm[:, 1536:1664]
    lane = lax.broadcasted_iota(jnp.int32, zg.shape, 1)
    g_out[...] = jnp.where(lane < MLSTM_HEADS, zg, _log_sigmoid(zg))
    so_out[...] = _sigmoid(z(P_MO, P_GU)).astype(BF16)

    zu = z(P_GU, D_PROJ)
    gu_out[...] = _gelu(zu[:, :GMLP_WIDTH]).astype(BF16)
    vrows = _layernorm(_gelu(zu[:, GMLP_WIDTH:]), lng_ref[...], lnb_ref[...])
    gv_out[...] = vrows.astype(BF16)
    if rest:
        rest[0][...] = vrows


def _proj_call(x2d, pw, tabs, T, want_vrows):
    N = x2d.shape[0]
    TM = min(512, N)
    nt = max(1, T // TM)
    grid = (N // TM,)
    row = lambda w: pl.BlockSpec((TM, w), lambda i: (i, 0))
    tab = pl.BlockSpec((TM, HEAD_PAD), lambda i: (i % nt, 0))
    in_specs = [
        row(D_MODEL),
        _const_spec((D_MODEL, D_PROJ)), _const_spec((1, D_PROJ)),
        _const_spec((1, Q_RANK)), _const_spec((1, KV_RANK)),
        _const_spec((Q_RANK, 2 * QK_PAD)), _const_spec((KV_RANK, QK_PAD + V_ALL)),
        tab, tab, tab, tab,
        _const_spec((1, GMLP_WIDTH)), _const_spec((1, GMLP_WIDTH)),
    ]
    out_shapes = [
        ((N, QK_PAD), BF16), ((N, QK_PAD), BF16), ((N, V_ALL), BF16), ((N, KV_RANK), F32),
        ((N, HEAD_PAD), F32), ((N, MLSTM_WIDTH), BF16), ((N, MLSTM_WIDTH), BF16), ((N, MLSTM_WIDTH), BF16),
        ((N, HEAD_PAD), F32), ((N, MLSTM_WIDTH), BF16), ((N, GMLP_WIDTH), BF16), ((N, GMLP_WIDTH), BF16),
    ]
    if want_vrows:
        out_shapes.append(((N, GMLP_WIDTH), F32))
    return pl.pallas_call(
        _proj_kernel,
        grid=grid,
        in_specs=in_specs,
        out_specs=[row(s[1]) for s, _ in out_shapes],
        out_shape=[jax.ShapeDtypeStruct(s, d) for s, d in out_shapes],
        compiler_params=pltpu.CompilerParams(dimension_semantics=("parallel",), vmem_limit_bytes=VMEM_LIMIT),
        name="proj",
    )(x2d, pw["w1"], pw["b1"], pw["qg"], pw["kvg"], pw["wq"], pw["wkv"],
      tabs["cq"], tabs["sq"], tabs["ck"], tabs["sk"], pw["lng"], pw["lnb"])


def _pastkv_kernel(c_ref, kr_ref, wkv_ref, k_out, v_out):
    kk = _dot(c_ref[...].astype(BF16), wkv_ref[...])
    kr8 = jnp.concatenate([kr_ref[...]] * MLA_HEADS, axis=1)
    k_out[...] = (kk[:, :QK_PAD] + kr8).astype(BF16)
    v_out[...] = kk[:, QK_PAD:].astype(BF16)


def _pastkv_call(ckv2d, kr2d, wkv):
    N = ckv2d.shape[0]
    TM = min(1024, N)
    row = lambda w: pl.BlockSpec((TM, w), lambda i: (i, 0))
    return pl.pallas_call(
        _pastkv_kernel,
        grid=(N // TM,),
        in_specs=[row(KV_RANK), row(HEAD_PAD), _const_spec((KV_RANK, QK_PAD + V_ALL))],
        out_specs=[row(QK_PAD), row(V_ALL)],
        out_shape=[jax.ShapeDtypeStruct((N, QK_PAD), BF16), jax.ShapeDtypeStruct((N, V_ALL), BF16)],
        compiler_params=pltpu.CompilerParams(dimension_semantics=("parallel",), vmem_limit_bytes=VMEM_LIMIT),
        name="pastkv",
    )(ckv2d, kr2d, wkv)


def _attn_kernel(q_ref, k_ref, v_ref, o_ref, mx_scr, ls_scr, acc_scr, *, TQ):
    i = pl.program_id(1)
    rc = lax.broadcasted_iota(jnp.int32, (TQ, TQ), 0) // CHUNK
    cc = lax.broadcasted_iota(jnp.int32, (TQ, TQ), 1) // CHUNK
    visible = cc <= rc
    lane = lax.broadcasted_iota(jnp.int32, (TQ, HEAD_PAD), 1)
    diag0 = pl.multiple_of(i * TQ, TQ)
    nlane = TQ // 128

    def fold(a, op):
        r = a[:, 0:128]
        for t in range(1, nlane):
            r = op(r, a[:, 128 * t:128 * (t + 1)])
        return r

    def scores(h, r0, masked):
        hs = slice(HEAD_PAD * h, HEAD_PAD * (h + 1))
        s = _dot_nt(q_ref[0, :, hs], k_ref[0, pl.ds(r0, TQ), hs])
        return jnp.where(visible, s, NEG_INF) if masked else s

    def sweep_max(r0, masked, first):
        for h in range(MLA_HEADS):
            mx = fold(scores(h, r0, masked), jnp.maximum)
            mx_scr[h] = mx if first else jnp.maximum(mx_scr[h], mx)

    def sweep_pv(r0, masked, first):
        for h in range(MLA_HEADS):
            vs = slice(HEAD_PAD * (h // 2), HEAD_PAD * (h // 2 + 1))
            mb = mx_scr[h]
            p = jnp.exp2(scores(h, r0, masked) - jnp.concatenate([mb] * nlane, axis=1))
            ls = fold(p, jnp.add)
            pv = _dot(p.astype(BF16), v_ref[0, pl.ds(r0, TQ), vs])
            ls_scr[h] = ls if first else ls_scr[h] + ls
            acc_scr[h] = pv if first else acc_scr[h] + pv

    def body_max(j, c):
        sweep_max(pl.multiple_of(j * TQ, TQ), False, False)
        return c

    def body_pv(j, c):
        sweep_pv(pl.multiple_of(j * TQ, TQ), False, False)
        return c

    sweep_max(diag0, True, True)
    lax.fori_loop(0, i, body_max, 0)
    for h in range(MLA_HEADS):
        mx_scr[h] = jnp.broadcast_to(jnp.max(mx_scr[h], axis=-1, keepdims=True), (TQ, 128))
    sweep_pv(diag0, True, True)
    lax.fori_loop(0, i, body_pv, 0)
    for p in range(MLA_HEADS // 2):
        o0 = acc_scr[2 * p] / jnp.sum(ls_scr[2 * p], axis=-1, keepdims=True)
        o1 = acc_scr[2 * p + 1] / jnp.sum(ls_scr[2 * p + 1], axis=-1, keepdims=True)
        o_ref[0, :, HEAD_PAD * p:HEAD_PAD * (p + 1)] = jnp.where(lane < V_DIM, o0, o1).astype(BF16)


def _attn_call(q, k, v):
    B, T, _ = q.shape
    TQ = min(256, T)
    nq = T // TQ
    return pl.pallas_call(
        functools.partial(_attn_kernel, TQ=TQ),
        grid=(B, nq),
        in_specs=[pl.BlockSpec((1, TQ, QK_PAD), lambda b, i: (b, i, 0)),
                  pl.BlockSpec((1, T, QK_PAD), lambda b, i: (b, 0, 0)),
                  pl.BlockSpec((1, T, V_ALL), lambda b, i: (b, 0, 0))],
        out_specs=pl.BlockSpec((1, TQ, V_ALL), lambda b, i: (b, i, 0)),
        out_shape=jax.ShapeDtypeStruct((B, T, V_ALL), BF16),
        scratch_shapes=[pltpu.VMEM((MLA_HEADS, TQ, 128), F32)] * 3,
        compiler_params=pltpu.CompilerParams(dimension_semantics=("parallel", "arbitrary"),
                                             vmem_limit_bytes=VMEM_LIMIT),
        name="attn",
    )(q, k, v)


def _attn_hist_kernel(q_ref, kp_ref, vp_ref, kn_ref, vn_ref, o_ref):
    T = q_ref.shape[1]
    lane = lax.broadcasted_iota(jnp.int32, (T, HEAD_PAD), 1)
    for p in range(MLA_HEADS // 2):
        pair = []
        vs = slice(HEAD_PAD * p, HEAD_PAD * (p + 1))
        for hh in range(2):
            h = 2 * p + hh
            hs = slice(HEAD_PAD * h, HEAD_PAD * (h + 1))
            qh = q_ref[0, :, hs]
            s1 = _dot_nt(qh, kp_ref[0, :, hs])
            s2 = _dot_nt(qh, kn_ref[0, :, hs])
            m = jnp.maximum(jnp.max(s1, axis=-1, keepdims=True), jnp.max(s2, axis=-1, keepdims=True))
            p1 = jnp.exp2(s1 - m)
            p2 = jnp.exp2(s2 - m)
            l = jnp.sum(p1, axis=-1, keepdims=True) + jnp.sum(p2, axis=-1, keepdims=True)
            acc = _dot(p1.astype(BF16), vp_ref[0, :, vs]) + _dot(p2.astype(BF16), vn_ref[0, :, vs])
            pair.append(acc / l)
        o_ref[0, :, vs] = jnp.where(lane < V_DIM, pair[0], pair[1]).astype(BF16)


def _attn_hist_call(q, kp, vp, kn, vn):
    B, T, _ = q.shape
    P = kp.shape[1]
    blk = lambda t, w: pl.BlockSpec((1, t, w), lambda b: (b, 0, 0))
    return pl.pallas_call(
        _attn_hist_kernel,
        grid=(B,),
        in_specs=[blk(T, QK_PAD), blk(P, QK_PAD), blk(P, V_ALL), blk(T, QK_PAD), blk(T, V_ALL)],
        out_specs=blk(T, V_ALL),
        out_shape=jax.ShapeDtypeStruct((B, T, V_ALL), BF16),
        compiler_params=pltpu.CompilerParams(dimension_semantics=("parallel",), vmem_limit_bytes=VMEM_LIMIT),
        name="attn_hist",
    )(q, kp, vp, kn, vn)


def _mlstm_kernel(q_ref, k_ref, v_ref, g_ref, so_ref, s0_ref, m0_ref, h_out, sfin_out, mfin_out,
                  s_scr, m_scr, *, TL):
    t = pl.program_id(1)
    L = CHUNK
    DH = MLSTM_DH

    @pl.when(t == 0)
    def _():
        s_scr[...] = s0_ref[0]
        m_scr[...] = m0_ref[0]

    r_i = lax.broadcasted_iota(jnp.int32, (L, L), 0)
    c_i = lax.broadcasted_iota(jnp.int32, (L, L), 1)
    causal = c_i <= r_i
    tri = jnp.where(causal, 1.0, 0.0).astype(BF16)
    lane = lax.broadcasted_iota(jnp.int32, (L, 128), 1)
    ones_blk = jnp.ones((L, DH), BF16)

    def chunk(c, carry):
        r0 = pl.multiple_of(c * L, L)
        G = g_ref[0, pl.ds(r0, L), :]
        g_hi = G.astype(BF16)
        g_r1 = G - g_hi.astype(F32)
        g_mid = g_r1.astype(BF16)
        g_lo = (g_r1 - g_mid.astype(F32)).astype(BF16)
        cum = _dot(tri, g_hi) + _dot(tri, g_mid) + _dot(tri, g_lo)
        VT = jnp.where(lane < MLSTM_HEADS, G, cum).T
        for h in range(MLSTM_HEADS):
            hs = slice(DH * h, DH * (h + 1))
            bcol = jnp.sum(jnp.where(lane == MLSTM_HEADS + h, cum, 0.0), axis=1, keepdims=True)
            igcol = jnp.sum(jnp.where(lane == h, G, 0.0), axis=1, keepdims=True)
            brow = VT[MLSTM_HEADS + h:MLSTM_HEADS + h + 1, :]
            igrow = VT[h:h + 1, :]
            m_prev = m_scr[h, 0:1, 0:1]
            d = jnp.where(causal, bcol - brow + igrow, NEG_INF)
            g = bcol + m_prev
            mt = jnp.maximum(g, jnp.max(d, axis=1, keepdims=True))
            inter = jnp.exp(g - mt)
            w = jnp.exp(d - mt)
            qh = q_ref[0, pl.ds(r0, L), hs]
            kh = k_ref[0, pl.ds(r0, L), hs]
            vh = v_ref[0, pl.ds(r0, L), hs]
            qk = _dot_nt(qh, kh) * w
            S = s_scr[h]
            sq = _dot_nt(qh, S.astype(BF16))
            vext = jnp.concatenate([vh, ones_blk], axis=1)
            numext = inter * sq + _dot(qk.astype(BF16), vext)
            den = jnp.maximum(jnp.abs(numext[:, DH:]), jnp.exp(-mt))
            hval = numext[:, :DH] / den
            h_out[0, pl.ds(r0, L), hs] = (so_ref[0, pl.ds(r0, L), hs].astype(F32) * hval).astype(BF16)
            mL = mt[L - 1:L, :]
            bL = bcol[L - 1:L, :]
            wscol = jnp.exp(bL - bcol + igcol - mL)
            decay = jnp.exp(bL + m_prev - mL)
            wvT = (wscol * vext.astype(F32)).T.astype(BF16)
            s_scr[h] = decay * S + _dot(wvT, kh)
            m_scr[h] = jnp.broadcast_to(mL, (8, 128))
        return carry

    lax.fori_loop(0, TL // L, chunk, 0)

    @pl.when(t == pl.num_programs(1) - 1)
    def _():
        sfin_out[0] = s_scr[...]
        mfin_out[0] = m_scr[...]


def _mlstm_call(mq, mk, mv, gates, so, s0, m0):
    B, T, _ = mq.shape
    TL = min(512, T)
    seq = lambda w: pl.BlockSpec((1, TL, w), lambda b, t: (b, t, 0))
    st = pl.BlockSpec((1, MLSTM_HEADS, 2 * MLSTM_DH, MLSTM_DH), lambda b, t: (b, 0, 0, 0))
    mst = pl.BlockSpec((1, MLSTM_HEADS, 8, 128), lambda b, t: (b, 0, 0, 0))
    return pl.pallas_call(
        functools.partial(_mlstm_kernel, TL=TL),
        grid=(B, T // TL),
        in_specs=[seq(MLSTM_WIDTH), seq(MLSTM_WIDTH), seq(MLSTM_WIDTH), seq(128), seq(MLSTM_WIDTH), st, mst],
        out_specs=[seq(MLSTM_WIDTH), st, mst],
        out_shape=[jax.ShapeDtypeStruct((B, T, MLSTM_WIDTH), BF16),
                   jax.ShapeDtypeStruct((B, MLSTM_HEADS, 2 * MLSTM_DH, MLSTM_DH), F32),
                   jax.ShapeDtypeStruct((B, MLSTM_HEADS, 8, 128), F32)],
        scratch_shapes=[pltpu.VMEM((MLSTM_HEADS, 2 * MLSTM_DH, MLSTM_DH), F32),
                        pltpu.VMEM((MLSTM_HEADS, 8, 128), F32)],
        compiler_params=pltpu.CompilerParams(dimension_semantics=("parallel", "arbitrary"),
                                             vmem_limit_bytes=VMEM_LIMIT),
        name="mlstm",
    )(mq, mk, mv, gates, so, s0, m0)


def _merge_kernel(x_ref, a_ref, b_ref, gu_ref, gv_ref, wgt_ref, bgt_ref, ws_ref, bsf_ref, wb_ref, wo_ref,
                  g1_ref, b1_ref, x1_out, c_scr, *, L, TM):
    x = x_ref[...]
    xb = x.astype(BF16)
    r_i = lax.broadcasted_iota(jnp.int32, (L, L), 0)
    c_i = lax.broadcasted_iota(jnp.int32, (L, L), 1)
    for g in range(GMLP_GROUPS):
        gs = slice(GMLP_DG * g, GMLP_DG * (g + 1))
        wsg = jnp.where(c_i <= r_i, ws_ref[g], 0.0).astype(BF16)
        for c in range(TM // L):
            rs = slice(L * c, L * (c + 1))
            sp = _dot(wsg, gv_ref[rs, gs]) + bsf_ref[:, gs]
            c_scr[rs, gs] = (gu_ref[rs, gs].astype(F32) * sp).astype(BF16)
    merged = None
    for kb, br in enumerate((a_ref, b_ref, c_scr)):
        cs = slice(D_MODEL * kb, D_MODEL * (kb + 1))
        gate = _sigmoid(_dot(xb, wgt_ref[:, cs]) + bgt_ref[:, cs])
        term = gate * _dot(br[...], wb_ref[kb])
        merged = term if merged is None else merged + term
    y = _dot(merged.astype(BF16), wo_ref[...])
    x1_out[...] = _layernorm(DN_ALPHA * x + y, g1_ref[...], b1_ref[...])


def _merge_call(x2d, a2d, b2d, gu, gv, mw, L):
    N = x2d.shape[0]
    TM = min(512, N)
    row = lambda w: pl.BlockSpec((TM, w), lambda i: (i, 0))
    return pl.pallas_call(
        functools.partial(_merge_kernel, L=L, TM=TM),
        grid=(N // TM,),
        in_specs=[row(D_MODEL), row(V_ALL), row(MLSTM_WIDTH), row(GMLP_WIDTH), row(GMLP_WIDTH),
                  _const_spec((D_MODEL, N_BRANCH * D_MODEL)), _const_spec((1, N_BRANCH * D_MODEL)),
                  _const_spec((GMLP_GROUPS, L, L)), _const_spec((L, GMLP_WIDTH)),
                  _const_spec((N_BRANCH, 512, D_MODEL)), _const_spec((D_MODEL, D_MODEL)),
                  _const_spec((1, D_MODEL)), _const_spec((1, D_MODEL))],
        out_specs=row(D_MODEL),
        out_shape=jax.ShapeDtypeStruct((N, D_MODEL), F32),
        scratch_shapes=[pltpu.VMEM((TM, GMLP_WIDTH), BF16)],
        compiler_params=pltpu.CompilerParams(dimension_semantics=("parallel",), vmem_limit_bytes=VMEM_LIMIT),
        name="merge",
    )(x2d, a2d, b2d, gu, gv, mw["wgt"], mw["bgt"], mw["ws"], mw["bsf"], mw["wb"], mw["wo"], mw["g1"], mw["b1"])


def _moe_kernel(x_ref, rwt_ref, rb_ref, wg_ref, wu_ref, wd_ref, g2_ref, b2_ref, x2_out, h_scr, *, TM):
    x = x_ref[...]
    xb = x.astype(BF16)
    s = _sigmoid(_dot_nt(rwt_ref[...], xb))
    sb = s + rb_ref[...]
    rows = [sb[e:e + 1, :] for e in range(N_EXPERTS)]
    srow = [s[e:e + 1, :] for e in range(N_EXPERTS)]
    gscore = []
    for g in range(N_GROUPS):
        mem = rows[EXPERTS_PER_GROUP * g:EXPERTS_PER_GROUP * (g + 1)]
        best = None
        for a in range(EXPERTS_PER_GROUP):
            for b in range(a + 1, EXPERTS_PER_GROUP):
                pr = mem[a] + mem[b]
                best = pr if best is None else jnp.maximum(best, pr)
        gscore.append(best)
    gmax = functools.reduce(jnp.maximum, gscore)
    taken = None
    gsel = []
    for g in range(N_GROUPS):
        hit = gscore[g] == gmax
        if taken is None:
            gsel.append(hit)
            taken = hit
        else:
            gsel.append(jnp.logical_and(hit, jnp.logical_not(taken)))
            taken = jnp.logical_or(taken, hit)
    sel_w = []
    for e in range(N_EXPERTS):
        g = e // EXPERTS_PER_GROUP
        rank = None
        for o in range(EXPERTS_PER_GROUP * g, EXPERTS_PER_GROUP * (g + 1)):
            if o == e:
                continue
            ahead = (rows[o] >= rows[e]) if o < e else (rows[o] > rows[e])
            ahead = jnp.where(ahead, 1.0, 0.0)
            rank = ahead if rank is None else rank + ahead
        chosen = jnp.logical_and(gsel[g], rank < 1.5)
        sel_w.append(jnp.where(chosen, srow[e], 0.0))
    den = functools.reduce(jnp.add, sel_w)
    gate_rows = [w_ / den for w_ in sel_w]
    gate_t = jnp.concatenate(gate_rows + [jnp.zeros((128 - N_EXPERTS, TM), F32)], axis=0)
    gate = gate_t.T
    for e in range(N_EXPERTS + 1):
        hcol = slice(D_EXPERT * e, D_EXPERT * (e + 1))
        hg = _dot(xb, wg_ref[:, hcol])
        hu = _dot(xb, wu_ref[:, hcol])
        hh = hg * _sigmoid(hg) * hu
        if e < N_EXPERTS:
            hh = hh * gate[:, e:e + 1]
        h_scr[:, hcol] = hh.astype(BF16)
    out = _dot(h_scr[...], wd_ref[...])
    x2_out[...] = _layernorm(DN_ALPHA * x + out, g2_ref[...], b2_ref[...])


def _moe_call(x2d, ew):
    N = x2d.shape[0]
    TM = min(512, N)
    HW = (N_EXPERTS + 1) * D_EXPERT
    row = lambda w: pl.BlockSpec((TM, w), lambda i: (i, 0))
    return pl.pallas_call(
        functools.partial(_moe_kernel, TM=TM),
        grid=(N // TM,),
        in_specs=[row(D_MODEL), _const_spec((N_EXPERTS, D_MODEL)), _const_spec((N_EXPERTS, 1)),
                  _const_spec((D_MODEL, HW)), _const_spec((D_MODEL, HW)), _const_spec((HW, D_MODEL)),
                  _const_spec((1, D_MODEL)), _const_spec((1, D_MODEL))],
        out_specs=row(D_MODEL),
        out_shape=jax.ShapeDtypeStruct((N, D_MODEL), F32),
        scratch_shapes=[pltpu.VMEM((TM, HW), BF16)],
        compiler_params=pltpu.CompilerParams(dimension_semantics=("parallel",), vmem_limit_bytes=VMEM_LIMIT),
        name="moe",
    )(x2d, ew["rwt"], ew["rb"], ew["wg"], ew["wu"], ew["wd"], ew["g2"], ew["b2"])


def _rot_cols(w):
    half = w.shape[-1] // 2
    return jnp.concatenate([-w[..., half:], w[..., :half]], axis=-1)


def _prep_layer(l, w_in, b_in, q_norm_g, kv_norm_g, w_uq, w_ukv, gmlp_ln_g, gmlp_ln_b, gmlp_ws, gmlp_bs,
                w_branch, w_out, ln1_g, ln1_b, moe_w_gate, moe_w_up, moe_w_down, shared_w_gate,
                shared_w_up, shared_w_down, ln2_g, ln2_b):
    wi = w_in[l]
    bi = b_in[l][None, :]

    def proj_cols(m):
        rows = m.shape[0]
        zero = lambda n: jnp.zeros((rows, n), m.dtype)
        kr = m[:, O_KR:O_MQ]
        kr128 = jnp.concatenate([zero(KR_LANE), kr, zero(HEAD_PAD - KR_LANE - ROPE_DIM)], axis=1)
        krr128 = jnp.concatenate([zero(KR_LANE), _rot_cols(kr), zero(HEAD_PAD - KR_LANE - ROPE_DIM)], axis=1)
        gates = jnp.concatenate([m[:, O_MI:O_MO], zero(HEAD_PAD - 2 * MLSTM_HEADS)], axis=1)
        return jnp.concatenate([m[:, O_CQ:O_KR], kr128, krr128, m[:, O_MQ:O_MI], gates, m[:, O_MO:O_GT]], axis=1)

    uq = w_uq[l].reshape(Q_RANK, MLA_HEADS, NOPE_DIM + ROPE_DIM)
    zq = lambda n: jnp.zeros((Q_RANK, MLA_HEADS, n), F32)
    pad = HEAD_PAD - NOPE_DIM - ROPE_DIM
    wq_a = jnp.concatenate([uq, zq(pad)], axis=-1).reshape(Q_RANK, QK_PAD)
    wq_b = jnp.concatenate([zq(NOPE_DIM), _rot_cols(uq[..., NOPE_DIM:]), zq(pad)], axis=-1).reshape(Q_RANK, QK_PAD)
    ukv = w_ukv[l].reshape(KV_RANK, MLA_HEADS, NOPE_DIM + V_DIM)
    wk = jnp.concatenate([ukv[..., :NOPE_DIM], jnp.zeros((KV_RANK, MLA_HEADS, HEAD_PAD - NOPE_DIM), F32)],
                         axis=-1).reshape(KV_RANK, QK_PAD)
    wv = ukv[..., NOPE_DIM:].reshape(KV_RANK, V_ALL)
    pw = dict(
        w1=proj_cols(wi).astype(BF16), b1=proj_cols(bi),
        qg=q_norm_g[l][None, :], kvg=kv_norm_g[l][None, :],
        wq=jnp.concatenate([wq_a, wq_b], axis=1).astype(BF16),
        wkv=jnp.concatenate([wk, wv], axis=1).astype(BF16),
        lng=gmlp_ln_g[l][None, :], lnb=gmlp_ln_b[l][None, :],
    )
    mw = dict(
        wgt=wi[:, O_GT:].astype(BF16), bgt=bi[:, O_GT:],
        ws_full=gmlp_ws[l], bs_full=gmlp_bs[l],
        wb=w_branch[l].astype(BF16), wo=w_out[l].astype(BF16),
        g1=ln1_g[l][None, :], b1=ln1_b[l][None, :],
    )
    cat_in = lambda we, ws_: jnp.concatenate(
        [jnp.transpose(we[l], (1, 0, 2)).reshape(D_MODEL, N_EXPERTS * D_EXPERT), ws_[l]], axis=1).astype(BF16)
    ew = dict(
        wg=cat_in(moe_w_gate, shared_w_gate), wu=cat_in(moe_w_up, shared_w_up),
        wd=jnp.concatenate([moe_w_down[l].reshape(N_EXPERTS * D_EXPERT, D_MODEL), shared_w_down[l]],
                           axis=0).astype(BF16),
        g2=ln2_g[l][None, :], b2=ln2_b[l][None, :],
    )
    return pw, mw, ew


def _rope_tables(T, past, rows):
    half = ROPE_DIM // 2
    pos = (past + jnp.arange(T)).astype(F32)
    inv = ROPE_THETA ** (-jnp.arange(half, dtype=F32) / half)
    ang = pos[:, None] * inv[None, :]
    cos = jnp.cos(ang)
    sin = jnp.sin(ang)
    c2 = jnp.concatenate([cos, cos], axis=1)
    s2 = jnp.concatenate([sin, sin], axis=1)
    z = lambda n: jnp.zeros((T, n), F32)
    tail = HEAD_PAD - NOPE_DIM - ROPE_DIM
    tabs = dict(
        cq=jnp.concatenate([jnp.ones((T, NOPE_DIM), F32), c2, z(tail)], axis=1) * Q_SCALE,
        sq=jnp.concatenate([z(NOPE_DIM), s2, z(tail)], axis=1) * Q_SCALE,
        ck=jnp.concatenate([z(KR_LANE), c2, z(tail)], axis=1),
        sk=jnp.concatenate([z(KR_LANE), s2, z(tail)], axis=1),
    )
    if rows > T:
        tabs = {k: jnp.tile(v, (rows // T, 1)) for k, v in tabs.items()}
    return tabs


def _trunk(x, hist, layers, rwt, rb):
    B, T, _ = x.shape
    N = B * T
    past = 0 if hist is None else hist[0].shape[2]
    tabs = _rope_tables(T, past, min(512, N))
    L = GMLP_CHUNK if T % GMLP_CHUNK == 0 else T
    x2d = x.reshape(N, D_MODEL)
    ckvs, krs, Cs, ns, ms, gvs = [], [], [], [], [], []
    for l, (pw, mw, ew) in enumerate(layers):
        outs = _proj_call(x2d, pw, tabs, T, hist is not None)
        q, k, v, ckvn, kr, mq, mk, mv, gates, so, gu, gv = outs[:12]
        seq = lambda a: a.reshape(B, T, a.shape[-1])
        if hist is None:
            a_out = _attn_call(seq(q), seq(k), seq(v))
            s0 = jnp.zeros((B, MLSTM_HEADS, 2 * MLSTM_DH, MLSTM_DH), F32)
            m0 = jnp.zeros((B, MLSTM_HEADS, 8, 128), F32)
        else:
            ckv_past, kr_past, c0, n0, m0_in = hist
            P = ckv_past.shape[2]
            krp = jnp.pad(kr_past[l].reshape(B * P, ROPE_DIM),
                          ((0, 0), (KR_LANE, HEAD_PAD - KR_LANE - ROPE_DIM)))
            kp, vp = _pastkv_call(ckv_past[l].reshape(B * P, KV_RANK), krp, pw["wkv"])
            a_out = _attn_hist_call(seq(q), kp.reshape(B, P, QK_PAD), vp.reshape(B, P, V_ALL), seq(k), seq(v))
            s0 = jnp.concatenate(
                [c0[l], jnp.broadcast_to(n0[l][:, :, None, :], (B, MLSTM_HEADS, MLSTM_DH, MLSTM_DH))], axis=2)
            m0 = jnp.broadcast_to(m0_in[l][:, :, None, None], (B, MLSTM_HEADS, 8, 128))
        b_out, sfin, mfin = _mlstm_call(seq(mq), seq(mk), seq(mv), seq(gates), seq(so), s0, m0)
        mwl = dict(mw)
        mwl["ws"] = mw["ws_full"][:, :L, :L]
        mwl["bsf"] = jnp.repeat(mw["bs_full"][:, :L].T, GMLP_DG, axis=1)
        x1 = _merge_call(x2d, a_out.reshape(N, V_ALL), b_out.reshape(N, MLSTM_WIDTH), gu, gv, mwl, L)
        x2d = _moe_call(x1, dict(ew, rwt=rwt, rb=rb))
        ckvs.append(ckvn.reshape(B, T, KV_RANK))
        krs.append(kr[:, KR_LANE:KR_LANE + ROPE_DIM].reshape(B, T, ROPE_DIM))
        Cs.append(sfin[:, :, :MLSTM_DH, :])
        ns.append(sfin[:, :, MLSTM_DH, :])
        ms.append(mfin[:, :, 0, 0])
        if hist is not None:
            gvs.append(outs[12].reshape(B, T, GMLP_WIDTH))
    res = [x2d.reshape(B, T, D_MODEL), jnp.stack(ckvs), jnp.stack(krs), jnp.stack(Cs), jnp.stack(ns), jnp.stack(ms)]
    if hist is not None:
        res.append(jnp.stack(gvs))
    return res


def kernel(x_prompt, x_sample, cache_mla_ckv, cache_mla_krope, state_mlstm_c, state_mlstm_n, state_mlstm_m,
           w_in, b_in, q_norm_g, kv_norm_g, w_uq, w_ukv, gmlp_ln_g, gmlp_ln_b, gmlp_ws, gmlp_bs, w_branch,
           w_out, ln1_g, ln1_b, router_w, router_b, moe_w_gate, moe_w_up, moe_w_down, shared_w_gate,
           shared_w_up, shared_w_down, ln2_g, ln2_b):
    depth = w_in.shape[0]
    layers = [_prep_layer(l, w_in, b_in, q_norm_g, kv_norm_g, w_uq, w_ukv, gmlp_ln_g, gmlp_ln_b, gmlp_ws,
                          gmlp_bs, w_branch, w_out, ln1_g, ln1_b, moe_w_gate, moe_w_up, moe_w_down,
                          shared_w_gate, shared_w_up, shared_w_down, ln2_g, ln2_b) for l in range(depth)]
    rwt = router_w.T.astype(BF16)
    rb = router_b[:, None]
    yp, p_ckv, p_kr, p_c, p_n, p_m = _trunk(x_prompt, None, layers, rwt, rb)
    ys, s_ckv, s_kr, s_c, s_n, s_m, s_gv = _trunk(
        x_sample, (cache_mla_ckv, cache_mla_krope, state_mlstm_c, state_mlstm_n, state_mlstm_m), layers, rwt, rb)
    return (yp, ys, p_ckv, p_kr, p_c, p_n, p_m, s_ckv, s_kr, s_c, s_n, s_m, s_gv)
```

```python
import functools

import jax
import jax.numpy as jnp
import numpy as np
from jax import lax
from jax.experimental import pallas as pl
from jax.experimental.pallas import tpu as pltpu

F32 = jnp.float32
BF16 = jnp.bfloat16

D_MODEL = 1024
CHUNK = 64
MLA_HEADS = 8
Q_RANK = 256
KV_RANK = 256
NOPE_DIM = 64
ROPE_DIM = 32
V_DIM = 64
ROPE_THETA = 10000.0
ATTN_SCALE = (NOPE_DIM + ROPE_DIM) ** -0.5
Q_SCALE = ATTN_SCALE * float(np.log2(np.e))
MLSTM_HEADS = 4
MLSTM_DH = 128
MLSTM_WIDTH = MLSTM_HEADS * MLSTM_DH
GMLP_GROUPS = 4
GMLP_DG = 128
GMLP_WIDTH = GMLP_GROUPS * GMLP_DG
GMLP_CHUNK = 128
N_BRANCH = 3
N_EXPERTS = 16
N_GROUPS = 4
EXPERTS_PER_GROUP = N_EXPERTS // N_GROUPS
D_EXPERT = 256
D_SHARED = 256
DEPTH = 4
DN_ALPHA = (2 * DEPTH) ** 0.25
EPS = 1e-5

HEAD_PAD = 128
QK_PAD = MLA_HEADS * HEAD_PAD
V_ALL = MLA_HEADS * V_DIM
P_CQ, P_CKV, P_KR, P_KRR, P_MQ, P_MK, P_MV, P_G, P_MO, P_GU, P_GV = (
    0, 256, 512, 640, 768, 1280, 1792, 2304, 2432, 2944, 3456)
D_PROJ = 3968
O_CQ, O_CKV, O_KR, O_MQ, O_MK, O_MV, O_MI, O_MF, O_MO, O_GU, O_GV, O_GT = (
    0, 256, 512, 544, 1056, 1568, 2080, 2084, 2088, 2600, 3112, 3624)
KR_LANE = NOPE_DIM
VMEM_LIMIT = 56 * 1024 * 1024
NEG_INF = float("-inf")


def _dot(a, b):
    return jnp.dot(a, b, preferred_element_type=F32)


def _dot_nt(a, b):
    return lax.dot_general(a, b, (((1,), (1,)), ((), ())), preferred_element_type=F32)


def _const_spec(shape):
    nd = len(shape)
    return pl.BlockSpec(shape, lambda *_: (0,) * nd, pipeline_mode=pl.Buffered(1))


def _layernorm(x, g, b):
    mu = jnp.mean(x, axis=-1, keepdims=True)
    xc = x - mu
    var = jnp.mean(xc * xc, axis=-1, keepdims=True)
    return xc * lax.rsqrt(var + EPS) * g + b


def _rmsnorm(x, g):
    return x * lax.rsqrt(jnp.mean(x * x, axis=-1, keepdims=True) + EPS) * g


def _gelu(x):
    return 0.5 * x * (1.0 + jnp.tanh(np.sqrt(2.0 / np.pi).astype(np.float32) * (x + 0.044715 * (x * x * x))))


def _sigmoid(x):
    return 1.0 / (1.0 + jnp.exp(-x))


def _log_sigmoid(x):
    return jnp.minimum(x, 0.0) - jnp.log(1.0 + jnp.exp(-jnp.abs(x)))


def _proj_kernel(x_ref, w1_ref, b1_ref, qg_ref, kvg_ref, wq_ref, wkv_ref, cq_ref, sq_ref, ck_ref, sk_ref,
                 lng_ref, lnb_ref,
                 q_out, k_out, v_out, ckv_out, kr_out, mq_out, mk_out, mv_out, g_out, so_out, gu_out,
                 gv_out, *rest):
    xb = x_ref[...].astype(BF16)

    def z(lo, hi):
        return _dot(xb, w1_ref[:, lo:hi]) + b1_ref[:, lo:hi]

    zc = z(P_CQ, P_MQ)
    cqn = _rmsnorm(zc[:, P_CQ:P_CKV], qg_ref[...])
    ckvn = _rmsnorm(zc[:, P_CKV:P_KR], kvg_ref[...])
    kr = zc[:, P_KR:P_KRR] * ck_ref[...] + zc[:, P_KRR:P_MQ] * sk_ref[...]
    ckv_out[...] = ckvn
    kr_out[...] = kr
    qq = _dot(cqn.astype(BF16), wq_ref[...])
    cos8 = jnp.concatenate([cq_ref[...]] * MLA_HEADS, axis=1)
    sin8 = jnp.concatenate([sq_ref[...]] * MLA_HEADS, axis=1)
    q_out[...] = (qq[:, :QK_PAD] * cos8 + qq[:, QK_PAD:] * sin8).astype(BF16)
    kk = _dot(ckvn.astype(BF16), wkv_ref[...])
    kr8 = jnp.concatenate([kr] * MLA_HEADS, axis=1)
    k_out[...] = (kk[:, :QK_PAD] + kr8).astype(BF16)
    v_out[...] = kk[:, QK_PAD:].astype(BF16)

    zm = z(P_MQ, P_MO)
    mq_out[...] = zm[:, 0:512].astype(BF16)
    mk_out[...] = (zm[:, 512:1024] * (MLSTM_DH ** -0.5)).astype(BF16)
    mv_out[...] = zm[:, 1024:1536].astype(BF16)
    zg = zm[:, 1536:1664]
    lane = lax.broadcasted_iota(jnp.int32, zg.shape, 1)
    g_out[...] = jnp.where(lane < MLSTM_HEADS, zg, _log_sigmoid(zg))
    so_out[...] = _sigmoid(z(P_MO, P_GU)).astype(BF16)

    zu = z(P_GU, D_PROJ)
    gu_out[...] = _gelu(zu[:, :GMLP_WIDTH]).astype(BF16)
    vrows = _layernorm(_gelu(zu[:, GMLP_WIDTH:]), lng_ref[...], lnb_ref[...])
    gv_out[...] = vrows.astype(BF16)
    if rest:
        rest[0][...] = vrows


def _proj_call(x2d, pw, tabs, T, want_vrows):
    N = x2d.shape[0]
    TM = min(512, N)
    nt = max(1, T // TM)
    grid = (N // TM,)
    row = lambda w: pl.BlockSpec((TM, w), lambda i: (i, 0))
    tab = pl.BlockSpec((TM, HEAD_PAD), lambda i: (i % nt, 0))
    in_specs = [
        row(D_MODEL),
        _const_spec((D_MODEL, D_PROJ)), _const_spec((1, D_PROJ)),
        _const_spec((1, Q_RANK)), _const_spec((1, KV_RANK)),
        _const_spec((Q_RANK, 2 * QK_PAD)), _const_spec((KV_RANK, QK_PAD + V_ALL)),
        tab, tab, tab, tab,
        _const_spec((1, GMLP_WIDTH)), _const_spec((1, GMLP_WIDTH)),
    ]
    out_shapes = [
        ((N, QK_PAD), BF16), ((N, QK_PAD), BF16), ((N, V_ALL), BF16), ((N, KV_RANK), F32),
        ((N, HEAD_PAD), F32), ((N, MLSTM_WIDTH), BF16), ((N, MLSTM_WIDTH), BF16), ((N, MLSTM_WIDTH), BF16),
        ((N, HEAD_PAD), F32), ((N, MLSTM_WIDTH), BF16), ((N, GMLP_WIDTH), BF16), ((N, GMLP_WIDTH), BF16),
    ]
    if want_vrows:
        out_shapes.append(((N, GMLP_WIDTH), F32))
    return pl.pallas_call(
        _proj_kernel,
        grid=grid,
        in_specs=in_specs,
        out_specs=[row(s[1]) for s, _ in out_shapes],
        out_shape=[jax.ShapeDtypeStruct(s, d) for s, d in out_shapes],
        compiler_params=pltpu.CompilerParams(dimension_semantics=("parallel",), vmem_limit_bytes=VMEM_LIMIT),
        name="proj",
    )(x2d, pw["w1"], pw["b1"], pw["qg"], pw["kvg"], pw["wq"], pw["wkv"],
      tabs["cq"], tabs["sq"], tabs["ck"], tabs["sk"], pw["lng"], pw["lnb"])


def _pastkv_kernel(c_ref, kr_ref, wkv_ref, k_out, v_out):
    kk = _dot(c_ref[...].astype(BF16), wkv_ref[...])
    kr8 = jnp.concatenate([kr_ref[...]] * MLA_HEADS, axis=1)
    k_out[...] = (kk[:, :QK_PAD] + kr8).astype(BF16)
    v_out[...] = kk[:, QK_PAD:].astype(BF16)


def _pastkv_call(ckv2d, kr2d, wkv):
    N = ckv2d.shape[0]
    TM = min(1024, N)
    row = lambda w: pl.BlockSpec((TM, w), lambda i: (i, 0))
    return pl.pallas_call(
        _pastkv_kernel,
        grid=(N // TM,),
        in_specs=[row(KV_RANK), row(HEAD_PAD), _const_spec((KV_RANK, QK_PAD + V_ALL))],
        out_specs=[row(QK_PAD), row(V_ALL)],
        out_shape=[jax.ShapeDtypeStruct((N, QK_PAD), BF16), jax.ShapeDtypeStruct((N, V_ALL), BF16)],
        compiler_params=pltpu.CompilerParams(dimension_semantics=("parallel",), vmem_limit_bytes=VMEM_LIMIT),
        name="pastkv",
    )(ckv2d, kr2d, wkv)


def _attn_kernel(q_ref, k_ref, v_ref, o_ref, mx_scr, ls_scr, acc_scr, s_scr, *, TQ):
    i = pl.program_id(1)
    rc = lax.broadcasted_iota(jnp.int32, (TQ, TQ), 0) // CHUNK
    cc = lax.broadcasted_iota(jnp.int32, (TQ, TQ), 1) // CHUNK
    visible = cc <= rc
    lane = lax.broadcasted_iota(jnp.int32, (TQ, HEAD_PAD), 1)
    nlane = TQ // 128

    def fold(a, op):
        r = a[:, 0:128]
        for t in range(1, nlane):
            r = op(r, a[:, 128 * t:128 * (t + 1)])
        return r

    def sweep_max(j, masked, first):
        r0 = pl.multiple_of(j * TQ, TQ)
        for h in range(MLA_HEADS):
            hs = slice(HEAD_PAD * h, HEAD_PAD * (h + 1))
            s = _dot_nt(q_ref[0, :, hs], k_ref[0, pl.ds(r0, TQ), hs])
            if masked:
                s = jnp.where(visible, s, NEG_INF)
            s_scr[h, j] = s
            mx = fold(s, jnp.maximum)
            mx_scr[h] = mx if first else jnp.maximum(mx_scr[h], mx)

    def sweep_pv(j, first):
        r0 = pl.multiple_of(j * TQ, TQ)
        for h in range(MLA_HEADS):
            vs = slice(HEAD_PAD * (h // 2), HEAD_PAD * (h // 2 + 1))
            mb = mx_scr[h]
            p = jnp.exp2(s_scr[h, j] - jnp.concatenate([mb] * nlane, axis=1))
            ls = fold(p, jnp.add)
            pv = _dot(p.astype(BF16), v_ref[0, pl.ds(r0, TQ), vs])
            ls_scr[h] = ls if first else ls_scr[h] + ls
            acc_scr[h] = pv if first else acc_scr[h] + pv

    def body_max(j, c):
        sweep_max(j, False, False)
        return c

    def body_pv(j, c):
        sweep_pv(j, False)
        return c

    sweep_max(i, True, True)
    lax.fori_loop(0, i, body_max, 0)
    for h in range(MLA_HEADS):
        mx_scr[h] = jnp.broadcast_to(jnp.max(mx_scr[h], axis=-1, keepdims=True), (TQ, 128))
    sweep_pv(i, True)
    lax.fori_loop(0, i, body_pv, 0)
    for p in range(MLA_HEADS // 2):
        o0 = acc_scr[2 * p] / jnp.sum(ls_scr[2 * p], axis=-1, keepdims=True)
        o1 = acc_scr[2 * p + 1] / jnp.sum(ls_scr[2 * p + 1], axis=-1, keepdims=True)
        o_ref[0, :, HEAD_PAD * p:HEAD_PAD * (p + 1)] = jnp.where(lane < V_DIM, o0, o1).astype(BF16)


def _attn_call(q, k, v):
    B, T, _ = q.shape
    TQ = min(256, T)
    nq = T // TQ
    return pl.pallas_call(
        functools.partial(_attn_kernel, TQ=TQ),
        grid=(B, nq),
        in_specs=[pl.BlockSpec((1, TQ, QK_PAD), lambda b, i: (b, i, 0)),
                  pl.BlockSpec((1, T, QK_PAD), lambda b, i: (b, 0, 0)),
                  pl.BlockSpec((1, T, V_ALL), lambda b, i: (b, 0, 0))],
        out_specs=pl.BlockSpec((1, TQ, V_ALL), lambda b, i: (b, i, 0)),
        out_shape=jax.ShapeDtypeStruct((B, T, V_ALL), BF16),
        scratch_shapes=[pltpu.VMEM((MLA_HEADS, TQ, 128), F32)] * 3 + [pltpu.VMEM((MLA_HEADS, nq, TQ, TQ), F32)],
        compiler_params=pltpu.CompilerParams(dimension_semantics=("parallel", "arbitrary"),
                                             vmem_limit_bytes=VMEM_LIMIT),
        name="attn",
    )(q, k, v)


def _attn_hist_kernel(q_ref, kp_ref, vp_ref, kn_ref, vn_ref, o_ref):
    T = q_ref.shape[1]
    lane = lax.broadcasted_iota(jnp.int32, (T, HEAD_PAD), 1)
    for p in range(MLA_HEADS // 2):
        pair = []
        vs = slice(HEAD_PAD * p, HEAD_PAD * (p + 1))
        for hh in range(2):
            h = 2 * p + hh
            hs = slice(HEAD_PAD * h, HEAD_PAD * (h + 1))
            qh = q_ref[0, :, hs]
            s1 = _dot_nt(qh, kp_ref[0, :, hs])
            s2 = _dot_nt(qh, kn_ref[0, :, hs])
            m = jnp.maximum(jnp.max(s1, axis=-1, keepdims=True), jnp.max(s2, axis=-1, keepdims=True))
            p1 = jnp.exp2(s1 - m)
            p2 = jnp.exp2(s2 - m)
            l = jnp.sum(p1, axis=-1, keepdims=True) + jnp.sum(p2, axis=-1, keepdims=True)
            acc = _dot(p1.astype(BF16), vp_ref[0, :, vs]) + _dot(p2.astype(BF16), vn_ref[0, :, vs])
            pair.append(acc / l)
        o_ref[0, :, vs] = jnp.where(lane < V_DIM, pair[0], pair[1]).astype(BF16)


def _attn_hist_call(q, kp, vp, kn, vn):
    B, T, _ = q.shape
    P = kp.shape[1]
    blk = lambda t, w: pl.BlockSpec((1, t, w), lambda b: (b, 0, 0))
    return pl.pallas_call(
        _attn_hist_kernel,
        grid=(B,),
        in_specs=[blk(T, QK_PAD), blk(P, QK_PAD), blk(P, V_ALL), blk(T, QK_PAD), blk(T, V_ALL)],
        out_specs=blk(T, V_ALL),
        out_shape=jax.ShapeDtypeStruct((B, T, V_ALL), BF16),
        compiler_params=pltpu.CompilerParams(dimension_semantics=("parallel",), vmem_limit_bytes=VMEM_LIMIT),
        name="attn_hist",
    )(q, kp, vp, kn, vn)


MLSTM_BB = 2


def _mlstm_kernel(q_ref, k_ref, v_ref, g_ref, so_ref, s0_ref, m0_ref, h_out, sfin_out, mfin_out,
                  s_scr, m_scr, *, TL, BB):
    t = pl.program_id(1)
    L = CHUNK
    DH = MLSTM_DH

    @pl.when(t == 0)
    def _():
        s_scr[...] = s0_ref[...]
        m_scr[...] = m0_ref[...]

    r_i = lax.broadcasted_iota(jnp.int32, (L, L), 0)
    c_i = lax.broadcasted_iota(jnp.int32, (L, L), 1)
    causal = c_i <= r_i
    tri = jnp.where(causal, 1.0, 0.0).astype(BF16)
    lane = lax.broadcasted_iota(jnp.int32, (L, 128), 1)
    ones_blk = jnp.ones((L, DH), BF16)

    def rep(col):
        return jnp.broadcast_to(col, (L, 128))

    def chunk(c, carry):
        r0 = pl.multiple_of(c * L, L)
        for bb in range(BB):
            G = g_ref[bb, pl.ds(r0, L), :]
            g_hi = G.astype(BF16)
            g_r1 = G - g_hi.astype(F32)
            g_mid = g_r1.astype(BF16)
            g_lo = (g_r1 - g_mid.astype(F32)).astype(BF16)
            cum = _dot(tri, g_hi) + _dot(tri, g_mid) + _dot(tri, g_lo)
            VT = jnp.where(lane < MLSTM_HEADS, G, cum).T
            for h in range(MLSTM_HEADS):
                hs = slice(DH * h, DH * (h + 1))
                b_t = rep(jnp.sum(jnp.where(lane == MLSTM_HEADS + h, cum, 0.0), axis=1, keepdims=True))
                ig_t = rep(jnp.sum(jnp.where(lane == h, G, 0.0), axis=1, keepdims=True))
                brow = VT[MLSTM_HEADS + h:MLSTM_HEADS + h + 1, :]
                igrow = VT[h:h + 1, :]
                m_prev = m_scr[bb, h, 0:1, :]
                d = jnp.where(causal, b_t[:, :L] - brow + igrow, NEG_INF)
                g_t = b_t + m_prev
                mt = jnp.maximum(g_t, rep(jnp.max(d, axis=1, keepdims=True)))
                inter = jnp.exp(g_t - mt)
                w = jnp.exp(d - mt[:, :L])
                qh = q_ref[bb, pl.ds(r0, L), hs]
                kh = k_ref[bb, pl.ds(r0, L), hs]
                vh = v_ref[bb, pl.ds(r0, L), hs]
                qk = _dot_nt(qh, kh) * w
                St = s_scr[bb, h]
                sq = _dot(qh, St.astype(BF16))
                vext = jnp.concatenate([vh, ones_blk], axis=1)
                intra = _dot(qk.astype(BF16), vext)
                num = inter * sq[:, :DH] + intra[:, :DH]
                den = jnp.maximum(jnp.abs(inter * sq[:, DH:] + intra[:, DH:]), jnp.exp(-mt))
                so = so_ref[bb, pl.ds(r0, L), hs].astype(F32)
                h_out[bb, pl.ds(r0, L), hs] = (so * (num / den)).astype(BF16)
                mL = mt[L - 1:L, :]
                bL = b_t[L - 1:L, :]
                ws_t = jnp.exp(bL - b_t + ig_t - mL)
                decay = jnp.exp(bL + m_prev - mL)
                wv = jnp.concatenate([ws_t * vh.astype(F32), ws_t], axis=1).astype(BF16)
                kT = kh.astype(F32).T.astype(BF16)
                s_scr[bb, h] = jnp.concatenate([decay, decay], axis=1) * St + _dot(kT, wv)
                m_scr[bb, h] = jnp.broadcast_to(mL, (8, 128))
        return carry

    lax.fori_loop(0, TL // L, chunk, 0)

    @pl.when(t == pl.num_programs(1) - 1)
    def _():
        sfin_out[...] = s_scr[...]
        mfin_out[...] = m_scr[...]


def _mlstm_call(mq, mk, mv, gates, so, s0, m0):
    B, T, _ = mq.shape
    TL = min(512, T)
    BB = MLSTM_BB
    seq = lambda w: pl.BlockSpec((BB, TL, w), lambda b, t: (b, t, 0))
    st = pl.BlockSpec((BB, MLSTM_HEADS, MLSTM_DH, 2 * MLSTM_DH), lambda b, t: (b, 0, 0, 0))
    mst = pl.BlockSpec((BB, MLSTM_HEADS, 8, 128), lambda b, t: (b, 0, 0, 0))
    return pl.pallas_call(
        functools.partial(_mlstm_kernel, TL=TL, BB=BB),
        grid=(B // BB, T // TL),
        in_specs=[seq(MLSTM_WIDTH), seq(MLSTM_WIDTH), seq(MLSTM_WIDTH), seq(128), seq(MLSTM_WIDTH), st, mst],
        out_specs=[seq(MLSTM_WIDTH), st, mst],
        out_shape=[jax.ShapeDtypeStruct((B, T, MLSTM_WIDTH), BF16),
                   jax.ShapeDtypeStruct((B, MLSTM_HEADS, MLSTM_DH, 2 * MLSTM_DH), F32),
                   jax.ShapeDtypeStruct((B, MLSTM_HEADS, 8, 128), F32)],
        scratch_shapes=[pltpu.VMEM((BB, MLSTM_HEADS, MLSTM_DH, 2 * MLSTM_DH), F32),
                        pltpu.VMEM((BB, MLSTM_HEADS, 8, 128), F32)],
        compiler_params=pltpu.CompilerParams(dimension_semantics=("parallel", "arbitrary"),
                                             vmem_limit_bytes=VMEM_LIMIT),
        name="mlstm",
    )(mq, mk, mv, gates, so, s0, m0)


def _merge_kernel(x_ref, a_ref, b_ref, gu_ref, gv_ref, wgt_ref, bgt_ref, ws_ref, bsf_ref, wb_ref, wo_ref,
                  g1_ref, b1_ref, x1_out, c_scr, *, L, TM):
    x = x_ref[...]
    xb = x.astype(BF16)
    r_i = lax.broadcasted_iota(jnp.int32, (L, L), 0)
    c_i = lax.broadcasted_iota(jnp.int32, (L, L), 1)
    for g in range(GMLP_GROUPS):
        gs = slice(GMLP_DG * g, GMLP_DG * (g + 1))
        wsg = jnp.where(c_i <= r_i, ws_ref[g], 0.0).astype(BF16)
        for c in range(TM // L):
            rs = slice(L * c, L * (c + 1))
            sp = _dot(wsg, gv_ref[rs, gs]) + bsf_ref[:, gs]
            c_scr[rs, gs] = (gu_ref[rs, gs].astype(F32) * sp).astype(BF16)
    merged = None
    for kb, br in enumerate((a_ref, b_ref, c_scr)):
        cs = slice(D_MODEL * kb, D_MODEL * (kb + 1))
        gate = _sigmoid(_dot(xb, wgt_ref[:, cs]) + bgt_ref[:, cs])
        term = gate * _dot(br[...], wb_ref[kb])
        merged = term if merged is None else merged + term
    y = _dot(merged.astype(BF16), wo_ref[...])
    x1_out[...] = _layernorm(DN_ALPHA * x + y, g1_ref[...], b1_ref[...])


def _merge_call(x2d, a2d, b2d, gu, gv, mw, L):
    N = x2d.shape[0]
    TM = min(512, N)
    row = lambda w: pl.BlockSpec((TM, w), lambda i: (i, 0))
    return pl.pallas_call(
        functools.partial(_merge_kernel, L=L, TM=TM),
        grid=(N // TM,),
        in_specs=[row(D_MODEL), row(V_ALL), row(MLSTM_WIDTH), row(GMLP_WIDTH), row(GMLP_WIDTH),
                  _const_spec((D_MODEL, N_BRANCH * D_MODEL)), _const_spec((1, N_BRANCH * D_MODEL)),
                  _const_spec((GMLP_GROUPS, L, L)), _const_spec((L, GMLP_WIDTH)),
                  _const_spec((N_BRANCH, 512, D_MODEL)), _const_spec((D_MODEL, D_MODEL)),
                  _const_spec((1, D_MODEL)), _const_spec((1, D_MODEL))],
        out_specs=row(D_MODEL),
        out_shape=jax.ShapeDtypeStruct((N, D_MODEL), F32),
        scratch_shapes=[pltpu.VMEM((TM, GMLP_WIDTH), BF16)],
        compiler_params=pltpu.CompilerParams(dimension_semantics=("parallel",), vmem_limit_bytes=VMEM_LIMIT),
        name="merge",
    )(x2d, a2d, b2d, gu, gv, mw["wgt"], mw["bgt"], mw["ws"], mw["bsf"], mw["wb"], mw["wo"], mw["g1"], mw["b1"])


def _moe_kernel(x_ref, rwt_ref, rb_ref, wg_ref, wu_ref, wd_ref, g2_ref, b2_ref, x2_out, h_scr, *, TM):
    x = x_ref[...]
    xb = x.astype(BF16)
    s = _sigmoid(_dot_nt(rwt_ref[...], xb))
    sb = s + rb_ref[...]
    rows = [sb[e:e + 1, :] for e in range(N_EXPERTS)]
    srow = [s[e:e + 1, :] for e in range(N_EXPERTS)]
    gscore = []
    for g in range(N_GROUPS):
        mem = rows[EXPERTS_PER_GROUP * g:EXPERTS_PER_GROUP * (g + 1)]
        best = None
        for a in range(EXPERTS_PER_GROUP):
            for b in range(a + 1, EXPERTS_PER_GROUP):
                pr = mem[a] + mem[b]
                best = pr if best is None else jnp.maximum(best, pr)
        gscore.append(best)
    gmax = functools.reduce(jnp.maximum, gscore)
    taken = None
    gsel = []
    for g in range(N_GROUPS):
        hit = gscore[g] == gmax
        if taken is None:
            gsel.append(hit)
            taken = hit
        else:
            gsel.append(jnp.logical_and(hit, jnp.logical_not(taken)))
            taken = jnp.logical_or(taken, hit)
    sel_w = []
    for e in range(N_EXPERTS):
        g = e // EXPERTS_PER_GROUP
        rank = None
        for o in range(EXPERTS_PER_GROUP * g, EXPERTS_PER_GROUP * (g + 1)):
            if o == e:
                continue
            ahead = (rows[o] >= rows[e]) if o < e else (rows[o] > rows[e])
            ahead = jnp.where(ahead, 1.0, 0.0)
            rank = ahead if rank is None else rank + ahead
        chosen = jnp.logical_and(gsel[g], rank < 1.5)
        sel_w.append(jnp.where(chosen, srow[e], 0.0))
    den = functools.reduce(jnp.add, sel_w)
    gate_rows = [w_ / den for w_ in sel_w]
    gate_t = jnp.concatenate(gate_rows + [jnp.zeros((128 - N_EXPERTS, TM), F32)], axis=0)
    gate = gate_t.T
    for e in range(N_EXPERTS + 1):
        hcol = slice(D_EXPERT * e, D_EXPERT * (e + 1))
        hg = _dot(xb, wg_ref[:, hcol])
        hu = _dot(xb, wu_ref[:, hcol])
        hh = hg * _sigmoid(hg) * hu
        if e < N_EXPERTS:
            hh = hh * gate[:, e:e + 1]
        h_scr[:, hcol] = hh.astype(BF16)
    out = _dot(h_scr[...], wd_ref[...])
    x2_out[...] = _layernorm(DN_ALPHA * x + out, g2_ref[...], b2_ref[...])


def _moe_call(x2d, ew):
    N = x2d.shape[0]
    TM = min(512, N)
    HW = (N_EXPERTS + 1) * D_EXPERT
    row = lambda w: pl.BlockSpec((TM, w), lambda i: (i, 0))
    return pl.pallas_call(
        functools.partial(_moe_kernel, TM=TM),
        grid=(N // TM,),
        in_specs=[row(D_MODEL), _const_spec((N_EXPERTS, D_MODEL)), _const_spec((N_EXPERTS, 1)),
                  _const_spec((D_MODEL, HW)), _const_spec((D_MODEL, HW)), _const_spec((HW, D_MODEL)),
                  _const_spec((1, D_MODEL)), _const_spec((1, D_MODEL))],
        out_specs=row(D_MODEL),
        out_shape=jax.ShapeDtypeStruct((N, D_MODEL), F32),
        scratch_shapes=[pltpu.VMEM((TM, HW), BF16)],
        compiler_params=pltpu.CompilerParams(dimension_semantics=("parallel",), vmem_limit_bytes=VMEM_LIMIT),
        name="moe",
    )(x2d, ew["rwt"], ew["rb"], ew["wg"], ew["wu"], ew["wd"], ew["g2"], ew["b2"])


def _rot_cols(w):
    half = w.shape[-1] // 2
    return jnp.concatenate([-w[..., half:], w[..., :half]], axis=-1)


def _prep_layer(l, w_in, b_in, q_norm_g, kv_norm_g, w_uq, w_ukv, gmlp_ln_g, gmlp_ln_b, gmlp_ws, gmlp_bs,
                w_branch, w_out, ln1_g, ln1_b, moe_w_gate, moe_w_up, moe_w_down, shared_w_gate,
                shared_w_up, shared_w_down, ln2_g, ln2_b):
    wi = w_in[l]
    bi = b_in[l][None, :]

    def proj_cols(m):
        rows = m.shape[0]
        zero = lambda n: jnp.zeros((rows, n), m.dtype)
        kr = m[:, O_KR:O_MQ]
        kr128 = jnp.concatenate([zero(KR_LANE), kr, zero(HEAD_PAD - KR_LANE - ROPE_DIM)], axis=1)
        krr128 = jnp.concatenate([zero(KR_LANE), _rot_cols(kr), zero(HEAD_PAD - KR_LANE - ROPE_DIM)], axis=1)
        gates = jnp.concatenate([m[:, O_MI:O_MO], zero(HEAD_PAD - 2 * MLSTM_HEADS)], axis=1)
        return jnp.concatenate([m[:, O_CQ:O_KR], kr128, krr128, m[:, O_MQ:O_MI], gates, m[:, O_MO:O_GT]], axis=1)

    uq = w_uq[l].reshape(Q_RANK, MLA_HEADS, NOPE_DIM + ROPE_DIM)
    zq = lambda n: jnp.zeros((Q_RANK, MLA_HEADS, n), F32)
    pad = HEAD_PAD - NOPE_DIM - ROPE_DIM
    wq_a = jnp.concatenate([uq, zq(pad)], axis=-1).reshape(Q_RANK, QK_PAD)
    wq_b = jnp.concatenate([zq(NOPE_DIM), _rot_cols(uq[..., NOPE_DIM:]), zq(pad)], axis=-1).reshape(Q_RANK, QK_PAD)
    ukv = w_ukv[l].reshape(KV_RANK, MLA_HEADS, NOPE_DIM + V_DIM)
    wk = jnp.concatenate([ukv[..., :NOPE_DIM], jnp.zeros((KV_RANK, MLA_HEADS, HEAD_PAD - NOPE_DIM), F32)],
                         axis=-1).reshape(KV_RANK, QK_PAD)
    wv = ukv[..., NOPE_DIM:].reshape(KV_RANK, V_ALL)
    pw = dict(
        w1=proj_cols(wi).astype(BF16), b1=proj_cols(bi),
        qg=q_norm_g[l][None, :], kvg=kv_norm_g[l][None, :],
        wq=jnp.concatenate([wq_a, wq_b], axis=1).astype(BF16),
        wkv=jnp.concatenate([wk, wv], axis=1).astype(BF16),
        lng=gmlp_ln_g[l][None, :], lnb=gmlp_ln_b[l][None, :],
    )
    mw = dict(
        wgt=wi[:, O_GT:].astype(BF16), bgt=bi[:, O_GT:],
        ws_full=gmlp_ws[l], bs_full=gmlp_bs[l],
        wb=w_branch[l].astype(BF16), wo=w_out[l].astype(BF16),
        g1=ln1_g[l][None, :], b1=ln1_b[l][None, :],
    )
    cat_in = lambda we, ws_: jnp.concatenate(
        [jnp.transpose(we[l], (1, 0, 2)).reshape(D_MODEL, N_EXPERTS * D_EXPERT), ws_[l]], axis=1).astype(BF16)
    ew = dict(
        wg=cat_in(moe_w_gate, shared_w_gate), wu=cat_in(moe_w_up, shared_w_up),
        wd=jnp.concatenate([moe_w_down[l].reshape(N_EXPERTS * D_EXPERT, D_MODEL), shared_w_down[l]],
                           axis=0).astype(BF16),
        g2=ln2_g[l][None, :], b2=ln2_b[l][None, :],
    )
    return pw, mw, ew


def _rope_tables(T, past, rows):
    half = ROPE_DIM // 2
    pos = (past + jnp.arange(T)).astype(F32)
    inv = ROPE_THETA ** (-jnp.arange(half, dtype=F32) / half)
    ang = pos[:, None] * inv[None, :]
    cos = jnp.cos(ang)
    sin = jnp.sin(ang)
    c2 = jnp.concatenate([cos, cos], axis=1)
    s2 = jnp.concatenate([sin, sin], axis=1)
    z = lambda n: jnp.zeros((T, n), F32)
    tail = HEAD_PAD - NOPE_DIM - ROPE_DIM
    tabs = dict(
        cq=jnp.concatenate([jnp.ones((T, NOPE_DIM), F32), c2, z(tail)], axis=1) * Q_SCALE,
        sq=jnp.concatenate([z(NOPE_DIM), s2, z(tail)], axis=1) * Q_SCALE,
        ck=jnp.concatenate([z(KR_LANE), c2, z(tail)], axis=1),
        sk=jnp.concatenate([z(KR_LANE), s2, z(tail)], axis=1),
    )
    if rows > T:
        tabs = {k: jnp.tile(v, (rows // T, 1)) for k, v in tabs.items()}
    return tabs


def _trunk(x, hist, layers, rwt, rb):
    B, T, _ = x.shape
    N = B * T
    past = 0 if hist is None else hist[0].shape[2]
    tabs = _rope_tables(T, past, min(512, N))
    L = GMLP_CHUNK if T % GMLP_CHUNK == 0 else T
    x2d = x.reshape(N, D_MODEL)
    ckvs, krs, Cs, ns, ms, gvs = [], [], [], [], [], []
    for l, (pw, mw, ew) in enumerate(layers):
        outs = _proj_call(x2d, pw, tabs, T, hist is not None)
        q, k, v, ckvn, kr, mq, mk, mv, gates, so, gu, gv = outs[:12]
        seq = lambda a: a.reshape(B, T, a.shape[-1])
        if hist is None:
            a_out = _attn_call(seq(q), seq(k), seq(v))
            s0 = jnp.zeros((B, MLSTM_HEADS, MLSTM_DH, 2 * MLSTM_DH), F32)
            m0 = jnp.zeros((B, MLSTM_HEADS, 8, 128), F32)
        else:
            ckv_past, kr_past, c0, n0, m0_in = hist
            P = ckv_past.shape[2]
            krp = jnp.pad(kr_past[l].reshape(B * P, ROPE_DIM),
                          ((0, 0), (KR_LANE, HEAD_PAD - KR_LANE - ROPE_DIM)))
            kp, vp = _pastkv_call(ckv_past[l].reshape(B * P, KV_RANK), krp, pw["wkv"])
            a_out = _attn_hist_call(seq(q), kp.reshape(B, P, QK_PAD), vp.reshape(B, P, V_ALL), seq(k), seq(v))
            s0 = jnp.concatenate(
                [jnp.swapaxes(c0[l], -1, -2),
                 jnp.broadcast_to(n0[l][:, :, :, None], (B, MLSTM_HEADS, MLSTM_DH, MLSTM_DH))], axis=3)
            m0 = jnp.broadcast_to(m0_in[l][:, :, None, None], (B, MLSTM_HEADS, 8, 128))
        b_out, sfin, mfin = _mlstm_call(seq(mq), seq(mk), seq(mv), seq(gates), seq(so), s0, m0)
        mwl = dict(mw)
        mwl["ws"] = mw["ws_full"][:, :L, :L]
        mwl["bsf"] = jnp.repeat(mw["bs_full"][:, :L].T, GMLP_DG, axis=1)
        x1 = _merge_call(x2d, a_out.reshape(N, V_ALL), b_out.reshape(N, MLSTM_WIDTH), gu, gv, mwl, L)
        x2d = _moe_call(x1, dict(ew, rwt=rwt, rb=rb))
        ckvs.append(ckvn.reshape(B, T, KV_RANK))
        krs.append(kr[:, KR_LANE:KR_LANE + ROPE_DIM].reshape(B, T, ROPE_DIM))
        Cs.append(jnp.swapaxes(sfin[:, :, :, :MLSTM_DH], -1, -2))
        ns.append(sfin[:, :, :, MLSTM_DH])
        ms.append(mfin[:, :, 0, 0])
        if hist is not None:
            gvs.append(outs[12].reshape(B, T, GMLP_WIDTH))
    res = [x2d.reshape(B, T, D_MODEL), jnp.stack(ckvs), jnp.stack(krs), jnp.stack(Cs), jnp.stack(ns), jnp.stack(ms)]
    if hist is not None:
        res.append(jnp.stack(gvs))
    return res


def kernel(x_prompt, x_sample, cache_mla_ckv, cache_mla_krope, state_mlstm_c, state_mlstm_n, state_mlstm_m,
           w_in, b_in, q_norm_g, kv_norm_g, w_uq, w_ukv, gmlp_ln_g, gmlp_ln_b, gmlp_ws, gmlp_bs, w_branch,
           w_out, ln1_g, ln1_b, router_w, router_b, moe_w_gate, moe_w_up, moe_w_down, shared_w_gate,
           shared_w_up, shared_w_down, ln2_g, ln2_b):
    depth = w_in.shape[0]
    layers = [_prep_layer(l, w_in, b_in, q_norm_g, kv_norm_g, w_uq, w_ukv, gmlp_ln_g, gmlp_ln_b, gmlp_ws,
                          gmlp_bs, w_branch, w_out, ln1_g, ln1_b, moe_w_gate, moe_w_up, moe_w_down,
                          shared_w_gate, shared_w_up, shared_w_down, ln2_g, ln2_b) for l in range(depth)]
    rwt = router_w.T.astype(BF16)
    rb = router_b[:, None]
    yp, p_ckv, p_kr, p_c, p_n, p_m = _trunk(x_prompt, None, layers, rwt, rb)
    ys, s_ckv, s_kr, s_c, s_n, s_m, s_gv = _trunk(
        x_sample, (cache_mla_ckv, cache_mla_krope, state_mlstm_c, state_mlstm_n, state_mlstm_m), layers, rwt, rb)
    return (yp, ys, p_ckv, p_kr, p_c, p_n, p_m, s_ckv, s_kr, s_c, s_n, s_m, s_gv)
```

```python
import functools

import jax
import jax.numpy as jnp
import numpy as np
from jax import lax
from jax.experimental import pallas as pl
from jax.experimental.pallas import tpu as pltpu

F32 = jnp.float32
BF16 = jnp.bfloat16

D_MODEL = 1024
CHUNK = 64
MLA_HEADS = 8
Q_RANK = 256
KV_RANK = 256
NOPE_DIM = 64
ROPE_DIM = 32
V_DIM = 64
ROPE_THETA = 10000.0
ATTN_SCALE = (NOPE_DIM + ROPE_DIM) ** -0.5
Q_SCALE = ATTN_SCALE * float(np.log2(np.e))
MLSTM_HEADS = 4
MLSTM_DH = 128
MLSTM_WIDTH = MLSTM_HEADS * MLSTM_DH
GMLP_GROUPS = 4
GMLP_DG = 128
GMLP_WIDTH = GMLP_GROUPS * GMLP_DG
GMLP_CHUNK = 128
N_BRANCH = 3
N_EXPERTS = 16
N_GROUPS = 4
EXPERTS_PER_GROUP = N_EXPERTS // N_GROUPS
D_EXPERT = 256
D_SHARED = 256
DEPTH = 4
DN_ALPHA = (2 * DEPTH) ** 0.25
EPS = 1e-5

HEAD_PAD = 128
QK_PAD = MLA_HEADS * HEAD_PAD
V_ALL = MLA_HEADS * V_DIM
P_CQ, P_CKV, P_KR, P_KRR, P_MQ, P_MK, P_MV, P_G, P_MO, P_GU, P_GV = (
    0, 256, 512, 640, 768, 1280, 1792, 2304, 2432, 2944, 3456)
D_PROJ = 3968
O_CQ, O_CKV, O_KR, O_MQ, O_MK, O_MV, O_MI, O_MF, O_MO, O_GU, O_GV, O_GT = (
    0, 256, 512, 544, 1056, 1568, 2080, 2084, 2088, 2600, 3112, 3624)
KR_LANE = NOPE_DIM
VMEM_LIMIT = 56 * 1024 * 1024
NEG_INF = float("-inf")


def _dot(a, b):
    return jnp.dot(a, b, preferred_element_type=F32)


def _dot_nt(a, b):
    return lax.dot_general(a, b, (((1,), (1,)), ((), ())), preferred_element_type=F32)


def _const_spec(shape):
    nd = len(shape)
    return pl.BlockSpec(shape, lambda *_: (0,) * nd, pipeline_mode=pl.Buffered(1))


def _layernorm(x, g, b):
    mu = jnp.mean(x, axis=-1, keepdims=True)
    xc = x - mu
    var = jnp.mean(xc * xc, axis=-1, keepdims=True)
    return xc * lax.rsqrt(var + EPS) * g + b


def _rmsnorm(x, g):
    return x * lax.rsqrt(jnp.mean(x * x, axis=-1, keepdims=True) + EPS) * g


def _gelu(x):
    return 0.5 * x * (1.0 + jnp.tanh(np.sqrt(2.0 / np.pi).astype(np.float32) * (x + 0.044715 * (x * x * x))))


def _sigmoid(x):
    return 1.0 / (1.0 + jnp.exp(-x))


def _log_sigmoid(x):
    return jnp.minimum(x, 0.0) - jnp.log(1.0 + jnp.exp(-jnp.abs(x)))


def _proj_kernel(x_ref, w1_ref, b1_ref, qg_ref, kvg_ref, wq_ref, wkv_ref, cq_ref, sq_ref, ck_ref, sk_ref,
                 lng_ref, lnb_ref,
                 q_out, k_out, v_out, ckv_out, kr_out, mq_out, mk_out, mv_out, g_out, so_out, gu_out,
                 gv_out, *rest):
    xb = x_ref[...].astype(BF16)

    def z(lo, hi):
        return _dot(xb, w1_ref[:, lo:hi]) + b1_ref[:, lo:hi]

    zc = z(P_CQ, P_MQ)
    cqn = _rmsnorm(zc[:, P_CQ:P_CKV], qg_ref[...])
    ckvn = _rmsnorm(zc[:, P_CKV:P_KR], kvg_ref[...])
    kr = zc[:, P_KR:P_KRR] * ck_ref[...] + zc[:, P_KRR:P_MQ] * sk_ref[...]
    ckv_out[...] = ckvn
    kr_out[...] = kr
    qq = _dot(cqn.astype(BF16), wq_ref[...])
    cos8 = jnp.concatenate([cq_ref[...]] * MLA_HEADS, axis=1)
    sin8 = jnp.concatenate([sq_ref[...]] * MLA_HEADS, axis=1)
    q_out[...] = (qq[:, :QK_PAD] * cos8 + qq[:, QK_PAD:] * sin8).astype(BF16)
    kk = _dot(ckvn.astype(BF16), wkv_ref[...])
    kr8 = jnp.concatenate([kr] * MLA_HEADS, axis=1)
    k_out[...] = (kk[:, :QK_PAD] + kr8).astype(BF16)
    v_out[...] = kk[:, QK_PAD:].astype(BF16)

    zm = z(P_MQ, P_MO)
    mq_out[...] = zm[:, 0:512].astype(BF16)
    mk_out[...] = (zm[:, 512:1024] * (MLSTM_DH ** -0.5)).astype(BF16)
    mv_out[...] = zm[:, 1024:1536].astype(BF16)
    zg = zm[:, 1536:1664]
    lane = lax.broadcasted_iota(jnp.int32, zg.shape, 1)
    g_out[...] = jnp.where(lane < MLSTM_HEADS, zg, _log_sigmoid(zg))
    so_out[...] = _sigmoid(z(P_MO, P_GU)).astype(BF16)

    zu = z(P_GU, D_PROJ)
    gu_out[...] = _gelu(zu[:, :GMLP_WIDTH]).astype(BF16)
    vrows = _layernorm(_gelu(zu[:, GMLP_WIDTH:]), lng_ref[...], lnb_ref[...])
    gv_out[...] = vrows.astype(BF16)
    if rest:
        rest[0][...] = vrows


def _proj_call(x2d, pw, tabs, T, want_vrows):
    N = x2d.shape[0]
    TM = min(512, N)
    nt = max(1, T // TM)
    grid = (N // TM,)
    row = lambda w: pl.BlockSpec((TM, w), lambda i: (i, 0))
    tab = pl.BlockSpec((TM, HEAD_PAD), lambda i: (i % nt, 0))
    in_specs = [
        row(D_MODEL),
        _const_spec((D_MODEL, D_PROJ)), _const_spec((1, D_PROJ)),
        _const_spec((1, Q_RANK)), _const_spec((1, KV_RANK)),
        _const_spec((Q_RANK, 2 * QK_PAD)), _const_spec((KV_RANK, QK_PAD + V_ALL)),
        tab, tab, tab, tab,
        _const_spec((1, GMLP_WIDTH)), _const_spec((1, GMLP_WIDTH)),
    ]
    out_shapes = [
        ((N, QK_PAD), BF16), ((N, QK_PAD), BF16), ((N, V_ALL), BF16), ((N, KV_RANK), F32),
        ((N, HEAD_PAD), F32), ((N, MLSTM_WIDTH), BF16), ((N, MLSTM_WIDTH), BF16), ((N, MLSTM_WIDTH), BF16),
        ((N, HEAD_PAD), F32), ((N, MLSTM_WIDTH), BF16), ((N, GMLP_WIDTH), BF16), ((N, GMLP_WIDTH), BF16),
    ]
    if want_vrows:
        out_shapes.append(((N, GMLP_WIDTH), F32))
    return pl.pallas_call(
        _proj_kernel,
        grid=grid,
        in_specs=in_specs,
        out_specs=[row(s[1]) for s, _ in out_shapes],
        out_shape=[jax.ShapeDtypeStruct(s, d) for s, d in out_shapes],
        compiler_params=pltpu.CompilerParams(dimension_semantics=("parallel",), vmem_limit_bytes=VMEM_LIMIT),
        name="proj",
    )(x2d, pw["w1"], pw["b1"], pw["qg"], pw["kvg"], pw["wq"], pw["wkv"],
      tabs["cq"], tabs["sq"], tabs["ck"], tabs["sk"], pw["lng"], pw["lnb"])


def _pastkv_kernel(c_ref, kr_ref, wkv_ref, k_out, v_out):
    kk = _dot(c_ref[...].astype(BF16), wkv_ref[...])
    kr8 = jnp.concatenate([kr_ref[...]] * MLA_HEADS, axis=1)
    k_out[...] = (kk[:, :QK_PAD] + kr8).astype(BF16)
    v_out[...] = kk[:, QK_PAD:].astype(BF16)


def _pastkv_call(ckv2d, kr2d, wkv):
    N = ckv2d.shape[0]
    TM = min(1024, N)
    row = lambda w: pl.BlockSpec((TM, w), lambda i: (i, 0))
    return pl.pallas_call(
        _pastkv_kernel,
        grid=(N // TM,),
        in_specs=[row(KV_RANK), row(HEAD_PAD), _const_spec((KV_RANK, QK_PAD + V_ALL))],
        out_specs=[row(QK_PAD), row(V_ALL)],
        out_shape=[jax.ShapeDtypeStruct((N, QK_PAD), BF16), jax.ShapeDtypeStruct((N, V_ALL), BF16)],
        compiler_params=pltpu.CompilerParams(dimension_semantics=("parallel",), vmem_limit_bytes=VMEM_LIMIT),
        name="pastkv",
    )(ckv2d, kr2d, wkv)


def _attn_kernel(q_ref, k_ref, v_ref, o_ref, mx_scr, acc_scr, s_scr, *, TQ):
    i = pl.program_id(1)
    ones_blk = jnp.ones((TQ, HEAD_PAD), BF16)
    rc = lax.broadcasted_iota(jnp.int32, (TQ, TQ), 0) // CHUNK
    cc = lax.broadcasted_iota(jnp.int32, (TQ, TQ), 1) // CHUNK
    visible = cc <= rc
    lane = lax.broadcasted_iota(jnp.int32, (TQ, HEAD_PAD), 1)
    nlane = TQ // 128

    def fold(a, op):
        r = a[:, 0:128]
        for t in range(1, nlane):
            r = op(r, a[:, 128 * t:128 * (t + 1)])
        return r

    def sweep_max(j, masked, first):
        r0 = pl.multiple_of(j * TQ, TQ)
        for h in range(MLA_HEADS):
            hs = slice(HEAD_PAD * h, HEAD_PAD * (h + 1))
            s = _dot_nt(q_ref[0, :, hs], k_ref[0, pl.ds(r0, TQ), hs])
            if masked:
                s = jnp.where(visible, s, NEG_INF)
            s_scr[h, j] = s
            mx = fold(s, jnp.maximum)
            mx_scr[h] = mx if first else jnp.maximum(mx_scr[h], mx)

    def sweep_pv(j, first):
        r0 = pl.multiple_of(j * TQ, TQ)
        for h in range(MLA_HEADS):
            vs = slice(HEAD_PAD * (h // 2), HEAD_PAD * (h // 2 + 1))
            mb = mx_scr[h]
            p = jnp.exp2(s_scr[h, j] - jnp.concatenate([mb] * nlane, axis=1))
            vext = jnp.concatenate([v_ref[0, pl.ds(r0, TQ), vs], ones_blk], axis=1)
            pv = _dot(p.astype(BF16), vext)
            acc_scr[h] = pv if first else acc_scr[h] + pv

    def body_max(j, c):
        sweep_max(j, False, False)
        return c

    def body_pv(j, c):
        sweep_pv(j, False)
        return c

    sweep_max(i, True, True)
    lax.fori_loop(0, i, body_max, 0)
    for h in range(MLA_HEADS):
        mx_scr[h] = jnp.broadcast_to(jnp.max(mx_scr[h], axis=-1, keepdims=True), (TQ, 128))
    sweep_pv(i, True)
    lax.fori_loop(0, i, body_pv, 0)
    for p in range(MLA_HEADS // 2):
        a0 = acc_scr[2 * p]
        a1 = acc_scr[2 * p + 1]
        o0 = a0[:, :HEAD_PAD] / a0[:, HEAD_PAD:]
        o1 = a1[:, :HEAD_PAD] / a1[:, HEAD_PAD:]
        o_ref[0, :, HEAD_PAD * p:HEAD_PAD * (p + 1)] = jnp.where(lane < V_DIM, o0, o1).astype(BF16)


def _attn_call(q, k, v):
    B, T, _ = q.shape
    TQ = min(256, T)
    nq = T // TQ
    return pl.pallas_call(
        functools.partial(_attn_kernel, TQ=TQ),
        grid=(B, nq),
        in_specs=[pl.BlockSpec((1, TQ, QK_PAD), lambda b, i: (b, i, 0)),
                  pl.BlockSpec((1, T, QK_PAD), lambda b, i: (b, 0, 0)),
                  pl.BlockSpec((1, T, V_ALL), lambda b, i: (b, 0, 0))],
        out_specs=pl.BlockSpec((1, TQ, V_ALL), lambda b, i: (b, i, 0)),
        out_shape=jax.ShapeDtypeStruct((B, T, V_ALL), BF16),
        scratch_shapes=[pltpu.VMEM((MLA_HEADS, TQ, 128), F32), pltpu.VMEM((MLA_HEADS, TQ, 2 * HEAD_PAD), F32),
                        pltpu.VMEM((MLA_HEADS, nq, TQ, TQ), F32)],
        compiler_params=pltpu.CompilerParams(dimension_semantics=("parallel", "arbitrary"),
                                             vmem_limit_bytes=VMEM_LIMIT),
        name="attn",
    )(q, k, v)


def _attn_hist_kernel(q_ref, kp_ref, vp_ref, kn_ref, vn_ref, o_ref):
    T = q_ref.shape[1]
    lane = lax.broadcasted_iota(jnp.int32, (T, HEAD_PAD), 1)
    for p in range(MLA_HEADS // 2):
        pair = []
        vs = slice(HEAD_PAD * p, HEAD_PAD * (p + 1))
        for hh in range(2):
            h = 2 * p + hh
            hs = slice(HEAD_PAD * h, HEAD_PAD * (h + 1))
            qh = q_ref[0, :, hs]
            s1 = _dot_nt(qh, kp_ref[0, :, hs])
            s2 = _dot_nt(qh, kn_ref[0, :, hs])
            m = jnp.maximum(jnp.max(s1, axis=-1, keepdims=True), jnp.max(s2, axis=-1, keepdims=True))
            p1 = jnp.exp2(s1 - m)
            p2 = jnp.exp2(s2 - m)
            l = jnp.sum(p1, axis=-1, keepdims=True) + jnp.sum(p2, axis=-1, keepdims=True)
            acc = _dot(p1.astype(BF16), vp_ref[0, :, vs]) + _dot(p2.astype(BF16), vn_ref[0, :, vs])
            pair.append(acc / l)
        o_ref[0, :, vs] = jnp.where(lane < V_DIM, pair[0], pair[1]).astype(BF16)


def _attn_hist_call(q, kp, vp, kn, vn):
    B, T, _ = q.shape
    P = kp.shape[1]
    blk = lambda t, w: pl.BlockSpec((1, t, w), lambda b: (b, 0, 0))
    return pl.pallas_call(
        _attn_hist_kernel,
        grid=(B,),
        in_specs=[blk(T, QK_PAD), blk(P, QK_PAD), blk(P, V_ALL), blk(T, QK_PAD), blk(T, V_ALL)],
        out_specs=blk(T, V_ALL),
        out_shape=jax.ShapeDtypeStruct((B, T, V_ALL), BF16),
        compiler_params=pltpu.CompilerParams(dimension_semantics=("parallel",), vmem_limit_bytes=VMEM_LIMIT),
        name="attn_hist",
    )(q, kp, vp, kn, vn)


MLSTM_BB = 2


def _mlstm_kernel(q_ref, k_ref, v_ref, g_ref, so_ref, s0_ref, m0_ref, h_out, sfin_out, mfin_out,
                  s_scr, m_scr, *, TL, BB):
    t = pl.program_id(1)
    L = CHUNK
    DH = MLSTM_DH

    @pl.when(t == 0)
    def _():
        s_scr[...] = s0_ref[...]
        m_scr[...] = m0_ref[...]

    r_i = lax.broadcasted_iota(jnp.int32, (L, L), 0)
    c_i = lax.broadcasted_iota(jnp.int32, (L, L), 1)
    causal = c_i <= r_i
    tri = jnp.where(causal, 1.0, 0.0).astype(BF16)
    lane = lax.broadcasted_iota(jnp.int32, (L, 128), 1)
    ones_blk = jnp.ones((L, DH), BF16)

    def rep(col):
        return jnp.broadcast_to(col, (L, 128))

    def chunk(c, carry):
        r0 = pl.multiple_of(c * L, L)
        for bb in range(BB):
            G = g_ref[bb, pl.ds(r0, L), :]
            g_hi = G.astype(BF16)
            g_r1 = G - g_hi.astype(F32)
            g_mid = g_r1.astype(BF16)
            g_lo = (g_r1 - g_mid.astype(F32)).astype(BF16)
            cum = _dot(tri, g_hi) + _dot(tri, g_mid) + _dot(tri, g_lo)
            VT = jnp.where(lane < MLSTM_HEADS, G, cum).T
            for h in range(MLSTM_HEADS):
                hs = slice(DH * h, DH * (h + 1))
                b_t = rep(jnp.sum(jnp.where(lane == MLSTM_HEADS + h, cum, 0.0), axis=1, keepdims=True))
                ig_t = rep(jnp.sum(jnp.where(lane == h, G, 0.0), axis=1, keepdims=True))
                brow = VT[MLSTM_HEADS + h:MLSTM_HEADS + h + 1, :]
                igrow = VT[h:h + 1, :]
                m_prev = m_scr[bb, h, 0:1, :]
                d = jnp.where(causal, b_t[:, :L] - brow + igrow, NEG_INF)
                g_t = b_t + m_prev
                mt = jnp.maximum(g_t, rep(jnp.max(d, axis=1, keepdims=True)))
                inter = jnp.exp(g_t - mt)
                w = jnp.exp(d - mt[:, :L])
                qh = q_ref[bb, pl.ds(r0, L), hs]
                kh = k_ref[bb, pl.ds(r0, L), hs]
                vh = v_ref[bb, pl.ds(r0, L), hs]
                qk = _dot_nt(qh, kh) * w
                St = s_scr[bb, h]
                sq = _dot(qh, St.astype(BF16))
                vext = jnp.concatenate([vh, ones_blk], axis=1)
                intra = _dot(qk.astype(BF16), vext)
                num = inter * sq[:, :DH] + intra[:, :DH]
                den = jnp.maximum(jnp.abs(inter * sq[:, DH:] + intra[:, DH:]), jnp.exp(-mt))
                so = so_ref[bb, pl.ds(r0, L), hs].astype(F32)
                h_out[bb, pl.ds(r0, L), hs] = (so * (num / den)).astype(BF16)
                mL = mt[L - 1:L, :]
                bL = b_t[L - 1:L, :]
                ws_t = jnp.exp(bL - b_t + ig_t - mL)
                decay = jnp.exp(bL + m_prev - mL)
                wv = jnp.concatenate([ws_t * vh.astype(F32), ws_t], axis=1).astype(BF16)
                kT = kh.astype(F32).T.astype(BF16)
                s_scr[bb, h] = jnp.concatenate([decay, decay], axis=1) * St + _dot(kT, wv)
                m_scr[bb, h] = jnp.broadcast_to(mL, (8, 128))
        return carry

    lax.fori_loop(0, TL // L, chunk, 0)

    @pl.when(t == pl.num_programs(1) - 1)
    def _():
        sfin_out[...] = s_scr[...]
        mfin_out[...] = m_scr[...]


def _mlstm_call(mq, mk, mv, gates, so, s0, m0):
    B, T, _ = mq.shape
    TL = min(512, T)
    BB = MLSTM_BB
    seq = lambda w: pl.BlockSpec((BB, TL, w), lambda b, t: (b, t, 0))
    st = pl.BlockSpec((BB, MLSTM_HEADS, MLSTM_DH, 2 * MLSTM_DH), lambda b, t: (b, 0, 0, 0))
    mst = pl.BlockSpec((BB, MLSTM_HEADS, 8, 128), lambda b, t: (b, 0, 0, 0))
    return pl.pallas_call(
        functools.partial(_mlstm_kernel, TL=TL, BB=BB),
        grid=(B // BB, T // TL),
        in_specs=[seq(MLSTM_WIDTH), seq(MLSTM_WIDTH), seq(MLSTM_WIDTH), seq(128), seq(MLSTM_WIDTH), st, mst],
        out_specs=[seq(MLSTM_WIDTH), st, mst],
        out_shape=[jax.ShapeDtypeStruct((B, T, MLSTM_WIDTH), BF16),
                   jax.ShapeDtypeStruct((B, MLSTM_HEADS, MLSTM_DH, 2 * MLSTM_DH), F32),
                   jax.ShapeDtypeStruct((B, MLSTM_HEADS, 8, 128), F32)],
        scratch_shapes=[pltpu.VMEM((BB, MLSTM_HEADS, MLSTM_DH, 2 * MLSTM_DH), F32),
                        pltpu.VMEM((BB, MLSTM_HEADS, 8, 128), F32)],
        compiler_params=pltpu.CompilerParams(dimension_semantics=("parallel", "arbitrary"),
                                             vmem_limit_bytes=VMEM_LIMIT),
        name="mlstm",
    )(mq, mk, mv, gates, so, s0, m0)


def _merge_kernel(x_ref, a_ref, b_ref, gu_ref, gv_ref, wgt_ref, bgt_ref, ws_ref, bsf_ref, wb_ref, wo_ref,
                  g1_ref, b1_ref, x1_out, c_scr, *, L, TM):
    x = x_ref[...]
    xb = x.astype(BF16)
    r_i = lax.broadcasted_iota(jnp.int32, (L, L), 0)
    c_i = lax.broadcasted_iota(jnp.int32, (L, L), 1)
    for g in range(GMLP_GROUPS):
        gs = slice(GMLP_DG * g, GMLP_DG * (g + 1))
        wsg = jnp.where(c_i <= r_i, ws_ref[g], 0.0).astype(BF16)
        for c in range(TM // L):
            rs = slice(L * c, L * (c + 1))
            sp = _dot(wsg, gv_ref[rs, gs]) + bsf_ref[:, gs]
            c_scr[rs, gs] = (gu_ref[rs, gs].astype(F32) * sp).astype(BF16)
    merged = None
    for kb, br in enumerate((a_ref, b_ref, c_scr)):
        cs = slice(D_MODEL * kb, D_MODEL * (kb + 1))
        gate = _sigmoid(_dot(xb, wgt_ref[:, cs]) + bgt_ref[:, cs])
        term = gate * _dot(br[...], wb_ref[kb])
        merged = term if merged is None else merged + term
    y = _dot(merged.astype(BF16), wo_ref[...])
    x1_out[...] = _layernorm(DN_ALPHA * x + y, g1_ref[...], b1_ref[...])


def _merge_call(x2d, a2d, b2d, gu, gv, mw, L):
    N = x2d.shape[0]
    TM = min(512, N)
    row = lambda w: pl.BlockSpec((TM, w), lambda i: (i, 0))
    return pl.pallas_call(
        functools.partial(_merge_kernel, L=L, TM=TM),
        grid=(N // TM,),
        in_specs=[row(D_MODEL), row(V_ALL), row(MLSTM_WIDTH), row(GMLP_WIDTH), row(GMLP_WIDTH),
                  _const_spec((D_MODEL, N_BRANCH * D_MODEL)), _const_spec((1, N_BRANCH * D_MODEL)),
                  _const_spec((GMLP_GROUPS, L, L)), _const_spec((L, GMLP_WIDTH)),
                  _const_spec((N_BRANCH, 512, D_MODEL)), _const_spec((D_MODEL, D_MODEL)),
                  _const_spec((1, D_MODEL)), _const_spec((1, D_MODEL))],
        out_specs=row(D_MODEL),
        out_shape=jax.ShapeDtypeStruct((N, D_MODEL), F32),
        scratch_shapes=[pltpu.VMEM((TM, GMLP_WIDTH), BF16)],
        compiler_params=pltpu.CompilerParams(dimension_semantics=("parallel",), vmem_limit_bytes=VMEM_LIMIT),
        name="merge",
    )(x2d, a2d, b2d, gu, gv, mw["wgt"], mw["bgt"], mw["ws"], mw["bsf"], mw["wb"], mw["wo"], mw["g1"], mw["b1"])


def _route_rows(rwt_ref, rb_ref, xb):
    s = _sigmoid(_dot_nt(rwt_ref[...], xb))
    sb = s + rb_ref[...]
    rows = [sb[e:e + 1, :] for e in range(N_EXPERTS)]
    srow = [s[e:e + 1, :] for e in range(N_EXPERTS)]
    gscore = []
    for g in range(N_GROUPS):
        mem = rows[EXPERTS_PER_GROUP * g:EXPERTS_PER_GROUP * (g + 1)]
        best = None
        for a in range(EXPERTS_PER_GROUP):
            for b in range(a + 1, EXPERTS_PER_GROUP):
                pr = mem[a] + mem[b]
                best = pr if best is None else jnp.maximum(best, pr)
        gscore.append(best)
    gmax = functools.reduce(jnp.maximum, gscore)
    taken = None
    gsel = []
    for g in range(N_GROUPS):
        hit = gscore[g] == gmax
        if taken is None:
            gsel.append(hit)
            taken = hit
        else:
            gsel.append(jnp.logical_and(hit, jnp.logical_not(taken)))
            taken = jnp.logical_or(taken, hit)
    sel_w = []
    for e in range(N_EXPERTS):
        g = e // EXPERTS_PER_GROUP
        rank = None
        for o in range(EXPERTS_PER_GROUP * g, EXPERTS_PER_GROUP * (g + 1)):
            if o == e:
                continue
            ahead = (rows[o] >= rows[e]) if o < e else (rows[o] > rows[e])
            ahead = jnp.where(ahead, 1.0, 0.0)
            rank = ahead if rank is None else rank + ahead
        chosen = jnp.logical_and(gsel[g], rank < 1.5)
        sel_w.append(jnp.where(chosen, srow[e], 0.0))
    den = functools.reduce(jnp.add, sel_w)
    gate_rows = [w_ / den for w_ in sel_w]
    return gsel, gate_rows


def _swiglu_hidden(xb, wg, wu):
    hg = _dot(xb, wg)
    return hg * _sigmoid(hg) * _dot(xb, wu)


def _moe_kernel(x_ref, rwt_ref, rb_ref, tri_ref, wgs_ref, wus_ref, wds_ref, wg_ref, wu_ref, wd_ref,
                g2_ref, b2_ref, x2_out, xb_scr, col_scr, row_scr, flag_ref, *, TM, HT, CAP):
    j = pl.program_id(1)
    nh = TM // HT
    gate_lanes = EXPERTS_PER_GROUP
    RANK_LANE, GRP_LANE = gate_lanes, gate_lanes + 1

    @pl.when(j == 0)
    def _route():
        xb = x_ref[...].astype(BF16)
        xb_scr[...] = xb
        gsel, gate_rows = _route_rows(rwt_ref, rb_ref, xb)
        isg = [jnp.where(m, 1.0, 0.0) for m in gsel]
        grp = functools.reduce(jnp.add, [float(g) * isg[g] for g in range(N_GROUPS)])
        g4 = [functools.reduce(jnp.add, [isg[g] * gate_rows[EXPERTS_PER_GROUP * g + e] for g in range(N_GROUPS)])
              for e in range(EXPERTS_PER_GROUP)]
        ranks = []
        worst = None
        for hf in range(nh):
            hsl = slice(HT * hf, HT * (hf + 1))
            m8 = jnp.concatenate([isg[g][:, hsl] for g in range(N_GROUPS)]
                                 + [jnp.zeros((8 - N_GROUPS, HT), F32)], axis=0).astype(BF16)
            before = _dot(m8, tri_ref[...])
            ranks.append(functools.reduce(jnp.add, [isg[g][:, hsl] * before[g:g + 1, :] for g in range(N_GROUPS)]))
            for g in range(N_GROUPS):
                cnt = jnp.sum(isg[g][:, hsl])
                worst = cnt if worst is None else jnp.maximum(worst, cnt)
        rank = jnp.concatenate(ranks, axis=1)
        flag_ref[0] = (worst > float(CAP)).astype(jnp.int32)
        row_scr[...] = jnp.concatenate([rank, grp, jnp.zeros((6, TM), F32)], axis=0)
        col_scr[...] = jnp.concatenate(g4 + [rank, grp, jnp.zeros((128 - gate_lanes - 2, TM), F32)], axis=0).T
        x2_out[...] = _dot(_swiglu_hidden(xb, wgs_ref[...], wus_ref[...]).astype(BF16), wds_ref[...])

    gf = j.astype(F32)
    overflow = flag_ref[0] != 0

    def gated(h, gcols):
        parts = [h[:, D_EXPERT * e:D_EXPERT * (e + 1)] * gcols[:, e:e + 1] for e in range(EXPERTS_PER_GROUP)]
        return jnp.concatenate(parts, axis=1).astype(BF16)

    @pl.when(jnp.logical_not(overflow))
    def _compact():
        r_iota = lax.broadcasted_iota(jnp.int32, (CAP, HT), 0).astype(F32)
        c_iota = lax.broadcasted_iota(jnp.int32, (HT, CAP), 1).astype(F32)
        xcs, gcs = [], []
        for hf in range(nh):
            hsl = slice(HT * hf, HT * (hf + 1))
            pick = jnp.logical_and(row_scr[1:2, hsl] == gf, row_scr[0:1, hsl] == r_iota)
            P = jnp.where(pick, 1.0, 0.0).astype(BF16)
            xcs.append(_dot(P, xb_scr[hsl, :]).astype(BF16))
            cols = col_scr[hsl, :]
            c_hi = cols.astype(BF16)
            c_lo = (cols - c_hi.astype(F32)).astype(BF16)
            gcs.append(_dot(P, c_hi) + _dot(P, c_lo))
        xc = jnp.concatenate(xcs, axis=0)
        h = gated(_swiglu_hidden(xc, wg_ref[...], wu_ref[...]), jnp.concatenate(gcs, axis=0))
        y = _dot(h, wd_ref[...]).astype(BF16)
        for hf in range(nh):
            hsl = slice(HT * hf, HT * (hf + 1))
            cols = col_scr[hsl, :]
            pick = jnp.logical_and(cols[:, GRP_LANE:GRP_LANE + 1] == gf, cols[:, RANK_LANE:RANK_LANE + 1] == c_iota)
            Pt = jnp.where(pick, 1.0, 0.0).astype(BF16)
            x2_out[hsl, :] += _dot(Pt, y[CAP * hf:CAP * (hf + 1), :])

    @pl.when(overflow)
    def _uncompacted():
        RC = min(256, TM)

        def rows(c, carry):
            rs = pl.ds(pl.multiple_of(c * RC, RC), RC)
            cols = col_scr[rs, :]
            gcols = jnp.where(cols[:, GRP_LANE:GRP_LANE + 1] == gf, cols, 0.0)
            h = gated(_swiglu_hidden(xb_scr[rs, :], wg_ref[...], wu_ref[...]), gcols)
            x2_out[rs, :] += _dot(h, wd_ref[...])
            return carry

        lax.fori_loop(0, TM // RC, rows, 0)

    @pl.when(j == pl.num_programs(1) - 1)
    def _finish():
        x2_out[...] = _layernorm(DN_ALPHA * x_ref[...] + x2_out[...], g2_ref[...], b2_ref[...])


MOE_TM = 1024
MOE_CAP = 160


def _moe_call(x2d, ew):
    N = x2d.shape[0]
    TM = min(MOE_TM, N)
    HT = TM // 2
    CAP = min(MOE_CAP, HT)
    GW = EXPERTS_PER_GROUP * D_EXPERT
    row = pl.BlockSpec((TM, D_MODEL), lambda i, j: (i, 0))
    return pl.pallas_call(
        functools.partial(_moe_kernel, TM=TM, HT=HT, CAP=CAP),
        grid=(N // TM, N_GROUPS),
        in_specs=[row, _const_spec((N_EXPERTS, D_MODEL)), _const_spec((N_EXPERTS, 1)), _const_spec((HT, HT)),
                  _const_spec((D_MODEL, D_SHARED)), _const_spec((D_MODEL, D_SHARED)), _const_spec((D_SHARED, D_MODEL)),
                  pl.BlockSpec((D_MODEL, GW), lambda i, j: (0, j)), pl.BlockSpec((D_MODEL, GW), lambda i, j: (0, j)),
                  pl.BlockSpec((GW, D_MODEL), lambda i, j: (j, 0)),
                  _const_spec((1, D_MODEL)), _const_spec((1, D_MODEL))],
        out_specs=row,
        out_shape=jax.ShapeDtypeStruct((N, D_MODEL), F32),
        scratch_shapes=[pltpu.VMEM((TM, D_MODEL), BF16), pltpu.VMEM((TM, 128), F32), pltpu.VMEM((8, TM), F32),
                        pltpu.SMEM((1,), jnp.int32)],
        compiler_params=pltpu.CompilerParams(dimension_semantics=("parallel", "arbitrary"),
                                             vmem_limit_bytes=VMEM_LIMIT),
        name="moe",
    )(x2d, ew["rwt"], ew["rb"], jnp.triu(jnp.ones((HT, HT), BF16), 1), ew["wgs"], ew["wus"], ew["wds"],
      ew["wg"], ew["wu"], ew["wd"], ew["g2"], ew["b2"])


def _rot_cols(w):
    half = w.shape[-1] // 2
    return jnp.concatenate([-w[..., half:], w[..., :half]], axis=-1)


def _prep_layer(l, w_in, b_in, q_norm_g, kv_norm_g, w_uq, w_ukv, gmlp_ln_g, gmlp_ln_b, gmlp_ws, gmlp_bs,
                w_branch, w_out, ln1_g, ln1_b, moe_w_gate, moe_w_up, moe_w_down, shared_w_gate,
                shared_w_up, shared_w_down, ln2_g, ln2_b):
    wi = w_in[l]
    bi = b_in[l][None, :]

    def proj_cols(m):
        rows = m.shape[0]
        zero = lambda n: jnp.zeros((rows, n), m.dtype)
        kr = m[:, O_KR:O_MQ]
        kr128 = jnp.concatenate([zero(KR_LANE), kr, zero(HEAD_PAD - KR_LANE - ROPE_DIM)], axis=1)
        krr128 = jnp.concatenate([zero(KR_LANE), _rot_cols(kr), zero(HEAD_PAD - KR_LANE - ROPE_DIM)], axis=1)
        gates = jnp.concatenate([m[:, O_MI:O_MO], zero(HEAD_PAD - 2 * MLSTM_HEADS)], axis=1)
        return jnp.concatenate([m[:, O_CQ:O_KR], kr128, krr128, m[:, O_MQ:O_MI], gates, m[:, O_MO:O_GT]], axis=1)

    uq = w_uq[l].reshape(Q_RANK, MLA_HEADS, NOPE_DIM + ROPE_DIM)
    zq = lambda n: jnp.zeros((Q_RANK, MLA_HEADS, n), F32)
    pad = HEAD_PAD - NOPE_DIM - ROPE_DIM
    wq_a = jnp.concatenate([uq, zq(pad)], axis=-1).reshape(Q_RANK, QK_PAD)
    wq_b = jnp.concatenate([zq(NOPE_DIM), _rot_cols(uq[..., NOPE_DIM:]), zq(pad)], axis=-1).reshape(Q_RANK, QK_PAD)
    ukv = w_ukv[l].reshape(KV_RANK, MLA_HEADS, NOPE_DIM + V_DIM)
    wk = jnp.concatenate([ukv[..., :NOPE_DIM], jnp.zeros((KV_RANK, MLA_HEADS, HEAD_PAD - NOPE_DIM), F32)],
                         axis=-1).reshape(KV_RANK, QK_PAD)
    wv = ukv[..., NOPE_DIM:].reshape(KV_RANK, V_ALL)
    pw = dict(
        w1=proj_cols(wi).astype(BF16), b1=proj_cols(bi),
        qg=q_norm_g[l][None, :], kvg=kv_norm_g[l][None, :],
        wq=jnp.concatenate([wq_a, wq_b], axis=1).astype(BF16),
        wkv=jnp.concatenate([wk, wv], axis=1).astype(BF16),
        lng=gmlp_ln_g[l][None, :], lnb=gmlp_ln_b[l][None, :],
    )
    mw = dict(
        wgt=wi[:, O_GT:].astype(BF16), bgt=bi[:, O_GT:],
        ws_full=gmlp_ws[l], bs_full=gmlp_bs[l],
        wb=w_branch[l].astype(BF16), wo=w_out[l].astype(BF16),
        g1=ln1_g[l][None, :], b1=ln1_b[l][None, :],
    )
    cat_in = lambda we: jnp.transpose(we[l].astype(BF16), (1, 0, 2)).reshape(D_MODEL, N_EXPERTS * D_EXPERT)
    ew = dict(
        wg=cat_in(moe_w_gate), wu=cat_in(moe_w_up),
        wd=moe_w_down[l].reshape(N_EXPERTS * D_EXPERT, D_MODEL).astype(BF16),
        wgs=shared_w_gate[l].astype(BF16), wus=shared_w_up[l].astype(BF16), wds=shared_w_down[l].astype(BF16),
        g2=ln2_g[l][None, :], b2=ln2_b[l][None, :],
    )
    return pw, mw, ew


def _rope_tables(T, past, rows):
    half = ROPE_DIM // 2
    pos = (past + jnp.arange(T)).astype(F32)
    inv = ROPE_THETA ** (-jnp.arange(half, dtype=F32) / half)
    ang = pos[:, None] * inv[None, :]
    cos = jnp.cos(ang)
    sin = jnp.sin(ang)
    c2 = jnp.concatenate([cos, cos], axis=1)
    s2 = jnp.concatenate([sin, sin], axis=1)
    z = lambda n: jnp.zeros((T, n), F32)
    tail = HEAD_PAD - NOPE_DIM - ROPE_DIM
    tabs = dict(
        cq=jnp.concatenate([jnp.ones((T, NOPE_DIM), F32), c2, z(tail)], axis=1) * Q_SCALE,
        sq=jnp.concatenate([z(NOPE_DIM), s2, z(tail)], axis=1) * Q_SCALE,
        ck=jnp.concatenate([z(KR_LANE), c2, z(tail)], axis=1),
        sk=jnp.concatenate([z(KR_LANE), s2, z(tail)], axis=1),
    )
    if rows > T:
        tabs = {k: jnp.tile(v, (rows // T, 1)) for k, v in tabs.items()}
    return tabs


def _trunk(x, hist, layers, rwt, rb):
    B, T, _ = x.shape
    N = B * T
    past = 0 if hist is None else hist[0].shape[2]
    tabs = _rope_tables(T, past, min(512, N))
    L = GMLP_CHUNK if T % GMLP_CHUNK == 0 else T
    x2d = x.reshape(N, D_MODEL)
    ckvs, krs, Cs, ns, ms, gvs = [], [], [], [], [], []
    for l, (pw, mw, ew) in enumerate(layers):
        outs = _proj_call(x2d, pw, tabs, T, hist is not None)
        q, k, v, ckvn, kr, mq, mk, mv, gates, so, gu, gv = outs[:12]
        seq = lambda a: a.reshape(B, T, a.shape[-1])
        if hist is None:
            a_out = _attn_call(seq(q), seq(k), seq(v))
            s0 = jnp.zeros((B, MLSTM_HEADS, MLSTM_DH, 2 * MLSTM_DH), F32)
            m0 = jnp.zeros((B, MLSTM_HEADS, 8, 128), F32)
        else:
            ckv_past, kr_past, c0, n0, m0_in = hist
            P = ckv_past.shape[2]
            krp = jnp.pad(kr_past[l].reshape(B * P, ROPE_DIM),
                          ((0, 0), (KR_LANE, HEAD_PAD - KR_LANE - ROPE_DIM)))
            kp, vp = _pastkv_call(ckv_past[l].reshape(B * P, KV_RANK), krp, pw["wkv"])
            a_out = _attn_hist_call(seq(q), kp.reshape(B, P, QK_PAD), vp.reshape(B, P, V_ALL), seq(k), seq(v))
            s0 = jnp.concatenate(
                [jnp.swapaxes(c0[l], -1, -2),
                 jnp.broadcast_to(n0[l][:, :, :, None], (B, MLSTM_HEADS, MLSTM_DH, MLSTM_DH))], axis=3)
            m0 = jnp.broadcast_to(m0_in[l][:, :, None, None], (B, MLSTM_HEADS, 8, 128))
        b_out, sfin, mfin = _mlstm_call(seq(mq), seq(mk), seq(mv), seq(gates), seq(so), s0, m0)
        mwl = dict(mw)
        mwl["ws"] = mw["ws_full"][:, :L, :L]
        mwl["bsf"] = jnp.repeat(mw["bs_full"][:, :L].T, GMLP_DG, axis=1)
        x1 = _merge_call(x2d, a_out.reshape(N, V_ALL), b_out.reshape(N, MLSTM_WIDTH), gu, gv, mwl, L)
        x2d = _moe_call(x1, dict(ew, rwt=rwt, rb=rb))
        ckvs.append(ckvn.reshape(B, T, KV_RANK))
        krs.append(kr[:, KR_LANE:KR_LANE + ROPE_DIM].reshape(B, T, ROPE_DIM))
        Cs.append(jnp.swapaxes(sfin[:, :, :, :MLSTM_DH], -1, -2))
        ns.append(sfin[:, :, :, MLSTM_DH])
        ms.append(mfin[:, :, 0, 0])
        if hist is not None:
            gvs.append(outs[12].reshape(B, T, GMLP_WIDTH))
    res = [x2d.reshape(B, T, D_MODEL), jnp.stack(ckvs), jnp.stack(krs), jnp.stack(Cs), jnp.stack(ns), jnp.stack(ms)]
    if hist is not None:
        res.append(jnp.stack(gvs))
    return res


def kernel(x_prompt, x_sample, cache_mla_ckv, cache_mla_krope, state_mlstm_c, state_mlstm_n, state_mlstm_m,
           w_in, b_in, q_norm_g, kv_norm_g, w_uq, w_ukv, gmlp_ln_g, gmlp_ln_b, gmlp_ws, gmlp_bs, w_branch,
           w_out, ln1_g, ln1_b, router_w, router_b, moe_w_gate, moe_w_up, moe_w_down, shared_w_gate,
           shared_w_up, shared_w_down, ln2_g, ln2_b):
    depth = w_in.shape[0]
    layers = [_prep_layer(l, w_in, b_in, q_norm_g, kv_norm_g, w_uq, w_ukv, gmlp_ln_g, gmlp_ln_b, gmlp_ws,
                          gmlp_bs, w_branch, w_out, ln1_g, ln1_b, moe_w_gate, moe_w_up, moe_w_down,
                          shared_w_gate, shared_w_up, shared_w_down, ln2_g, ln2_b) for l in range(depth)]
    rwt = router_w.T.astype(BF16)
    rb = router_b[:, None]
    yp, p_ckv, p_kr, p_c, p_n, p_m = _trunk(x_prompt, None, layers, rwt, rb)
    ys, s_ckv, s_kr, s_c, s_n, s_m, s_gv = _trunk(
        x_sample, (cache_mla_ckv, cache_mla_krope, state_mlstm_c, state_mlstm_n, state_mlstm_m), layers, rwt, rb)
    return (yp, ys, p_ckv, p_kr, p_c, p_n, p_m, s_ckv, s_kr, s_c, s_n, s_m, s_gv)
```

```python
import functools

import jax
import jax.numpy as jnp
import numpy as np
from jax import lax
from jax.experimental import pallas as pl
from jax.experimental.pallas import tpu as pltpu

F32 = jnp.float32
BF16 = jnp.bfloat16

D_MODEL = 1024
CHUNK = 64
MLA_HEADS = 8
Q_RANK = 256
KV_RANK = 256
NOPE_DIM = 64
ROPE_DIM = 32
V_DIM = 64
ROPE_THETA = 10000.0
ATTN_SCALE = (NOPE_DIM + ROPE_DIM) ** -0.5
Q_SCALE = ATTN_SCALE * float(np.log2(np.e))
MLSTM_HEADS = 4
MLSTM_DH = 128
MLSTM_WIDTH = MLSTM_HEADS * MLSTM_DH
GMLP_GROUPS = 4
GMLP_DG = 128
GMLP_WIDTH = GMLP_GROUPS * GMLP_DG
GMLP_CHUNK = 128
N_BRANCH = 3
N_EXPERTS = 16
N_GROUPS = 4
EXPERTS_PER_GROUP = N_EXPERTS // N_GROUPS
D_EXPERT = 256
D_SHARED = 256
DEPTH = 4
DN_ALPHA = (2 * DEPTH) ** 0.25
EPS = 1e-5

HEAD_PAD = 128
QK_PAD = MLA_HEADS * HEAD_PAD
V_ALL = MLA_HEADS * V_DIM
P_CQ, P_CKV, P_KR, P_KRR, P_MQ, P_MK, P_MV, P_G, P_MO, P_GU, P_GV = (
    0, 256, 512, 640, 768, 1280, 1792, 2304, 2432, 2944, 3456)
D_PROJ = 3968
O_CQ, O_CKV, O_KR, O_MQ, O_MK, O_MV, O_MI, O_MF, O_MO, O_GU, O_GV, O_GT = (
    0, 256, 512, 544, 1056, 1568, 2080, 2084, 2088, 2600, 3112, 3624)
KR_LANE = NOPE_DIM
VMEM_LIMIT = 56 * 1024 * 1024
NEG_INF = float("-inf")


def _dot(a, b):
    return jnp.dot(a, b, preferred_element_type=F32)


def _dot_nt(a, b):
    return lax.dot_general(a, b, (((1,), (1,)), ((), ())), preferred_element_type=F32)


def _const_spec(shape):
    nd = len(shape)
    return pl.BlockSpec(shape, lambda *_: (0,) * nd, pipeline_mode=pl.Buffered(1))


def _layernorm(x, g, b):
    mu = jnp.mean(x, axis=-1, keepdims=True)
    xc = x - mu
    var = jnp.mean(xc * xc, axis=-1, keepdims=True)
    return xc * lax.rsqrt(var + EPS) * g + b


def _rmsnorm(x, g):
    return x * lax.rsqrt(jnp.mean(x * x, axis=-1, keepdims=True) + EPS) * g


def _gelu(x):
    return 0.5 * x * (1.0 + jnp.tanh(np.sqrt(2.0 / np.pi).astype(np.float32) * (x + 0.044715 * (x * x * x))))


def _sigmoid(x):
    return 1.0 / (1.0 + jnp.exp(-x))


def _log_sigmoid(x):
    return jnp.minimum(x, 0.0) - jnp.log(1.0 + jnp.exp(-jnp.abs(x)))


def _proj_kernel(x_ref, w1_ref, b1_ref, qg_ref, kvg_ref, wq_ref, wkv_ref, cq_ref, sq_ref, ck_ref, sk_ref,
                 lng_ref, lnb_ref,
                 q_out, k_out, v_out, ckv_out, kr_out, mq_out, mk_out, mv_out, g_out, so_out, gu_out,
                 gv_out, *rest):
    xb = x_ref[...].astype(BF16)

    def z(lo, hi):
        return _dot(xb, w1_ref[:, lo:hi]) + b1_ref[:, lo:hi]

    zc = z(P_CQ, P_MQ)
    cqn = _rmsnorm(zc[:, P_CQ:P_CKV], qg_ref[...])
    ckvn = _rmsnorm(zc[:, P_CKV:P_KR], kvg_ref[...])
    kr = zc[:, P_KR:P_KRR] * ck_ref[...] + zc[:, P_KRR:P_MQ] * sk_ref[...]
    ckv_out[...] = ckvn
    kr_out[...] = kr
    qq = _dot(cqn.astype(BF16), wq_ref[...])
    cos8 = jnp.concatenate([cq_ref[...]] * MLA_HEADS, axis=1)
    sin8 = jnp.concatenate([sq_ref[...]] * MLA_HEADS, axis=1)
    q_out[...] = (qq[:, :QK_PAD] * cos8 + qq[:, QK_PAD:] * sin8).astype(BF16)
    kk = _dot(ckvn.astype(BF16), wkv_ref[...])
    kr8 = jnp.concatenate([kr] * MLA_HEADS, axis=1)
    k_out[...] = (kk[:, :QK_PAD] + kr8).astype(BF16)
    v_out[...] = kk[:, QK_PAD:].astype(BF16)

    zm = z(P_MQ, P_MO)
    mq_out[...] = zm[:, 0:512].astype(BF16)
    mk_out[...] = (zm[:, 512:1024] * (MLSTM_DH ** -0.5)).astype(BF16)
    mv_out[...] = zm[:, 1024:1536].astype(BF16)
    zg = zm[:, 1536:1664]
    lane = lax.broadcasted_iota(jnp.int32, zg.shape, 1)
    g_out[...] = jnp.where(lane < MLSTM_HEADS, zg, _log_sigmoid(zg))
    so_out[...] = _sigmoid(z(P_MO, P_GU)).astype(BF16)

    zu = z(P_GU, D_PROJ)
    gu_out[...] = _gelu(zu[:, :GMLP_WIDTH]).astype(BF16)
    vrows = _layernorm(_gelu(zu[:, GMLP_WIDTH:]), lng_ref[...], lnb_ref[...])
    gv_out[...] = vrows.astype(BF16)
    if rest:
        rest[0][...] = vrows


def _proj_call(x2d, pw, tabs, T, want_vrows):
    N = x2d.shape[0]
    TM = min(512, N)
    nt = max(1, T // TM)
    grid = (N // TM,)
    row = lambda w: pl.BlockSpec((TM, w), lambda i: (i, 0))
    tab = pl.BlockSpec((TM, HEAD_PAD), lambda i: (i % nt, 0))
    in_specs = [
        row(D_MODEL),
        _const_spec((D_MODEL, D_PROJ)), _const_spec((1, D_PROJ)),
        _const_spec((1, Q_RANK)), _const_spec((1, KV_RANK)),
        _const_spec((Q_RANK, 2 * QK_PAD)), _const_spec((KV_RANK, QK_PAD + V_ALL)),
        tab, tab, tab, tab,
        _const_spec((1, GMLP_WIDTH)), _const_spec((1, GMLP_WIDTH)),
    ]
    out_shapes = [
        ((N, QK_PAD), BF16), ((N, QK_PAD), BF16), ((N, V_ALL), BF16), ((N, KV_RANK), F32),
        ((N, HEAD_PAD), F32), ((N, MLSTM_WIDTH), BF16), ((N, MLSTM_WIDTH), BF16), ((N, MLSTM_WIDTH), BF16),
        ((N, HEAD_PAD), F32), ((N, MLSTM_WIDTH), BF16), ((N, GMLP_WIDTH), BF16), ((N, GMLP_WIDTH), BF16),
    ]
    if want_vrows:
        out_shapes.append(((N, GMLP_WIDTH), F32))
    return pl.pallas_call(
        _proj_kernel,
        grid=grid,
        in_specs=in_specs,
        out_specs=[row(s[1]) for s, _ in out_shapes],
        out_shape=[jax.ShapeDtypeStruct(s, d) for s, d in out_shapes],
        compiler_params=pltpu.CompilerParams(dimension_semantics=("parallel",), vmem_limit_bytes=VMEM_LIMIT),
        name="proj",
    )(x2d, pw["w1"], pw["b1"], pw["qg"], pw["kvg"], pw["wq"], pw["wkv"],
      tabs["cq"], tabs["sq"], tabs["ck"], tabs["sk"], pw["lng"], pw["lnb"])


def _pastkv_kernel(c_ref, kr_ref, wkv_ref, k_out, v_out):
    kk = _dot(c_ref[...].astype(BF16), wkv_ref[...])
    kr8 = jnp.concatenate([kr_ref[...]] * MLA_HEADS, axis=1)
    k_out[...] = (kk[:, :QK_PAD] + kr8).astype(BF16)
    v_out[...] = kk[:, QK_PAD:].astype(BF16)


def _pastkv_call(ckv2d, kr2d, wkv):
    N = ckv2d.shape[0]
    TM = min(1024, N)
    row = lambda w: pl.BlockSpec((TM, w), lambda i: (i, 0))
    return pl.pallas_call(
        _pastkv_kernel,
        grid=(N // TM,),
        in_specs=[row(KV_RANK), row(HEAD_PAD), _const_spec((KV_RANK, QK_PAD + V_ALL))],
        out_specs=[row(QK_PAD), row(V_ALL)],
        out_shape=[jax.ShapeDtypeStruct((N, QK_PAD), BF16), jax.ShapeDtypeStruct((N, V_ALL), BF16)],
        compiler_params=pltpu.CompilerParams(dimension_semantics=("parallel",), vmem_limit_bytes=VMEM_LIMIT),
        name="pastkv",
    )(ckv2d, kr2d, wkv)


def _attn_kernel(q_ref, k_ref, v_ref, o_ref, mx_scr, acc_scr, s_scr, *, TQ):
    i = pl.program_id(1)
    ones_blk = jnp.ones((TQ, HEAD_PAD), BF16)
    rc = lax.broadcasted_iota(jnp.int32, (TQ, TQ), 0) // CHUNK
    cc = lax.broadcasted_iota(jnp.int32, (TQ, TQ), 1) // CHUNK
    visible = cc <= rc
    lane = lax.broadcasted_iota(jnp.int32, (TQ, HEAD_PAD), 1)
    nlane = TQ // 128

    def fold(a, op):
        r = a[:, 0:128]
        for t in range(1, nlane):
            r = op(r, a[:, 128 * t:128 * (t + 1)])
        return r

    def sweep_max(j, masked, first):
        r0 = pl.multiple_of(j * TQ, TQ)
        for h in range(MLA_HEADS):
            hs = slice(HEAD_PAD * h, HEAD_PAD * (h + 1))
            s = _dot_nt(q_ref[0, :, hs], k_ref[0, pl.ds(r0, TQ), hs])
            if masked:
                s = jnp.where(visible, s, NEG_INF)
            s_scr[h, j] = s
            mx = fold(s, jnp.maximum)
            mx_scr[h] = mx if first else jnp.maximum(mx_scr[h], mx)

    def sweep_pv(j, first):
        r0 = pl.multiple_of(j * TQ, TQ)
        for h in range(MLA_HEADS):
            vs = slice(HEAD_PAD * (h // 2), HEAD_PAD * (h // 2 + 1))
            mb = mx_scr[h]
            p = jnp.exp2(s_scr[h, j] - jnp.concatenate([mb] * nlane, axis=1))
            vext = jnp.concatenate([v_ref[0, pl.ds(r0, TQ), vs], ones_blk], axis=1)
            pv = _dot(p.astype(BF16), vext)
            acc_scr[h] = pv if first else acc_scr[h] + pv

    def body_max(j, c):
        sweep_max(j, False, False)
        return c

    def body_pv(j, c):
        sweep_pv(j, False)
        return c

    sweep_max(i, True, True)
    lax.fori_loop(0, i, body_max, 0)
    for h in range(MLA_HEADS):
        mx_scr[h] = jnp.broadcast_to(jnp.max(mx_scr[h], axis=-1, keepdims=True), (TQ, 128))
    sweep_pv(i, True)
    lax.fori_loop(0, i, body_pv, 0)
    for p in range(MLA_HEADS // 2):
        a0 = acc_scr[2 * p]
        a1 = acc_scr[2 * p + 1]
        o0 = a0[:, :HEAD_PAD] / a0[:, HEAD_PAD:]
        o1 = a1[:, :HEAD_PAD] / a1[:, HEAD_PAD:]
        o_ref[0, :, HEAD_PAD * p:HEAD_PAD * (p + 1)] = jnp.where(lane < V_DIM, o0, o1).astype(BF16)


def _attn_call(q, k, v):
    B, T, _ = q.shape
    TQ = min(256, T)
    nq = T // TQ
    return pl.pallas_call(
        functools.partial(_attn_kernel, TQ=TQ),
        grid=(B, nq),
        in_specs=[pl.BlockSpec((1, TQ, QK_PAD), lambda b, i: (b, i, 0)),
                  pl.BlockSpec((1, T, QK_PAD), lambda b, i: (b, 0, 0)),
                  pl.BlockSpec((1, T, V_ALL), lambda b, i: (b, 0, 0))],
        out_specs=pl.BlockSpec((1, TQ, V_ALL), lambda b, i: (b, i, 0)),
        out_shape=jax.ShapeDtypeStruct((B, T, V_ALL), BF16),
        scratch_shapes=[pltpu.VMEM((MLA_HEADS, TQ, 128), F32), pltpu.VMEM((MLA_HEADS, TQ, 2 * HEAD_PAD), F32),
                        pltpu.VMEM((MLA_HEADS, nq, TQ, TQ), F32)],
        compiler_params=pltpu.CompilerParams(dimension_semantics=("parallel", "arbitrary"),
                                             vmem_limit_bytes=VMEM_LIMIT),
        name="attn",
    )(q, k, v)


def _attn_hist_kernel(q_ref, kp_ref, vp_ref, kn_ref, vn_ref, o_ref):
    T = q_ref.shape[1]
    lane = lax.broadcasted_iota(jnp.int32, (T, HEAD_PAD), 1)
    for p in range(MLA_HEADS // 2):
        pair = []
        vs = slice(HEAD_PAD * p, HEAD_PAD * (p + 1))
        for hh in range(2):
            h = 2 * p + hh
            hs = slice(HEAD_PAD * h, HEAD_PAD * (h + 1))
            qh = q_ref[0, :, hs]
            s1 = _dot_nt(qh, kp_ref[0, :, hs])
            s2 = _dot_nt(qh, kn_ref[0, :, hs])
            m = jnp.maximum(jnp.max(s1, axis=-1, keepdims=True), jnp.max(s2, axis=-1, keepdims=True))
            p1 = jnp.exp2(s1 - m)
            p2 = jnp.exp2(s2 - m)
            l = jnp.sum(p1, axis=-1, keepdims=True) + jnp.sum(p2, axis=-1, keepdims=True)
            acc = _dot(p1.astype(BF16), vp_ref[0, :, vs]) + _dot(p2.astype(BF16), vn_ref[0, :, vs])
            pair.append(acc / l)
        o_ref[0, :, vs] = jnp.where(lane < V_DIM, pair[0], pair[1]).astype(BF16)


def _attn_hist_call(q, kp, vp, kn, vn):
    B, T, _ = q.shape
    P = kp.shape[1]
    blk = lambda t, w: pl.BlockSpec((1, t, w), lambda b: (b, 0, 0))
    return pl.pallas_call(
        _attn_hist_kernel,
        grid=(B,),
        in_specs=[blk(T, QK_PAD), blk(P, QK_PAD), blk(P, V_ALL), blk(T, QK_PAD), blk(T, V_ALL)],
        out_specs=blk(T, V_ALL),
        out_shape=jax.ShapeDtypeStruct((B, T, V_ALL), BF16),
        compiler_params=pltpu.CompilerParams(dimension_semantics=("parallel",), vmem_limit_bytes=VMEM_LIMIT),
        name="attn_hist",
    )(q, kp, vp, kn, vn)


MLSTM_BB = 4


def _mlstm_kernel(q_ref, k_ref, v_ref, g_ref, so_ref, s0_ref, m0_ref, h_out, sfin_out, mfin_out,
                  s_scr, m_scr, *, TL, BB):
    t = pl.program_id(1)
    L = CHUNK
    DH = MLSTM_DH

    @pl.when(t == 0)
    def _():
        s_scr[...] = s0_ref[...]
        m_scr[...] = m0_ref[...]

    r_i = lax.broadcasted_iota(jnp.int32, (L, L), 0)
    c_i = lax.broadcasted_iota(jnp.int32, (L, L), 1)
    causal = c_i <= r_i
    tri = jnp.where(causal, 1.0, 0.0).astype(BF16)
    lane = lax.broadcasted_iota(jnp.int32, (L, 128), 1)
    ones_blk = jnp.ones((L, DH), BF16)

    def rep(col):
        return jnp.broadcast_to(col, (L, 128))

    def chunk(c, carry):
        rows = pl.ds(pl.multiple_of(c * L, L), L)
        pairs = [(bb, h) for bb in range(BB) for h in range(MLSTM_HEADS)]
        hsl = lambda h: slice(DH * h, DH * (h + 1))
        Gs, cums, VTs = [], [], []
        for bb in range(BB):
            G = g_ref[bb, rows, :]
            g_hi = G.astype(BF16)
            g_r1 = G - g_hi.astype(F32)
            g_mid = g_r1.astype(BF16)
            g_lo = (g_r1 - g_mid.astype(F32)).astype(BF16)
            cum = _dot(tri, g_hi) + _dot(tri, g_mid) + _dot(tri, g_lo)
            Gs.append(G)
            cums.append(cum)
            VTs.append(jnp.where(lane < MLSTM_HEADS, G, cum).T)
        qk_raw = {p: _dot_nt(q_ref[p[0], rows, hsl(p[1])], k_ref[p[0], rows, hsl(p[1])]) for p in pairs}
        sq = {p: _dot(q_ref[p[0], rows, hsl(p[1])], s_scr[p[0], p[1]].astype(BF16)) for p in pairs}
        gate = {}
        for bb, h in pairs:
            b_t = rep(jnp.sum(jnp.where(lane == MLSTM_HEADS + h, cums[bb], 0.0), axis=1, keepdims=True))
            ig_t = rep(jnp.sum(jnp.where(lane == h, Gs[bb], 0.0), axis=1, keepdims=True))
            brow = VTs[bb][MLSTM_HEADS + h:MLSTM_HEADS + h + 1, :]
            igrow = VTs[bb][h:h + 1, :]
            d = jnp.where(causal, b_t[:, :L] - brow + igrow, NEG_INF)
            gate[bb, h] = (b_t, ig_t, d, rep(jnp.max(d, axis=1, keepdims=True)))
        stab = {}
        for bb, h in pairs:
            b_t, ig_t, d, dmax = gate[bb, h]
            m_prev = m_scr[bb, h, 0:1, :]
            g_t = b_t + m_prev
            mt = jnp.maximum(g_t, dmax)
            stab[bb, h] = (m_prev, mt, jnp.exp(g_t - mt), jnp.exp(d - mt[:, :L]))
        for bb, h in pairs:
            m_prev, mt, inter, w = stab[bb, h]
            vext = jnp.concatenate([v_ref[bb, rows, hsl(h)], ones_blk], axis=1)
            intra = _dot((qk_raw[bb, h] * w).astype(BF16), vext)
            num = inter * sq[bb, h][:, :DH] + intra[:, :DH]
            den = jnp.maximum(jnp.abs(inter * sq[bb, h][:, DH:] + intra[:, DH:]), jnp.exp(-mt))
            so = so_ref[bb, rows, hsl(h)].astype(F32)
            h_out[bb, rows, hsl(h)] = (so * (num / den)).astype(BF16)
        for bb, h in pairs:
            b_t, ig_t, _, _ = gate[bb, h]
            m_prev, mt, _, _ = stab[bb, h]
            mL = mt[L - 1:L, :]
            bL = b_t[L - 1:L, :]
            ws_t = jnp.exp(bL - b_t + ig_t - mL)
            decay = jnp.exp(bL + m_prev - mL)
            wv = jnp.concatenate([ws_t * v_ref[bb, rows, hsl(h)].astype(F32), ws_t], axis=1).astype(BF16)
            kT = k_ref[bb, rows, hsl(h)].astype(F32).T.astype(BF16)
            s_scr[bb, h] = jnp.concatenate([decay, decay], axis=1) * s_scr[bb, h] + _dot(kT, wv)
            m_scr[bb, h] = jnp.broadcast_to(mL, (8, 128))
        return carry

    lax.fori_loop(0, TL // L, chunk, 0)

    @pl.when(t == pl.num_programs(1) - 1)
    def _():
        sfin_out[...] = s_scr[...]
        mfin_out[...] = m_scr[...]


def _mlstm_call(mq, mk, mv, gates, so, s0, m0):
    B, T, _ = mq.shape
    TL = min(512, T)
    BB = int(np.gcd(MLSTM_BB, B))
    seq = lambda w: pl.BlockSpec((BB, TL, w), lambda b, t: (b, t, 0))
    st = pl.BlockSpec((BB, MLSTM_HEADS, MLSTM_DH, 2 * MLSTM_DH), lambda b, t: (b, 0, 0, 0))
    mst = pl.BlockSpec((BB, MLSTM_HEADS, 8, 128), lambda b, t: (b, 0, 0, 0))
    return pl.pallas_call(
        functools.partial(_mlstm_kernel, TL=TL, BB=BB),
        grid=(B // BB, T // TL),
        in_specs=[seq(MLSTM_WIDTH), seq(MLSTM_WIDTH), seq(MLSTM_WIDTH), seq(128), seq(MLSTM_WIDTH), st, mst],
        out_specs=[seq(MLSTM_WIDTH), st, mst],
        out_shape=[jax.ShapeDtypeStruct((B, T, MLSTM_WIDTH), BF16),
                   jax.ShapeDtypeStruct((B, MLSTM_HEADS, MLSTM_DH, 2 * MLSTM_DH), F32),
                   jax.ShapeDtypeStruct((B, MLSTM_HEADS, 8, 128), F32)],
        scratch_shapes=[pltpu.VMEM((BB, MLSTM_HEADS, MLSTM_DH, 2 * MLSTM_DH), F32),
                        pltpu.VMEM((BB, MLSTM_HEADS, 8, 128), F32)],
        compiler_params=pltpu.CompilerParams(dimension_semantics=("parallel", "arbitrary"),
                                             vmem_limit_bytes=VMEM_LIMIT),
        name="mlstm",
    )(mq, mk, mv, gates, so, s0, m0)


def _merge_kernel(x_ref, a_ref, b_ref, gu_ref, gv_ref, wgt_ref, bgt_ref, ws_ref, bsf_ref, wb_ref, wo_ref,
                  g1_ref, b1_ref, x1_out, c_scr, *, L, TM):
    x = x_ref[...]
    xb = x.astype(BF16)
    r_i = lax.broadcasted_iota(jnp.int32, (L, L), 0)
    c_i = lax.broadcasted_iota(jnp.int32, (L, L), 1)
    for g in range(GMLP_GROUPS):
        gs = slice(GMLP_DG * g, GMLP_DG * (g + 1))
        wsg = jnp.where(c_i <= r_i, ws_ref[g], 0.0).astype(BF16)
        for c in range(TM // L):
            rs = slice(L * c, L * (c + 1))
            sp = _dot(wsg, gv_ref[rs, gs]) + bsf_ref[:, gs]
            c_scr[rs, gs] = (gu_ref[rs, gs].astype(F32) * sp).astype(BF16)
    merged = None
    for kb, br in enumerate((a_ref, b_ref, c_scr)):
        cs = slice(D_MODEL * kb, D_MODEL * (kb + 1))
        gate = _sigmoid(_dot(xb, wgt_ref[:, cs]) + bgt_ref[:, cs])
        term = gate * _dot(br[...], wb_ref[kb])
        merged = term if merged is None else merged + term
    y = _dot(merged.astype(BF16), wo_ref[...])
    x1_out[...] = _layernorm(DN_ALPHA * x + y, g1_ref[...], b1_ref[...])


def _merge_call(x2d, a2d, b2d, gu, gv, mw, L):
    N = x2d.shape[0]
    TM = min(512, N)
    row = lambda w: pl.BlockSpec((TM, w), lambda i: (i, 0))
    return pl.pallas_call(
        functools.partial(_merge_kernel, L=L, TM=TM),
        grid=(N // TM,),
        in_specs=[row(D_MODEL), row(V_ALL), row(MLSTM_WIDTH), row(GMLP_WIDTH), row(GMLP_WIDTH),
                  _const_spec((D_MODEL, N_BRANCH * D_MODEL)), _const_spec((1, N_BRANCH * D_MODEL)),
                  _const_spec((GMLP_GROUPS, L, L)), _const_spec((L, GMLP_WIDTH)),
                  _const_spec((N_BRANCH, 512, D_MODEL)), _const_spec((D_MODEL, D_MODEL)),
                  _const_spec((1, D_MODEL)), _const_spec((1, D_MODEL))],
        out_specs=row(D_MODEL),
        out_shape=jax.ShapeDtypeStruct((N, D_MODEL), F32),
        scratch_shapes=[pltpu.VMEM((TM, GMLP_WIDTH), BF16)],
        compiler_params=pltpu.CompilerParams(dimension_semantics=("parallel",), vmem_limit_bytes=VMEM_LIMIT),
        name="merge",
    )(x2d, a2d, b2d, gu, gv, mw["wgt"], mw["bgt"], mw["ws"], mw["bsf"], mw["wb"], mw["wo"], mw["g1"], mw["b1"])


def _route_rows(rwt_ref, rb_ref, xb):
    s = _sigmoid(_dot_nt(rwt_ref[...], xb))
    sb = s + rb_ref[...]
    rows = [sb[e:e + 1, :] for e in range(N_EXPERTS)]
    srow = [s[e:e + 1, :] for e in range(N_EXPERTS)]
    gscore = []
    for g in range(N_GROUPS):
        mem = rows[EXPERTS_PER_GROUP * g:EXPERTS_PER_GROUP * (g + 1)]
        best = None
        for a in range(EXPERTS_PER_GROUP):
            for b in range(a + 1, EXPERTS_PER_GROUP):
                pr = mem[a] + mem[b]
                best = pr if best is None else jnp.maximum(best, pr)
        gscore.append(best)
    gmax = functools.reduce(jnp.maximum, gscore)
    taken = None
    gsel = []
    for g in range(N_GROUPS):
        hit = gscore[g] == gmax
        if taken is None:
            gsel.append(hit)
            taken = hit
        else:
            gsel.append(jnp.logical_and(hit, jnp.logical_not(taken)))
            taken = jnp.logical_or(taken, hit)
    sel_w = []
    for e in range(N_EXPERTS):
        g = e // EXPERTS_PER_GROUP
        rank = None
        for o in range(EXPERTS_PER_GROUP * g, EXPERTS_PER_GROUP * (g + 1)):
            if o == e:
                continue
            ahead = (rows[o] >= rows[e]) if o < e else (rows[o] > rows[e])
            ahead = jnp.where(ahead, 1.0, 0.0)
            rank = ahead if rank is None else rank + ahead
        chosen = jnp.logical_and(gsel[g], rank < 1.5)
        sel_w.append(jnp.where(chosen, srow[e], 0.0))
    den = functools.reduce(jnp.add, sel_w)
    gate_rows = [w_ / den for w_ in sel_w]
    return gsel, gate_rows


def _swiglu_hidden(xb, wg, wu):
    hg = _dot(xb, wg)
    return hg * _sigmoid(hg) * _dot(xb, wu)


def _moe_kernel(x_ref, rwt_ref, rb_ref, tri_ref, wgs_ref, wus_ref, wds_ref, wg_ref, wu_ref, wd_ref,
                g2_ref, b2_ref, x2_out, xb_scr, col_scr, row_scr, flag_ref, *, TM, HT, CAP):
    j = pl.program_id(1)
    nh = TM // HT
    gate_lanes = EXPERTS_PER_GROUP
    RANK_LANE, GRP_LANE = gate_lanes, gate_lanes + 1

    @pl.when(j == 0)
    def _route():
        xb = x_ref[...].astype(BF16)
        xb_scr[...] = xb
        gsel, gate_rows = _route_rows(rwt_ref, rb_ref, xb)
        isg = [jnp.where(m, 1.0, 0.0) for m in gsel]
        grp = functools.reduce(jnp.add, [float(g) * isg[g] for g in range(N_GROUPS)])
        g4 = [functools.reduce(jnp.add, [isg[g] * gate_rows[EXPERTS_PER_GROUP * g + e] for g in range(N_GROUPS)])
              for e in range(EXPERTS_PER_GROUP)]
        ranks = []
        worst = None
        for hf in range(nh):
            hsl = slice(HT * hf, HT * (hf + 1))
            m8 = jnp.concatenate([isg[g][:, hsl] for g in range(N_GROUPS)]
                                 + [jnp.zeros((8 - N_GROUPS, HT), F32)], axis=0).astype(BF16)
            before = _dot(m8, tri_ref[...])
            ranks.append(functools.reduce(jnp.add, [isg[g][:, hsl] * before[g:g + 1, :] for g in range(N_GROUPS)]))
            for g in range(N_GROUPS):
                cnt = jnp.sum(isg[g][:, hsl])
                worst = cnt if worst is None else jnp.maximum(worst, cnt)
        rank = jnp.concatenate(ranks, axis=1)
        flag_ref[0] = (worst > float(CAP)).astype(jnp.int32)
        row_scr[...] = jnp.concatenate([rank, grp, jnp.zeros((6, TM), F32)], axis=0)
        col_scr[...] = jnp.concatenate(g4 + [rank, grp, jnp.zeros((128 - gate_lanes - 2, TM), F32)], axis=0).T
        x2_out[...] = _dot(_swiglu_hidden(xb, wgs_ref[...], wus_ref[...]).astype(BF16), wds_ref[...])

    gf = j.astype(F32)
    overflow = flag_ref[0] != 0

    def gated(h, gcols):
        parts = [h[:, D_EXPERT * e:D_EXPERT * (e + 1)] * gcols[:, e:e + 1] for e in range(EXPERTS_PER_GROUP)]
        return jnp.concatenate(parts, axis=1).astype(BF16)

    @pl.when(jnp.logical_not(overflow))
    def _compact():
        r_iota = lax.broadcasted_iota(jnp.int32, (CAP, HT), 0).astype(F32)
        c_iota = lax.broadcasted_iota(jnp.int32, (HT, CAP), 1).astype(F32)
        xcs, gcs = [], []
        for hf in range(nh):
            hsl = slice(HT * hf, HT * (hf + 1))
            pick = jnp.logical_and(row_scr[1:2, hsl] == gf, row_scr[0:1, hsl] == r_iota)
            P = jnp.where(pick, 1.0, 0.0).astype(BF16)
            xcs.append(_dot(P, xb_scr[hsl, :]).astype(BF16))
            cols = col_scr[hsl, :]
            c_hi = cols.astype(BF16)
            c_lo = (cols - c_hi.astype(F32)).astype(BF16)
            gcs.append(_dot(P, c_hi) + _dot(P, c_lo))
        xc = jnp.concatenate(xcs, axis=0)
        h = gated(_swiglu_hidden(xc, wg_ref[...], wu_ref[...]), jnp.concatenate(gcs, axis=0))
        y = _dot(h, wd_ref[...]).astype(BF16)
        for hf in range(nh):
            hsl = slice(HT * hf, HT * (hf + 1))
            cols = col_scr[hsl, :]
            pick = jnp.logical_and(cols[:, GRP_LANE:GRP_LANE + 1] == gf, cols[:, RANK_LANE:RANK_LANE + 1] == c_iota)
            Pt = jnp.where(pick, 1.0, 0.0).astype(BF16)
            x2_out[hsl, :] += _dot(Pt, y[CAP * hf:CAP * (hf + 1), :])

    @pl.when(overflow)
    def _uncompacted():
        RC = min(256, TM)

        def rows(c, carry):
            rs = pl.ds(pl.multiple_of(c * RC, RC), RC)
            cols = col_scr[rs, :]
            gcols = jnp.where(cols[:, GRP_LANE:GRP_LANE + 1] == gf, cols, 0.0)
            h = gated(_swiglu_hidden(xb_scr[rs, :], wg_ref[...], wu_ref[...]), gcols)
            x2_out[rs, :] += _dot(h, wd_ref[...])
            return carry

        lax.fori_loop(0, TM // RC, rows, 0)

    @pl.when(j == pl.num_programs(1) - 1)
    def _finish():
        x2_out[...] = _layernorm(DN_ALPHA * x_ref[...] + x2_out[...], g2_ref[...], b2_ref[...])


MOE_TM = 1024
MOE_CAP = 160


def _moe_call(x2d, ew):
    N = x2d.shape[0]
    TM = min(MOE_TM, N)
    HT = TM // 2
    CAP = min(MOE_CAP, HT)
    GW = EXPERTS_PER_GROUP * D_EXPERT
    row = pl.BlockSpec((TM, D_MODEL), lambda i, j: (i, 0))
    return pl.pallas_call(
        functools.partial(_moe_kernel, TM=TM, HT=HT, CAP=CAP),
        grid=(N // TM, N_GROUPS),
        in_specs=[row, _const_spec((N_EXPERTS, D_MODEL)), _const_spec((N_EXPERTS, 1)), _const_spec((HT, HT)),
                  _const_spec((D_MODEL, D_SHARED)), _const_spec((D_MODEL, D_SHARED)), _const_spec((D_SHARED, D_MODEL)),
                  pl.BlockSpec((D_MODEL, GW), lambda i, j: (0, j)), pl.BlockSpec((D_MODEL, GW), lambda i, j: (0, j)),
                  pl.BlockSpec((GW, D_MODEL), lambda i, j: (j, 0)),
                  _const_spec((1, D_MODEL)), _const_spec((1, D_MODEL))],
        out_specs=row,
        out_shape=jax.ShapeDtypeStruct((N, D_MODEL), F32),
        scratch_shapes=[pltpu.VMEM((TM, D_MODEL), BF16), pltpu.VMEM((TM, 128), F32), pltpu.VMEM((8, TM), F32),
                        pltpu.SMEM((1,), jnp.int32)],
        compiler_params=pltpu.CompilerParams(dimension_semantics=("parallel", "arbitrary"),
                                             vmem_limit_bytes=VMEM_LIMIT),
        name="moe",
    )(x2d, ew["rwt"], ew["rb"], jnp.triu(jnp.ones((HT, HT), BF16), 1), ew["wgs"], ew["wus"], ew["wds"],
      ew["wg"], ew["wu"], ew["wd"], ew["g2"], ew["b2"])


def _rot_cols(w):
    half = w.shape[-1] // 2
    return jnp.concatenate([-w[..., half:], w[..., :half]], axis=-1)


def _prep_layer(l, w_in, b_in, q_norm_g, kv_norm_g, w_uq, w_ukv, gmlp_ln_g, gmlp_ln_b, gmlp_ws, gmlp_bs,
                w_branch, w_out, ln1_g, ln1_b, moe_w_gate, moe_w_up, moe_w_down, shared_w_gate,
                shared_w_up, shared_w_down, ln2_g, ln2_b):
    wi = w_in[l]
    bi = b_in[l][None, :]

    def proj_cols(m):
        rows = m.shape[0]
        zero = lambda n: jnp.zeros((rows, n), m.dtype)
        kr = m[:, O_KR:O_MQ]
        kr128 = jnp.concatenate([zero(KR_LANE), kr, zero(HEAD_PAD - KR_LANE - ROPE_DIM)], axis=1)
        krr128 = jnp.concatenate([zero(KR_LANE), _rot_cols(kr), zero(HEAD_PAD - KR_LANE - ROPE_DIM)], axis=1)
        gates = jnp.concatenate([m[:, O_MI:O_MO], zero(HEAD_PAD - 2 * MLSTM_HEADS)], axis=1)
        return jnp.concatenate([m[:, O_CQ:O_KR], kr128, krr128, m[:, O_MQ:O_MI], gates, m[:, O_MO:O_GT]], axis=1)

    uq = w_uq[l].reshape(Q_RANK, MLA_HEADS, NOPE_DIM + ROPE_DIM)
    zq = lambda n: jnp.zeros((Q_RANK, MLA_HEADS, n), F32)
    pad = HEAD_PAD - NOPE_DIM - ROPE_DIM
    wq_a = jnp.concatenate([uq, zq(pad)], axis=-1).reshape(Q_RANK, QK_PAD)
    wq_b = jnp.concatenate([zq(NOPE_DIM), _rot_cols(uq[..., NOPE_DIM:]), zq(pad)], axis=-1).reshape(Q_RANK, QK_PAD)
    ukv = w_ukv[l].reshape(KV_RANK, MLA_HEADS, NOPE_DIM + V_DIM)
    wk = jnp.concatenate([ukv[..., :NOPE_DIM], jnp.zeros((KV_RANK, MLA_HEADS, HEAD_PAD - NOPE_DIM), F32)],
                         axis=-1).reshape(KV_RANK, QK_PAD)
    wv = ukv[..., NOPE_DIM:].reshape(KV_RANK, V_ALL)
    pw = dict(
        w1=proj_cols(wi).astype(BF16), b1=proj_cols(bi),
        qg=q_norm_g[l][None, :], kvg=kv_norm_g[l][None, :],
        wq=jnp.concatenate([wq_a, wq_b], axis=1).astype(BF16),
        wkv=jnp.concatenate([wk, wv], axis=1).astype(BF16),
        lng=gmlp_ln_g[l][None, :], lnb=gmlp_ln_b[l][None, :],
    )
    mw = dict(
        wgt=wi[:, O_GT:].astype(BF16), bgt=bi[:, O_GT:],
        ws_full=gmlp_ws[l], bs_full=gmlp_bs[l],
        wb=w_branch[l].astype(BF16), wo=w_out[l].astype(BF16),
        g1=ln1_g[l][None, :], b1=ln1_b[l][None, :],
    )
    cat_in = lambda we: jnp.transpose(we[l].astype(BF16), (1, 0, 2)).reshape(D_MODEL, N_EXPERTS * D_EXPERT)
    ew = dict(
        wg=cat_in(moe_w_gate), wu=cat_in(moe_w_up),
        wd=moe_w_down[l].reshape(N_EXPERTS * D_EXPERT, D_MODEL).astype(BF16),
        wgs=shared_w_gate[l].astype(BF16), wus=shared_w_up[l].astype(BF16), wds=shared_w_down[l].astype(BF16),
        g2=ln2_g[l][None, :], b2=ln2_b[l][None, :],
    )
    return pw, mw, ew


def _rope_tables(T, past, rows):
    half = ROPE_DIM // 2
    pos = (past + jnp.arange(T)).astype(F32)
    inv = ROPE_THETA ** (-jnp.arange(half, dtype=F32) / half)
    ang = pos[:, None] * inv[None, :]
    cos = jnp.cos(ang)
    sin = jnp.sin(ang)
    c2 = jnp.concatenate([cos, cos], axis=1)
    s2 = jnp.concatenate([sin, sin], axis=1)
    z = lambda n: jnp.zeros((T, n), F32)
    tail = HEAD_PAD - NOPE_DIM - ROPE_DIM
    tabs = dict(
        cq=jnp.concatenate([jnp.ones((T, NOPE_DIM), F32), c2, z(tail)], axis=1) * Q_SCALE,
        sq=jnp.concatenate([z(NOPE_DIM), s2, z(tail)], axis=1) * Q_SCALE,
        ck=jnp.concatenate([z(KR_LANE), c2, z(tail)], axis=1),
        sk=jnp.concatenate([z(KR_LANE), s2, z(tail)], axis=1),
    )
    if rows > T:
        tabs = {k: jnp.tile(v, (rows // T, 1)) for k, v in tabs.items()}
    return tabs


def _trunk(x, hist, layers, rwt, rb):
    B, T, _ = x.shape
    N = B * T
    past = 0 if hist is None else hist[0].shape[2]
    tabs = _rope_tables(T, past, min(512, N))
    L = GMLP_CHUNK if T % GMLP_CHUNK == 0 else T
    x2d = x.reshape(N, D_MODEL)
    ckvs, krs, Cs, ns, ms, gvs = [], [], [], [], [], []
    for l, (pw, mw, ew) in enumerate(layers):
        outs = _proj_call(x2d, pw, tabs, T, hist is not None)
        q, k, v, ckvn, kr, mq, mk, mv, gates, so, gu, gv = outs[:12]
        seq = lambda a: a.reshape(B, T, a.shape[-1])
        if hist is None:
            a_out = _attn_call(seq(q), seq(k), seq(v))
            s0 = jnp.zeros((B, MLSTM_HEADS, MLSTM_DH, 2 * MLSTM_DH), F32)
            m0 = jnp.zeros((B, MLSTM_HEADS, 8, 128), F32)
        else:
            ckv_past, kr_past, c0, n0, m0_in = hist
            P = ckv_past.shape[2]
            krp = jnp.pad(kr_past[l].reshape(B * P, ROPE_DIM),
                          ((0, 0), (KR_LANE, HEAD_PAD - KR_LANE - ROPE_DIM)))
            kp, vp = _pastkv_call(ckv_past[l].reshape(B * P, KV_RANK), krp, pw["wkv"])
            a_out = _attn_hist_call(seq(q), kp.reshape(B, P, QK_PAD), vp.reshape(B, P, V_ALL), seq(k), seq(v))
            s0 = jnp.concatenate(
                [jnp.swapaxes(c0[l], -1, -2),
                 jnp.broadcast_to(n0[l][:, :, :, None], (B, MLSTM_HEADS, MLSTM_DH, MLSTM_DH))], axis=3)
            m0 = jnp.broadcast_to(m0_in[l][:, :, None, None], (B, MLSTM_HEADS, 8, 128))
        b_out, sfin, mfin = _mlstm_call(seq(mq), seq(mk), seq(mv), seq(gates), seq(so), s0, m0)
        mwl = dict(mw)
        mwl["ws"] = mw["ws_full"][:, :L, :L]
        mwl["bsf"] = jnp.repeat(mw["bs_full"][:, :L].T, GMLP_DG, axis=1)
        x1 = _merge_call(x2d, a_out.reshape(N, V_ALL), b_out.reshape(N, MLSTM_WIDTH), gu, gv, mwl, L)
        x2d = _moe_call(x1, dict(ew, rwt=rwt, rb=rb))
        ckvs.append(ckvn.reshape(B, T, KV_RANK))
        krs.append(kr[:, KR_LANE:KR_LANE + ROPE_DIM].reshape(B, T, ROPE_DIM))
        Cs.append(jnp.swapaxes(sfin[:, :, :, :MLSTM_DH], -1, -2))
        ns.append(sfin[:, :, :, MLSTM_DH])
        ms.append(mfin[:, :, 0, 0])
        if hist is not None:
            gvs.append(outs[12].reshape(B, T, GMLP_WIDTH))
    res = [x2d.reshape(B, T, D_MODEL), jnp.stack(ckvs), jnp.stack(krs), jnp.stack(Cs), jnp.stack(ns), jnp.stack(ms)]
    if hist is not None:
        res.append(jnp.stack(gvs))
    return res


def kernel(x_prompt, x_sample, cache_mla_ckv, cache_mla_krope, state_mlstm_c, state_mlstm_n, state_mlstm_m,
           w_in, b_in, q_norm_g, kv_norm_g, w_uq, w_ukv, gmlp_ln_g, gmlp_ln_b, gmlp_ws, gmlp_bs, w_branch,
           w_out, ln1_g, ln1_b, router_w, router_b, moe_w_gate, moe_w_up, moe_w_down, shared_w_gate,
           shared_w_up, shared_w_down, ln2_g, ln2_b):
    depth = w_in.shape[0]
    layers = [_prep_layer(l, w_in, b_in, q_norm_g, kv_norm_g, w_uq, w_ukv, gmlp_ln_g, gmlp_ln_b, gmlp_ws,
                          gmlp_bs, w_branch, w_out, ln1_g, ln1_b, moe_w_gate, moe_w_up, moe_w_down,
                          shared_w_gate, shared_w_up, shared_w_down, ln2_g, ln2_b) for l in range(depth)]
    rwt = router_w.T.astype(BF16)
    rb = router_b[:, None]
    yp, p_ckv, p_kr, p_c, p_n, p_m = _trunk(x_prompt, None, layers, rwt, rb)
    ys, s_ckv, s_kr, s_c, s_n, s_m, s_gv = _trunk(
        x_sample, (cache_mla_ckv, cache_mla_krope, state_mlstm_c, state_mlstm_n, state_mlstm_m), layers, rwt, rb)
    return (yp, ys, p_ckv, p_kr, p_c, p_n, p_m, s_ckv, s_kr, s_c, s_n, s_m, s_gv)
```

```python
import functools

import jax
import jax.numpy as jnp
import numpy as np
from jax import lax
from jax.experimental import pallas as pl
from jax.experimental.pallas import tpu as pltpu

F32 = jnp.float32
BF16 = jnp.bfloat16

D_MODEL = 1024
CHUNK = 64
MLA_HEADS = 8
Q_RANK = 256
KV_RANK = 256
NOPE_DIM = 64
ROPE_DIM = 32
V_DIM = 64
ROPE_THETA = 10000.0
ATTN_SCALE = (NOPE_DIM + ROPE_DIM) ** -0.5
Q_SCALE = ATTN_SCALE * float(np.log2(np.e))
MLSTM_HEADS = 4
MLSTM_DH = 128
MLSTM_WIDTH = MLSTM_HEADS * MLSTM_DH
GMLP_GROUPS = 4
GMLP_DG = 128
GMLP_WIDTH = GMLP_GROUPS * GMLP_DG
GMLP_CHUNK = 128
N_BRANCH = 3
N_EXPERTS = 16
N_GROUPS = 4
EXPERTS_PER_GROUP = N_EXPERTS // N_GROUPS
D_EXPERT = 256
D_SHARED = 256
DEPTH = 4
DN_ALPHA = (2 * DEPTH) ** 0.25
EPS = 1e-5

HEAD_PAD = 128
QK_PAD = MLA_HEADS * HEAD_PAD
V_ALL = MLA_HEADS * V_DIM
P_CQ, P_CKV, P_KR, P_KRR, P_MQ, P_MK, P_MV, P_G, P_MO, P_GU, P_GV = (
    0, 256, 512, 640, 768, 1280, 1792, 2304, 2432, 2944, 3456)
D_PROJ = 3968
O_CQ, O_CKV, O_KR, O_MQ, O_MK, O_MV, O_MI, O_MF, O_MO, O_GU, O_GV, O_GT = (
    0, 256, 512, 544, 1056, 1568, 2080, 2084, 2088, 2600, 3112, 3624)
KR_LANE = NOPE_DIM
VMEM_LIMIT = 56 * 1024 * 1024
NEG_INF = float("-inf")


def _dot(a, b):
    return jnp.dot(a, b, preferred_element_type=F32)


def _dot_nt(a, b):
    return lax.dot_general(a, b, (((1,), (1,)), ((), ())), preferred_element_type=F32)


def _const_spec(shape):
    nd = len(shape)
    return pl.BlockSpec(shape, lambda *_: (0,) * nd, pipeline_mode=pl.Buffered(1))


def _layernorm(x, g, b):
    mu = jnp.mean(x, axis=-1, keepdims=True)
    xc = x - mu
    var = jnp.mean(xc * xc, axis=-1, keepdims=True)
    return xc * lax.rsqrt(var + EPS) * g + b


def _rmsnorm(x, g):
    return x * lax.rsqrt(jnp.mean(x * x, axis=-1, keepdims=True) + EPS) * g


def _gelu(x):
    return 0.5 * x * (1.0 + jnp.tanh(np.sqrt(2.0 / np.pi).astype(np.float32) * (x + 0.044715 * (x * x * x))))


def _sigmoid(x):
    return 1.0 / (1.0 + jnp.exp(-x))


def _log_sigmoid(x):
    return jnp.minimum(x, 0.0) - jnp.log(1.0 + jnp.exp(-jnp.abs(x)))


def _proj_kernel(x_ref, w1_ref, b1_ref, qg_ref, kvg_ref, wq_ref, wkv_ref, cq_ref, sq_ref, ck_ref, sk_ref,
                 lng_ref, lnb_ref,
                 q_out, k_out, v_out, ckv_out, kr_out, mq_out, mk_out, mv_out, g_out, so_out, gu_out,
                 gv_out, *rest, key_block):
    xb = x_ref[...].astype(BF16)

    def z(lo, hi):
        return _dot(xb, w1_ref[:, lo:hi]) + b1_ref[:, lo:hi]

    zc = z(P_CQ, P_MQ)
    cqn = _rmsnorm(zc[:, P_CQ:P_CKV], qg_ref[...])
    ckvn = _rmsnorm(zc[:, P_CKV:P_KR], kvg_ref[...])
    kr = zc[:, P_KR:P_KRR] * ck_ref[...] + zc[:, P_KRR:P_MQ] * sk_ref[...]
    ckv_out[...] = ckvn
    kr_out[...] = kr
    qq = _dot(cqn.astype(BF16), wq_ref[...])
    cos8 = jnp.concatenate([cq_ref[...]] * MLA_HEADS, axis=1)
    sin8 = jnp.concatenate([sq_ref[...]] * MLA_HEADS, axis=1)
    q_out[...] = (qq[:, :QK_PAD] * cos8 + qq[:, QK_PAD:] * sin8).astype(BF16)
    kk = _dot(ckvn.astype(BF16), wkv_ref[...])
    kr8 = jnp.concatenate([kr] * MLA_HEADS, axis=1)
    kval = kk[:, :QK_PAD] + kr8
    if key_block is None:
        k_out[...] = kval.astype(BF16)
    else:
        for c in range(kval.shape[0] // key_block):
            k_out[c] = kval[key_block * c:key_block * (c + 1), :].T.astype(BF16)
    v_out[...] = kk[:, QK_PAD:].astype(BF16)

    zm = z(P_MQ, P_MO)
    mq_out[...] = zm[:, 0:512].astype(BF16)
    mk_out[...] = (zm[:, 512:1024] * (MLSTM_DH ** -0.5)).astype(BF16)
    mv_out[...] = zm[:, 1024:1536].astype(BF16)
    zg = zm[:, 1536:1664]
    lane = lax.broadcasted_iota(jnp.int32, zg.shape, 1)
    g_out[...] = jnp.where(lane < MLSTM_HEADS, zg, _log_sigmoid(zg))
    so_out[...] = _sigmoid(z(P_MO, P_GU)).astype(BF16)

    zu = z(P_GU, D_PROJ)
    gu_out[...] = _gelu(zu[:, :GMLP_WIDTH]).astype(BF16)
    vrows = _layernorm(_gelu(zu[:, GMLP_WIDTH:]), lng_ref[...], lnb_ref[...])
    gv_out[...] = vrows.astype(BF16)
    if rest:
        rest[0][...] = vrows


def _proj_call(x2d, pw, tabs, T, want_vrows, key_block):
    N = x2d.shape[0]
    TM = min(512, N)
    nt = max(1, T // TM)
    grid = (N // TM,)
    row = lambda w: pl.BlockSpec((TM, w), lambda i: (i, 0))
    tab = pl.BlockSpec((TM, HEAD_PAD), lambda i: (i % nt, 0))
    in_specs = [
        row(D_MODEL),
        _const_spec((D_MODEL, D_PROJ)), _const_spec((1, D_PROJ)),
        _const_spec((1, Q_RANK)), _const_spec((1, KV_RANK)),
        _const_spec((Q_RANK, 2 * QK_PAD)), _const_spec((KV_RANK, QK_PAD + V_ALL)),
        tab, tab, tab, tab,
        _const_spec((1, GMLP_WIDTH)), _const_spec((1, GMLP_WIDTH)),
    ]
    out_shapes = [
        ((N, QK_PAD), BF16), ((N, QK_PAD), BF16), ((N, V_ALL), BF16), ((N, KV_RANK), F32),
        ((N, HEAD_PAD), F32), ((N, MLSTM_WIDTH), BF16), ((N, MLSTM_WIDTH), BF16), ((N, MLSTM_WIDTH), BF16),
        ((N, HEAD_PAD), F32), ((N, MLSTM_WIDTH), BF16), ((N, GMLP_WIDTH), BF16), ((N, GMLP_WIDTH), BF16),
    ]
    if want_vrows:
        out_shapes.append(((N, GMLP_WIDTH), F32))
    out_specs = [row(s[1]) for s, _ in out_shapes]
    if key_block is not None:
        out_shapes[1] = ((N // key_block, QK_PAD, key_block), BF16)
        out_specs[1] = pl.BlockSpec((TM // key_block, QK_PAD, key_block), lambda i: (i, 0, 0))
    return pl.pallas_call(
        functools.partial(_proj_kernel, key_block=key_block),
        grid=grid,
        in_specs=in_specs,
        out_specs=out_specs,
        out_shape=[jax.ShapeDtypeStruct(s, d) for s, d in out_shapes],
        compiler_params=pltpu.CompilerParams(dimension_semantics=("parallel",), vmem_limit_bytes=VMEM_LIMIT),
        name="proj",
    )(x2d, pw["w1"], pw["b1"], pw["qg"], pw["kvg"], pw["wq"], pw["wkv"],
      tabs["cq"], tabs["sq"], tabs["ck"], tabs["sk"], pw["lng"], pw["lnb"])


def _attn_kernel(q_ref, kt_ref, v_ref, o_ref, mx_scr, acc_scr, s_scr, *, TQ):
    i = pl.program_id(1)
    ones_blk = jnp.ones((TQ, HEAD_PAD), BF16)
    rc = lax.broadcasted_iota(jnp.int32, (TQ, TQ), 0) // CHUNK
    cc = lax.broadcasted_iota(jnp.int32, (TQ, TQ), 1) // CHUNK
    visible = cc <= rc
    lane = lax.broadcasted_iota(jnp.int32, (TQ, HEAD_PAD), 1)
    nlane = TQ // 128

    def fold(a, op):
        r = a[:, 0:128]
        for t in range(1, nlane):
            r = op(r, a[:, 128 * t:128 * (t + 1)])
        return r

    hsl = lambda h: slice(HEAD_PAD * h, HEAD_PAD * (h + 1))
    heads = range(MLA_HEADS)

    def sweep_max(j, masked, first):
        ss = [_dot(q_ref[0, :, hsl(h)], kt_ref[0, j, hsl(h), :]) for h in heads]
        if masked:
            ss = [jnp.where(visible, s, NEG_INF) for s in ss]
        for h in heads:
            s_scr[h, j] = ss[h]
        for h in heads:
            mx = fold(ss[h], jnp.maximum)
            mx_scr[h] = mx if first else jnp.maximum(mx_scr[h], mx)

    def sweep_pv(j, first):
        r0 = pl.multiple_of(j * TQ, TQ)
        ps = [jnp.exp2(s_scr[h, j] - jnp.concatenate([mx_scr[h]] * nlane, axis=1)).astype(BF16) for h in heads]
        pvs = []
        for p in range(MLA_HEADS // 2):
            vs = slice(HEAD_PAD * p, HEAD_PAD * (p + 1))
            vext = jnp.concatenate([v_ref[0, pl.ds(r0, TQ), vs], ones_blk], axis=1)
            pv2 = _dot(jnp.concatenate([ps[2 * p], ps[2 * p + 1]], axis=0), vext)
            pvs += [pv2[:TQ], pv2[TQ:]]
        for h in heads:
            acc_scr[h] = pvs[h] if first else acc_scr[h] + pvs[h]

    def body_max(j, c):
        sweep_max(j, False, False)
        return c

    def body_pv(j, c):
        sweep_pv(j, False)
        return c

    sweep_max(i, True, True)
    lax.fori_loop(0, i, body_max, 0)
    for h in heads:
        mx_scr[h] = jnp.broadcast_to(jnp.max(mx_scr[h], axis=-1, keepdims=True), (TQ, 128))
    sweep_pv(i, True)
    lax.fori_loop(0, i, body_pv, 0)
    for p in range(MLA_HEADS // 2):
        a0 = acc_scr[2 * p]
        a1 = acc_scr[2 * p + 1]
        o0 = a0[:, :HEAD_PAD] / a0[:, HEAD_PAD:]
        o1 = a1[:, :HEAD_PAD] / a1[:, HEAD_PAD:]
        o_ref[0, :, HEAD_PAD * p:HEAD_PAD * (p + 1)] = jnp.where(lane < V_DIM, o0, o1).astype(BF16)


def _attn_block(T):
    return min(256, T)


def _attn_call(q, kt, v):
    B, T, _ = q.shape
    TQ = _attn_block(T)
    nq = T // TQ
    return pl.pallas_call(
        functools.partial(_attn_kernel, TQ=TQ),
        grid=(B, nq),
        in_specs=[pl.BlockSpec((1, TQ, QK_PAD), lambda b, i: (b, i, 0)),
                  pl.BlockSpec((1, nq, QK_PAD, TQ), lambda b, i: (b, 0, 0, 0)),
                  pl.BlockSpec((1, T, V_ALL), lambda b, i: (b, 0, 0))],
        out_specs=pl.BlockSpec((1, TQ, V_ALL), lambda b, i: (b, i, 0)),
        out_shape=jax.ShapeDtypeStruct((B, T, V_ALL), BF16),
        scratch_shapes=[pltpu.VMEM((MLA_HEADS, TQ, 128), F32), pltpu.VMEM((MLA_HEADS, TQ, 2 * HEAD_PAD), F32),
                        pltpu.VMEM((MLA_HEADS, nq, TQ, TQ), F32)],
        compiler_params=pltpu.CompilerParams(dimension_semantics=("parallel", "arbitrary"),
                                             vmem_limit_bytes=VMEM_LIMIT),
        name="attn",
    )(q, kt, v)


def _attn_hist_kernel(q_ref, cp_ref, krp_ref, wkv_ref, kn_ref, vn_ref, o_ref, kp_ref, vp_ref):
    T = q_ref.shape[1]
    lane = lax.broadcasted_iota(jnp.int32, (T, HEAD_PAD), 1)
    kk = _dot(cp_ref[0].astype(BF16), wkv_ref[...])
    kp_ref[0] = (kk[:, :QK_PAD] + jnp.concatenate([krp_ref[0]] * MLA_HEADS, axis=1)).astype(BF16)
    vp_ref[0] = kk[:, QK_PAD:].astype(BF16)
    for p in range(MLA_HEADS // 2):
        pair = []
        vs = slice(HEAD_PAD * p, HEAD_PAD * (p + 1))
        for hh in range(2):
            h = 2 * p + hh
            hs = slice(HEAD_PAD * h, HEAD_PAD * (h + 1))
            qh = q_ref[0, :, hs]
            s1 = _dot_nt(qh, kp_ref[0, :, hs])
            s2 = _dot_nt(qh, kn_ref[0, :, hs])
            m = jnp.maximum(jnp.max(s1, axis=-1, keepdims=True), jnp.max(s2, axis=-1, keepdims=True))
            p1 = jnp.exp2(s1 - m)
            p2 = jnp.exp2(s2 - m)
            l = jnp.sum(p1, axis=-1, keepdims=True) + jnp.sum(p2, axis=-1, keepdims=True)
            acc = _dot(p1.astype(BF16), vp_ref[0, :, vs]) + _dot(p2.astype(BF16), vn_ref[0, :, vs])
            pair.append(acc / l)
        o_ref[0, :, vs] = jnp.where(lane < V_DIM, pair[0], pair[1]).astype(BF16)


def _attn_hist_call(q, ckv_past, krp, wkv, kn, vn):
    B, T, _ = q.shape
    P = ckv_past.shape[1]
    blk = lambda t, w: pl.BlockSpec((1, t, w), lambda b: (b, 0, 0))
    return pl.pallas_call(
        _attn_hist_kernel,
        grid=(B,),
        in_specs=[blk(T, QK_PAD), blk(P, KV_RANK), blk(P, HEAD_PAD), _const_spec((KV_RANK, QK_PAD + V_ALL)),
                  blk(T, QK_PAD), blk(T, V_ALL)],
        out_specs=blk(T, V_ALL),
        out_shape=jax.ShapeDtypeStruct((B, T, V_ALL), BF16),
        scratch_shapes=[pltpu.VMEM((1, P, QK_PAD), BF16), pltpu.VMEM((1, P, V_ALL), BF16)],
        compiler_params=pltpu.CompilerParams(dimension_semantics=("parallel",), vmem_limit_bytes=VMEM_LIMIT),
        name="attn_hist",
    )(q, ckv_past, krp, wkv, kn, vn)


MLSTM_BB = 4


def _mlstm_kernel(q_ref, k_ref, v_ref, g_ref, so_ref, s0_ref, m0_ref, h_out, sfin_out, mfin_out,
                  s_scr, m_scr, *, TL, BB):
    t = pl.program_id(1)
    L = CHUNK
    DH = MLSTM_DH

    @pl.when(t == 0)
    def _():
        s_scr[...] = s0_ref[...]
        m_scr[...] = m0_ref[...]

    r_i = lax.broadcasted_iota(jnp.int32, (L, L), 0)
    c_i = lax.broadcasted_iota(jnp.int32, (L, L), 1)
    causal = c_i <= r_i
    tri = jnp.where(causal, 1.0, 0.0).astype(BF16)
    lane = lax.broadcasted_iota(jnp.int32, (L, 128), 1)
    ones_blk = jnp.ones((L, DH), BF16)

    def rep(col):
        return jnp.broadcast_to(col, (L, 128))

    def chunk(c, carry):
        rows = pl.ds(pl.multiple_of(c * L, L), L)
        pairs = [(bb, h) for bb in range(BB) for h in range(MLSTM_HEADS)]
        hsl = lambda h: slice(DH * h, DH * (h + 1))
        Gs, cums, VTs = [], [], []
        for bb in range(BB):
            G = g_ref[bb, rows, :]
            g_hi = G.astype(BF16)
            g_r1 = G - g_hi.astype(F32)
            g_mid = g_r1.astype(BF16)
            g_lo = (g_r1 - g_mid.astype(F32)).astype(BF16)
            cum = _dot(tri, g_hi) + _dot(tri, g_mid) + _dot(tri, g_lo)
            Gs.append(G)
            cums.append(cum)
            VTs.append(jnp.where(lane < MLSTM_HEADS, G, cum).T)
        qk_raw = {p: _dot_nt(q_ref[p[0], rows, hsl(p[1])], k_ref[p[0], rows, hsl(p[1])]) for p in pairs}
        sq = {p: _dot(q_ref[p[0], rows, hsl(p[1])], s_scr[p[0], p[1]].astype(BF16)) for p in pairs}
        gate = {}
        for bb, h in pairs:
            b_t = rep(jnp.sum(jnp.where(lane == MLSTM_HEADS + h, cums[bb], 0.0), axis=1, keepdims=True))
            ig_t = rep(jnp.sum(jnp.where(lane == h, Gs[bb], 0.0), axis=1, keepdims=True))
            brow = VTs[bb][MLSTM_HEADS + h:MLSTM_HEADS + h + 1, :]
            igrow = VTs[bb][h:h + 1, :]
            d = jnp.where(causal, b_t[:, :L] - brow + igrow, NEG_INF)
            gate[bb, h] = (b_t, ig_t, d, rep(jnp.max(d, axis=1, keepdims=True)))
        stab = {}
        for bb, h in pairs:
            b_t, ig_t, d, dmax = gate[bb, h]
            m_prev = m_scr[bb, h, 0:1, :]
            g_t = b_t + m_prev
            mt = jnp.maximum(g_t, dmax)
            stab[bb, h] = (m_prev, mt, jnp.exp(g_t - mt), jnp.exp(d - mt[:, :L]))
        for bb, h in pairs:
            m_prev, mt, inter, w = stab[bb, h]
            vext = jnp.concatenate([v_ref[bb, rows, hsl(h)], ones_blk], axis=1)
            intra = _dot((qk_raw[bb, h] * w).astype(BF16), vext)
            num = inter * sq[bb, h][:, :DH] + intra[:, :DH]
            den = jnp.maximum(jnp.abs(inter * sq[bb, h][:, DH:] + intra[:, DH:]), jnp.exp(-mt))
            so = so_ref[bb, rows, hsl(h)].astype(F32)
            h_out[bb, rows, hsl(h)] = (so * (num / den)).astype(BF16)
        for bb, h in pairs:
            b_t, ig_t, _, _ = gate[bb, h]
            m_prev, mt, _, _ = stab[bb, h]
            mL = mt[L - 1:L, :]
            bL = b_t[L - 1:L, :]
            ws_t = jnp.exp(bL - b_t + ig_t - mL)
            decay = jnp.exp(bL + m_prev - mL)
            wv = jnp.concatenate([ws_t * v_ref[bb, rows, hsl(h)].astype(F32), ws_t], axis=1).astype(BF16)
            kT = k_ref[bb, rows, hsl(h)].astype(F32).T.astype(BF16)
            s_scr[bb, h] = jnp.concatenate([decay, decay], axis=1) * s_scr[bb, h] + _dot(kT, wv)
            m_scr[bb, h] = jnp.broadcast_to(mL, (8, 128))
        return carry

    lax.fori_loop(0, TL // L, chunk, 0)

    @pl.when(t == pl.num_programs(1) - 1)
    def _():
        sfin_out[...] = s_scr[...]
        mfin_out[...] = m_scr[...]


def _mlstm_call(mq, mk, mv, gates, so, s0, m0):
    B, T, _ = mq.shape
    TL = min(512, T)
    BB = int(np.gcd(MLSTM_BB, B))
    seq = lambda w: pl.BlockSpec((BB, TL, w), lambda b, t: (b, t, 0))
    st = pl.BlockSpec((BB, MLSTM_HEADS, MLSTM_DH, 2 * MLSTM_DH), lambda b, t: (b, 0, 0, 0))
    mst = pl.BlockSpec((BB, MLSTM_HEADS, 8, 128), lambda b, t: (b, 0, 0, 0))
    return pl.pallas_call(
        functools.partial(_mlstm_kernel, TL=TL, BB=BB),
        grid=(B // BB, T // TL),
        in_specs=[seq(MLSTM_WIDTH), seq(MLSTM_WIDTH), seq(MLSTM_WIDTH), seq(128), seq(MLSTM_WIDTH), st, mst],
        out_specs=[seq(MLSTM_WIDTH), st, mst],
        out_shape=[jax.ShapeDtypeStruct((B, T, MLSTM_WIDTH), BF16),
                   jax.ShapeDtypeStruct((B, MLSTM_HEADS, MLSTM_DH, 2 * MLSTM_DH), F32),
                   jax.ShapeDtypeStruct((B, MLSTM_HEADS, 8, 128), F32)],
        scratch_shapes=[pltpu.VMEM((BB, MLSTM_HEADS, MLSTM_DH, 2 * MLSTM_DH), F32),
                        pltpu.VMEM((BB, MLSTM_HEADS, 8, 128), F32)],
        compiler_params=pltpu.CompilerParams(dimension_semantics=("parallel", "arbitrary"),
                                             vmem_limit_bytes=VMEM_LIMIT),
        name="mlstm",
    )(mq, mk, mv, gates, so, s0, m0)


def _merge_kernel(x_ref, a_ref, b_ref, gu_ref, gv_ref, wgt_ref, bgt_ref, ws_ref, bsf_ref, wb_ref, wo_ref,
                  g1_ref, b1_ref, x1_out, c_scr, *, L, TM):
    x = x_ref[...]
    xb = x.astype(BF16)
    r_i = lax.broadcasted_iota(jnp.int32, (L, L), 0)
    c_i = lax.broadcasted_iota(jnp.int32, (L, L), 1)
    for g in range(GMLP_GROUPS):
        gs = slice(GMLP_DG * g, GMLP_DG * (g + 1))
        wsg = jnp.where(c_i <= r_i, ws_ref[g], 0.0).astype(BF16)
        for c in range(TM // L):
            rs = slice(L * c, L * (c + 1))
            sp = _dot(wsg, gv_ref[rs, gs]) + bsf_ref[:, gs]
            c_scr[rs, gs] = (gu_ref[rs, gs].astype(F32) * sp).astype(BF16)
    merged = None
    for kb, br in enumerate((a_ref, b_ref, c_scr)):
        cs = slice(D_MODEL * kb, D_MODEL * (kb + 1))
        gate = _sigmoid(_dot(xb, wgt_ref[:, cs]) + bgt_ref[:, cs])
        term = gate * _dot(br[...], wb_ref[kb])
        merged = term if merged is None else merged + term
    y = _dot(merged.astype(BF16), wo_ref[...])
    x1_out[...] = _layernorm(DN_ALPHA * x + y, g1_ref[...], b1_ref[...])


def _merge_call(x2d, a2d, b2d, gu, gv, mw, L):
    N = x2d.shape[0]
    TM = min(512, N)
    row = lambda w: pl.BlockSpec((TM, w), lambda i: (i, 0))
    return pl.pallas_call(
        functools.partial(_merge_kernel, L=L, TM=TM),
        grid=(N // TM,),
        in_specs=[row(D_MODEL), row(V_ALL), row(MLSTM_WIDTH), row(GMLP_WIDTH), row(GMLP_WIDTH),
                  _const_spec((D_MODEL, N_BRANCH * D_MODEL)), _const_spec((1, N_BRANCH * D_MODEL)),
                  _const_spec((GMLP_GROUPS, L, L)), _const_spec((L, GMLP_WIDTH)),
                  _const_spec((N_BRANCH, 512, D_MODEL)), _const_spec((D_MODEL, D_MODEL)),
                  _const_spec((1, D_MODEL)), _const_spec((1, D_MODEL))],
        out_specs=row(D_MODEL),
        out_shape=jax.ShapeDtypeStruct((N, D_MODEL), F32),
        scratch_shapes=[pltpu.VMEM((TM, GMLP_WIDTH), BF16)],
        compiler_params=pltpu.CompilerParams(dimension_semantics=("parallel",), vmem_limit_bytes=VMEM_LIMIT),
        name="merge",
    )(x2d, a2d, b2d, gu, gv, mw["wgt"], mw["bgt"], mw["ws"], mw["bsf"], mw["wb"], mw["wo"], mw["g1"], mw["b1"])


def _route_rows(rwt_ref, rb_ref, xb):
    s = _sigmoid(_dot_nt(rwt_ref[...], xb))
    sb = s + rb_ref[...]
    rows = [sb[e:e + 1, :] for e in range(N_EXPERTS)]
    srow = [s[e:e + 1, :] for e in range(N_EXPERTS)]
    gscore = []
    for g in range(N_GROUPS):
        mem = rows[EXPERTS_PER_GROUP * g:EXPERTS_PER_GROUP * (g + 1)]
        best = None
        for a in range(EXPERTS_PER_GROUP):
            for b in range(a + 1, EXPERTS_PER_GROUP):
                pr = mem[a] + mem[b]
                best = pr if best is None else jnp.maximum(best, pr)
        gscore.append(best)
    gmax = functools.reduce(jnp.maximum, gscore)
    taken = None
    gsel = []
    for g in range(N_GROUPS):
        hit = gscore[g] == gmax
        if taken is None:
            gsel.append(hit)
            taken = hit
        else:
            gsel.append(jnp.logical_and(hit, jnp.logical_not(taken)))
            taken = jnp.logical_or(taken, hit)
    sel_w = []
    for e in range(N_EXPERTS):
        g = e // EXPERTS_PER_GROUP
        rank = None
        for o in range(EXPERTS_PER_GROUP * g, EXPERTS_PER_GROUP * (g + 1)):
            if o == e:
                continue
            ahead = (rows[o] >= rows[e]) if o < e else (rows[o] > rows[e])
            ahead = jnp.where(ahead, 1.0, 0.0)
            rank = ahead if rank is None else rank + ahead
        chosen = jnp.logical_and(gsel[g], rank < 1.5)
        sel_w.append(jnp.where(chosen, srow[e], 0.0))
    den = functools.reduce(jnp.add, sel_w)
    gate_rows = [w_ / den for w_ in sel_w]
    return gsel, gate_rows


def _swiglu_hidden(xb, wg, wu):
    hg = _dot(xb, wg)
    return hg * _sigmoid(hg) * _dot(xb, wu)


def _moe_kernel(x_ref, rwt_ref, rb_ref, tri_ref, wgs_ref, wus_ref, wds_ref, wg_ref, wu_ref, wd_ref,
                g2_ref, b2_ref, x2_out, xb_scr, col_scr, row_scr, flag_ref, *, TM, HT, CAP):
    j = pl.program_id(1)
    nh = TM // HT
    gate_lanes = EXPERTS_PER_GROUP
    RANK_LANE, GRP_LANE = gate_lanes, gate_lanes + 1

    @pl.when(j == 0)
    def _route():
        xb = x_ref[...].astype(BF16)
        xb_scr[...] = xb
        gsel, gate_rows = _route_rows(rwt_ref, rb_ref, xb)
        isg = [jnp.where(m, 1.0, 0.0) for m in gsel]
        grp = functools.reduce(jnp.add, [float(g) * isg[g] for g in range(N_GROUPS)])
        g4 = [functools.reduce(jnp.add, [isg[g] * gate_rows[EXPERTS_PER_GROUP * g + e] for g in range(N_GROUPS)])
              for e in range(EXPERTS_PER_GROUP)]
        ranks = []
        worst = None
        for hf in range(nh):
            hsl = slice(HT * hf, HT * (hf + 1))
            m8 = jnp.concatenate([isg[g][:, hsl] for g in range(N_GROUPS)]
                                 + [jnp.zeros((8 - N_GROUPS, HT), F32)], axis=0).astype(BF16)
            before = _dot(m8, tri_ref[...])
            ranks.append(functools.reduce(jnp.add, [isg[g][:, hsl] * before[g:g + 1, :] for g in range(N_GROUPS)]))
            for g in range(N_GROUPS):
                cnt = jnp.sum(isg[g][:, hsl])
                worst = cnt if worst is None else jnp.maximum(worst, cnt)
        rank = jnp.concatenate(ranks, axis=1)
        flag_ref[0] = (worst > float(CAP)).astype(jnp.int32)
        row_scr[...] = jnp.concatenate([rank, grp, jnp.zeros((6, TM), F32)], axis=0)
        col_scr[...] = jnp.concatenate(g4 + [rank, grp, jnp.zeros((128 - gate_lanes - 2, TM), F32)], axis=0).T
        x2_out[...] = _dot(_swiglu_hidden(xb, wgs_ref[...], wus_ref[...]).astype(BF16), wds_ref[...])

    gf = j.astype(F32)
    overflow = flag_ref[0] != 0

    def gated(h, gcols):
        parts = [h[:, D_EXPERT * e:D_EXPERT * (e + 1)] * gcols[:, e:e + 1] for e in range(EXPERTS_PER_GROUP)]
        return jnp.concatenate(parts, axis=1).astype(BF16)

    @pl.when(jnp.logical_not(overflow))
    def _compact():
        r_iota = lax.broadcasted_iota(jnp.int32, (CAP, HT), 0).astype(F32)
        c_iota = lax.broadcasted_iota(jnp.int32, (HT, CAP), 1).astype(F32)
        xcs, gcs = [], []
        for hf in range(nh):
            hsl = slice(HT * hf, HT * (hf + 1))
            pick = jnp.logical_and(row_scr[1:2, hsl] == gf, row_scr[0:1, hsl] == r_iota)
            P = jnp.where(pick, 1.0, 0.0).astype(BF16)
            xcs.append(_dot(P, xb_scr[hsl, :]).astype(BF16))
            cols = col_scr[hsl, :]
            c_hi = cols.astype(BF16)
            c_lo = (cols - c_hi.astype(F32)).astype(BF16)
            gcs.append(_dot(P, c_hi) + _dot(P, c_lo))
        xc = jnp.concatenate(xcs, axis=0)
        h = gated(_swiglu_hidden(xc, wg_ref[...], wu_ref[...]), jnp.concatenate(gcs, axis=0))
        y = _dot(h, wd_ref[...]).astype(BF16)
        for hf in range(nh):
            hsl = slice(HT * hf, HT * (hf + 1))
            cols = col_scr[hsl, :]
            pick = jnp.logical_and(cols[:, GRP_LANE:GRP_LANE + 1] == gf, cols[:, RANK_LANE:RANK_LANE + 1] == c_iota)
            Pt = jnp.where(pick, 1.0, 0.0).astype(BF16)
            x2_out[hsl, :] += _dot(Pt, y[CAP * hf:CAP * (hf + 1), :])

    @pl.when(overflow)
    def _uncompacted():
        RC = min(256, TM)

        def rows(c, carry):
            rs = pl.ds(pl.multiple_of(c * RC, RC), RC)
            cols = col_scr[rs, :]
            gcols = jnp.where(cols[:, GRP_LANE:GRP_LANE + 1] == gf, cols, 0.0)
            h = gated(_swiglu_hidden(xb_scr[rs, :], wg_ref[...], wu_ref[...]), gcols)
            x2_out[rs, :] += _dot(h, wd_ref[...])
            return carry

        lax.fori_loop(0, TM // RC, rows, 0)

    @pl.when(j == pl.num_programs(1) - 1)
    def _finish():
        x2_out[...] = _layernorm(DN_ALPHA * x_ref[...] + x2_out[...], g2_ref[...], b2_ref[...])


MOE_TM = 1024
MOE_CAP = 160


def _moe_call(x2d, ew):
    N = x2d.shape[0]
    TM = min(MOE_TM, N)
    HT = TM // 2
    CAP = min(MOE_CAP, HT)
    GW = EXPERTS_PER_GROUP * D_EXPERT
    row = pl.BlockSpec((TM, D_MODEL), lambda i, j: (i, 0))
    return pl.pallas_call(
        functools.partial(_moe_kernel, TM=TM, HT=HT, CAP=CAP),
        grid=(N // TM, N_GROUPS),
        in_specs=[row, _const_spec((N_EXPERTS, D_MODEL)), _const_spec((N_EXPERTS, 1)), _const_spec((HT, HT)),
                  _const_spec((D_MODEL, D_SHARED)), _const_spec((D_MODEL, D_SHARED)), _const_spec((D_SHARED, D_MODEL)),
                  pl.BlockSpec((D_MODEL, GW), lambda i, j: (0, j)), pl.BlockSpec((D_MODEL, GW), lambda i, j: (0, j)),
                  pl.BlockSpec((GW, D_MODEL), lambda i, j: (j, 0)),
                  _const_spec((1, D_MODEL)), _const_spec((1, D_MODEL))],
        out_specs=row,
        out_shape=jax.ShapeDtypeStruct((N, D_MODEL), F32),
        scratch_shapes=[pltpu.VMEM((TM, D_MODEL), BF16), pltpu.VMEM((TM, 128), F32), pltpu.VMEM((8, TM), F32),
                        pltpu.SMEM((1,), jnp.int32)],
        compiler_params=pltpu.CompilerParams(dimension_semantics=("parallel", "arbitrary"),
                                             vmem_limit_bytes=VMEM_LIMIT),
        name="moe",
    )(x2d, ew["rwt"], ew["rb"], jnp.triu(jnp.ones((HT, HT), BF16), 1), ew["wgs"], ew["wus"], ew["wds"],
      ew["wg"], ew["wu"], ew["wd"], ew["g2"], ew["b2"])


def _rot_cols(w):
    half = w.shape[-1] // 2
    return jnp.concatenate([-w[..., half:], w[..., :half]], axis=-1)


def _prep_layer(l, w_in, b_in, q_norm_g, kv_norm_g, w_uq, w_ukv, gmlp_ln_g, gmlp_ln_b, gmlp_ws, gmlp_bs,
                w_branch, w_out, ln1_g, ln1_b, moe_w_gate, moe_w_up, moe_w_down, shared_w_gate,
                shared_w_up, shared_w_down, ln2_g, ln2_b):
    wi = w_in[l]
    bi = b_in[l][None, :]

    def proj_cols(m):
        rows = m.shape[0]
        zero = lambda n: jnp.zeros((rows, n), m.dtype)
        kr = m[:, O_KR:O_MQ]
        kr128 = jnp.concatenate([zero(KR_LANE), kr, zero(HEAD_PAD - KR_LANE - ROPE_DIM)], axis=1)
        krr128 = jnp.concatenate([zero(KR_LANE), _rot_cols(kr), zero(HEAD_PAD - KR_LANE - ROPE_DIM)], axis=1)
        gates = jnp.concatenate([m[:, O_MI:O_MO], zero(HEAD_PAD - 2 * MLSTM_HEADS)], axis=1)
        return jnp.concatenate([m[:, O_CQ:O_KR], kr128, krr128, m[:, O_MQ:O_MI], gates, m[:, O_MO:O_GT]], axis=1)

    uq = w_uq[l].reshape(Q_RANK, MLA_HEADS, NOPE_DIM + ROPE_DIM)
    zq = lambda n: jnp.zeros((Q_RANK, MLA_HEADS, n), F32)
    pad = HEAD_PAD - NOPE_DIM - ROPE_DIM
    wq_a = jnp.concatenate([uq, zq(pad)], axis=-1).reshape(Q_RANK, QK_PAD)
    wq_b = jnp.concatenate([zq(NOPE_DIM), _rot_cols(uq[..., NOPE_DIM:]), zq(pad)], axis=-1).reshape(Q_RANK, QK_PAD)
    ukv = w_ukv[l].reshape(KV_RANK, MLA_HEADS, NOPE_DIM + V_DIM)
    wk = jnp.concatenate([ukv[..., :NOPE_DIM], jnp.zeros((KV_RANK, MLA_HEADS, HEAD_PAD - NOPE_DIM), F32)],
                         axis=-1).reshape(KV_RANK, QK_PAD)
    wv = ukv[..., NOPE_DIM:].reshape(KV_RANK, V_ALL)
    pw = dict(
        w1=proj_cols(wi).astype(BF16), b1=proj_cols(bi),
        qg=q_norm_g[l][None, :], kvg=kv_norm_g[l][None, :],
        wq=jnp.concatenate([wq_a, wq_b], axis=1).astype(BF16),
        wkv=jnp.concatenate([wk, wv], axis=1).astype(BF16),
        lng=gmlp_ln_g[l][None, :], lnb=gmlp_ln_b[l][None, :],
    )
    mw = dict(
        wgt=wi[:, O_GT:].astype(BF16), bgt=bi[:, O_GT:],
        ws_full=gmlp_ws[l], bs_full=gmlp_bs[l],
        wb=w_branch[l].astype(BF16), wo=w_out[l].astype(BF16),
        g1=ln1_g[l][None, :], b1=ln1_b[l][None, :],
    )
    cat_in = lambda we: jnp.transpose(we[l].astype(BF16), (1, 0, 2)).reshape(D_MODEL, N_EXPERTS * D_EXPERT)
    ew = dict(
        wg=cat_in(moe_w_gate), wu=cat_in(moe_w_up),
        wd=moe_w_down[l].reshape(N_EXPERTS * D_EXPERT, D_MODEL).astype(BF16),
        wgs=shared_w_gate[l].astype(BF16), wus=shared_w_up[l].astype(BF16), wds=shared_w_down[l].astype(BF16),
        g2=ln2_g[l][None, :], b2=ln2_b[l][None, :],
    )
    return pw, mw, ew


def _rope_tables(T, past, rows):
    half = ROPE_DIM // 2
    pos = (past + jnp.arange(T)).astype(F32)
    inv = ROPE_THETA ** (-jnp.arange(half, dtype=F32) / half)
    ang = pos[:, None] * inv[None, :]
    cos = jnp.cos(ang)
    sin = jnp.sin(ang)
    c2 = jnp.concatenate([cos, cos], axis=1)
    s2 = jnp.concatenate([sin, sin], axis=1)
    z = lambda n: jnp.zeros((T, n), F32)
    tail = HEAD_PAD - NOPE_DIM - ROPE_DIM
    tabs = dict(
        cq=jnp.concatenate([jnp.ones((T, NOPE_DIM), F32), c2, z(tail)], axis=1) * Q_SCALE,
        sq=jnp.concatenate([z(NOPE_DIM), s2, z(tail)], axis=1) * Q_SCALE,
        ck=jnp.concatenate([z(KR_LANE), c2, z(tail)], axis=1),
        sk=jnp.concatenate([z(KR_LANE), s2, z(tail)], axis=1),
    )
    if rows > T:
        tabs = {k: jnp.tile(v, (rows // T, 1)) for k, v in tabs.items()}
    return tabs


def _trunk(x, hist, layers, rwt, rb):
    B, T, _ = x.shape
    N = B * T
    past = 0 if hist is None else hist[0].shape[2]
    tabs = _rope_tables(T, past, min(512, N))
    L = GMLP_CHUNK if T % GMLP_CHUNK == 0 else T
    x2d = x.reshape(N, D_MODEL)
    ckvs, krs, Cs, ns, ms, gvs = [], [], [], [], [], []
    for l, (pw, mw, ew) in enumerate(layers):
        KB = _attn_block(T) if hist is None else None
        outs = _proj_call(x2d, pw, tabs, T, hist is not None, KB)
        q, k, v, ckvn, kr, mq, mk, mv, gates, so, gu, gv = outs[:12]
        seq = lambda a: a.reshape(B, T, a.shape[-1])
        if hist is None:
            a_out = _attn_call(seq(q), k.reshape(B, T // KB, QK_PAD, KB), seq(v))
            s0 = jnp.zeros((B, MLSTM_HEADS, MLSTM_DH, 2 * MLSTM_DH), F32)
            m0 = jnp.zeros((B, MLSTM_HEADS, 8, 128), F32)
        else:
            ckv_past, kr_past, c0, n0, m0_in = hist
            P = ckv_past.shape[2]
            krp = jnp.pad(kr_past[l], ((0, 0), (0, 0), (KR_LANE, HEAD_PAD - KR_LANE - ROPE_DIM)))
            a_out = _attn_hist_call(seq(q), ckv_past[l], krp, pw["wkv"], seq(k), seq(v))
            s0 = jnp.concatenate(
                [jnp.swapaxes(c0[l], -1, -2),
                 jnp.broadcast_to(n0[l][:, :, :, None], (B, MLSTM_HEADS, MLSTM_DH, MLSTM_DH))], axis=3)
            m0 = jnp.broadcast_to(m0_in[l][:, :, None, None], (B, MLSTM_HEADS, 8, 128))
        b_out, sfin, mfin = _mlstm_call(seq(mq), seq(mk), seq(mv), seq(gates), seq(so), s0, m0)
        mwl = dict(mw)
        mwl["ws"] = mw["ws_full"][:, :L, :L]
        mwl["bsf"] = jnp.repeat(mw["bs_full"][:, :L].T, GMLP_DG, axis=1)
        x1 = _merge_call(x2d, a_out.reshape(N, V_ALL), b_out.reshape(N, MLSTM_WIDTH), gu, gv, mwl, L)
        x2d = _moe_call(x1, dict(ew, rwt=rwt, rb=rb))
        ckvs.append(ckvn.reshape(B, T, KV_RANK))
        krs.append(kr[:, KR_LANE:KR_LANE + ROPE_DIM].reshape(B, T, ROPE_DIM))
        Cs.append(jnp.swapaxes(sfin[:, :, :, :MLSTM_DH], -1, -2))
        ns.append(sfin[:, :, :, MLSTM_DH])
        ms.append(mfin[:, :, 0, 0])
        if hist is not None:
            gvs.append(outs[12].reshape(B, T, GMLP_WIDTH))
    res = [x2d.reshape(B, T, D_MODEL), jnp.stack(ckvs), jnp.stack(krs), jnp.stack(Cs), jnp.stack(ns), jnp.stack(ms)]
    if hist is not None:
        res.append(jnp.stack(gvs))
    return res


def kernel(x_prompt, x_sample, cache_mla_ckv, cache_mla_krope, state_mlstm_c, state_mlstm_n, state_mlstm_m,
           w_in, b_in, q_norm_g, kv_norm_g, w_uq, w_ukv, gmlp_ln_g, gmlp_ln_b, gmlp_ws, gmlp_bs, w_branch,
           w_out, ln1_g, ln1_b, router_w, router_b, moe_w_gate, moe_w_up, moe_w_down, shared_w_gate,
           shared_w_up, shared_w_down, ln2_g, ln2_b):
    depth = w_in.shape[0]
    layers = [_prep_layer(l, w_in, b_in, q_norm_g, kv_norm_g, w_uq, w_ukv, gmlp_ln_g, gmlp_ln_b, gmlp_ws,
                          gmlp_bs, w_branch, w_out, ln1_g, ln1_b, moe_w_gate, moe_w_up, moe_w_down,
                          shared_w_gate, shared_w_up, shared_w_down, ln2_g, ln2_b) for l in range(depth)]
    rwt = router_w.T.astype(BF16)
    rb = router_b[:, None]
    yp, p_ckv, p_kr, p_c, p_n, p_m = _trunk(x_prompt, None, layers, rwt, rb)
    ys, s_ckv, s_kr, s_c, s_n, s_m, s_gv = _trunk(
        x_sample, (cache_mla_ckv, cache_mla_krope, state_mlstm_c, state_mlstm_n, state_mlstm_m), layers, rwt, rb)
    return (yp, ys, p_ckv, p_kr, p_c, p_n, p_m, s_ckv, s_kr, s_c, s_n, s_m, s_gv)
```

```python
import functools

import jax
import jax.numpy as jnp
import numpy as np
from jax import lax
from jax.experimental import pallas as pl
from jax.experimental.pallas import tpu as pltpu

F32 = jnp.float32
BF16 = jnp.bfloat16

D_MODEL = 1024
CHUNK = 64
MLA_HEADS = 8
Q_RANK = 256
KV_RANK = 256
NOPE_DIM = 64
ROPE_DIM = 32
V_DIM = 64
ROPE_THETA = 10000.0
ATTN_SCALE = (NOPE_DIM + ROPE_DIM) ** -0.5
Q_SCALE = ATTN_SCALE * float(np.log2(np.e))
MLSTM_HEADS = 4
MLSTM_DH = 128
MLSTM_WIDTH = MLSTM_HEADS * MLSTM_DH
GMLP_GROUPS = 4
GMLP_DG = 128
GMLP_WIDTH = GMLP_GROUPS * GMLP_DG
GMLP_CHUNK = 128
N_BRANCH = 3
N_EXPERTS = 16
N_GROUPS = 4
EXPERTS_PER_GROUP = N_EXPERTS // N_GROUPS
D_EXPERT = 256
D_SHARED = 256
DEPTH = 4
DN_ALPHA = (2 * DEPTH) ** 0.25
EPS = 1e-5

HEAD_PAD = 128
QK_PAD = MLA_HEADS * HEAD_PAD
V_ALL = MLA_HEADS * V_DIM
P_CQ, P_CKV, P_KR, P_KRR, P_MQ, P_MK, P_MV, P_G, P_MO, P_GU, P_GV = (
    0, 256, 512, 640, 768, 1280, 1792, 2304, 2432, 2944, 3456)
D_PROJ = 3968
O_CQ, O_CKV, O_KR, O_MQ, O_MK, O_MV, O_MI, O_MF, O_MO, O_GU, O_GV, O_GT = (
    0, 256, 512, 544, 1056, 1568, 2080, 2084, 2088, 2600, 3112, 3624)
KR_LANE = NOPE_DIM
VMEM_LIMIT = 56 * 1024 * 1024
NEG_INF = float("-inf")


def _dot(a, b):
    return jnp.dot(a, b, preferred_element_type=F32)


def _dot_nt(a, b):
    return lax.dot_general(a, b, (((1,), (1,)), ((), ())), preferred_element_type=F32)


def _const_spec(shape):
    nd = len(shape)
    return pl.BlockSpec(shape, lambda *_: (0,) * nd, pipeline_mode=pl.Buffered(1))


def _layernorm(x, g, b):
    mu = jnp.mean(x, axis=-1, keepdims=True)
    xc = x - mu
    var = jnp.mean(xc * xc, axis=-1, keepdims=True)
    return xc * lax.rsqrt(var + EPS) * g + b


def _rmsnorm(x, g):
    return x * lax.rsqrt(jnp.mean(x * x, axis=-1, keepdims=True) + EPS) * g


def _gelu(x):
    return 0.5 * x * (1.0 + jnp.tanh(np.sqrt(2.0 / np.pi).astype(np.float32) * (x + 0.044715 * (x * x * x))))


def _sigmoid(x):
    return 1.0 / (1.0 + jnp.exp(-x))


def _log_sigmoid(x):
    return jnp.minimum(x, 0.0) - jnp.log(1.0 + jnp.exp(-jnp.abs(x)))


PROJ_N_IN = 13


def _proj_kernel(*refs, key_block, n_carried):
    (x_ref, w1_ref, b1_ref, qg_ref, kvg_ref, wq_ref, wkv_ref, cq_ref, sq_ref, ck_ref, sk_ref,
     lng_ref, lnb_ref) = refs[:PROJ_N_IN]
    (q_out, k_out, v_out, ckv_out, kr_out, mq_out, mk_out, mv_out, g_out, so_out, gu_out,
     gv_out, *rest) = refs[PROJ_N_IN + n_carried:]
    xb = x_ref[...].astype(BF16)

    def z(lo, hi):
        return _dot(xb, w1_ref[:, lo:hi]) + b1_ref[:, lo:hi]

    zc = z(P_CQ, P_MQ)
    cqn = _rmsnorm(zc[:, P_CQ:P_CKV], qg_ref[...])
    ckvn = _rmsnorm(zc[:, P_CKV:P_KR], kvg_ref[...])
    kr = zc[:, P_KR:P_KRR] * ck_ref[...] + zc[:, P_KRR:P_MQ] * sk_ref[...]
    ckv_out[...] = ckvn
    kr_out[...] = kr[:, KR_LANE:KR_LANE + ROPE_DIM]
    qq = _dot(cqn.astype(BF16), wq_ref[...])
    cos8 = jnp.concatenate([cq_ref[...]] * MLA_HEADS, axis=1)
    sin8 = jnp.concatenate([sq_ref[...]] * MLA_HEADS, axis=1)
    q_out[...] = (qq[:, :QK_PAD] * cos8 + qq[:, QK_PAD:] * sin8).astype(BF16)
    kk = _dot(ckvn.astype(BF16), wkv_ref[...])
    kr8 = jnp.concatenate([kr] * MLA_HEADS, axis=1)
    kval = kk[:, :QK_PAD] + kr8
    if key_block is None:
        k_out[...] = kval.astype(BF16)
    else:
        for c in range(kval.shape[0] // key_block):
            k_out[c] = kval[key_block * c:key_block * (c + 1), :].T.astype(BF16)
    v_out[...] = kk[:, QK_PAD:].astype(BF16)

    zm = z(P_MQ, P_MO)
    mq_out[...] = zm[:, 0:512].astype(BF16)
    mk_out[...] = (zm[:, 512:1024] * (MLSTM_DH ** -0.5)).astype(BF16)
    mv_out[...] = zm[:, 1024:1536].astype(BF16)
    zg = zm[:, 1536:1664]
    lane = lax.broadcasted_iota(jnp.int32, zg.shape, 1)
    g_out[...] = jnp.where(lane < MLSTM_HEADS, zg, _log_sigmoid(zg))
    so_out[...] = _sigmoid(z(P_MO, P_GU)).astype(BF16)

    zu = z(P_GU, D_PROJ)
    gu_out[...] = _gelu(zu[:, :GMLP_WIDTH]).astype(BF16)
    vrows = _layernorm(_gelu(zu[:, GMLP_WIDTH:]), lng_ref[...], lnb_ref[...])
    gv_out[...] = vrows.astype(BF16)
    if rest:
        rest[0][...] = vrows


CKV_OUT, KR_OUT = 3, 4


def _proj_call(x2d, pw, tabs, T, want_vrows, key_block, layer, depth, carried):
    N = x2d.shape[0]
    TM = min(512, N)
    nt = max(1, T // TM)
    grid = (N // TM,)
    row = lambda w: pl.BlockSpec((TM, w), lambda i: (i, 0))
    tab = pl.BlockSpec((TM, HEAD_PAD), lambda i: (i % nt, 0))
    in_specs = [
        row(D_MODEL),
        _const_spec((D_MODEL, D_PROJ)), _const_spec((1, D_PROJ)),
        _const_spec((1, Q_RANK)), _const_spec((1, KV_RANK)),
        _const_spec((Q_RANK, 2 * QK_PAD)), _const_spec((KV_RANK, QK_PAD + V_ALL)),
        tab, tab, tab, tab,
        _const_spec((1, GMLP_WIDTH)), _const_spec((1, GMLP_WIDTH)),
    ]
    out_shapes = [
        ((N, QK_PAD), BF16), ((N, QK_PAD), BF16), ((N, V_ALL), BF16), ((N, KV_RANK), F32),
        ((N, HEAD_PAD), F32), ((N, MLSTM_WIDTH), BF16), ((N, MLSTM_WIDTH), BF16), ((N, MLSTM_WIDTH), BF16),
        ((N, HEAD_PAD), F32), ((N, MLSTM_WIDTH), BF16), ((N, GMLP_WIDTH), BF16), ((N, GMLP_WIDTH), BF16),
    ]
    if want_vrows:
        out_shapes.append(((N, GMLP_WIDTH), F32))
    out_specs = [row(s[1]) for s, _ in out_shapes]
    if key_block is not None:
        out_shapes[1] = ((N // key_block, QK_PAD, key_block), BF16)
        out_specs[1] = pl.BlockSpec((TM // key_block, QK_PAD, key_block), lambda i: (i, 0, 0))
    for pos, width in ((CKV_OUT, KV_RANK), (KR_OUT, ROPE_DIM)):
        out_shapes[pos] = ((depth, N, width), F32)
        out_specs[pos] = pl.BlockSpec((None, TM, width), lambda i: (layer, i, 0))
    args = [x2d, pw["w1"], pw["b1"], pw["qg"], pw["kvg"], pw["wq"], pw["wkv"],
            tabs["cq"], tabs["sq"], tabs["ck"], tabs["sk"], pw["lng"], pw["lnb"]]
    aliases = {}
    if carried is not None:
        in_specs += [pl.BlockSpec(memory_space=pl.ANY)] * 2
        aliases = {PROJ_N_IN: CKV_OUT, PROJ_N_IN + 1: KR_OUT}
        args += list(carried)
    return pl.pallas_call(
        functools.partial(_proj_kernel, key_block=key_block, n_carried=len(aliases)),
        grid=grid,
        in_specs=in_specs,
        out_specs=out_specs,
        out_shape=[jax.ShapeDtypeStruct(s, d) for s, d in out_shapes],
        input_output_aliases=aliases,
        compiler_params=pltpu.CompilerParams(dimension_semantics=("parallel",), vmem_limit_bytes=VMEM_LIMIT),
        name="proj",
    )(*args)


def _attn_kernel(q_ref, kt_ref, v_ref, o_ref, mx_scr, acc_scr, s_scr, *, TQ):
    i = pl.program_id(1)
    ones_blk = jnp.ones((TQ, HEAD_PAD), BF16)
    rc = lax.broadcasted_iota(jnp.int32, (TQ, TQ), 0) // CHUNK
    cc = lax.broadcasted_iota(jnp.int32, (TQ, TQ), 1) // CHUNK
    visible = cc <= rc
    lane = lax.broadcasted_iota(jnp.int32, (TQ, HEAD_PAD), 1)
    nlane = TQ // 128

    def fold(a, op):
        r = a[:, 0:128]
        for t in range(1, nlane):
            r = op(r, a[:, 128 * t:128 * (t + 1)])
        return r

    hsl = lambda h: slice(HEAD_PAD * h, HEAD_PAD * (h + 1))
    heads = range(MLA_HEADS)

    def sweep_max(j, masked, first):
        ss = [_dot(q_ref[0, :, hsl(h)], kt_ref[0, j, hsl(h), :]) for h in heads]
        if masked:
            ss = [jnp.where(visible, s, NEG_INF) for s in ss]
        for h in heads:
            s_scr[h, j] = ss[h]
        for h in heads:
            mx = fold(ss[h], jnp.maximum)
            mx_scr[h] = mx if first else jnp.maximum(mx_scr[h], mx)

    def sweep_pv(j, first):
        r0 = pl.multiple_of(j * TQ, TQ)
        ps = [jnp.exp2(s_scr[h, j] - jnp.concatenate([mx_scr[h]] * nlane, axis=1)).astype(BF16) for h in heads]
        pvs = []
        for p in range(MLA_HEADS // 2):
            vs = slice(HEAD_PAD * p, HEAD_PAD * (p + 1))
            vext = jnp.concatenate([v_ref[0, pl.ds(r0, TQ), vs], ones_blk], axis=1)
            pv2 = _dot(jnp.concatenate([ps[2 * p], ps[2 * p + 1]], axis=0), vext)
            pvs += [pv2[:TQ], pv2[TQ:]]
        for h in heads:
            acc_scr[h] = pvs[h] if first else acc_scr[h] + pvs[h]

    def body_max(j, c):
        sweep_max(j, False, False)
        return c

    def body_pv(j, c):
        sweep_pv(j, False)
        return c

    sweep_max(i, True, True)
    lax.fori_loop(0, i, body_max, 0)
    for h in heads:
        mx_scr[h] = jnp.broadcast_to(jnp.max(mx_scr[h], axis=-1, keepdims=True), (TQ, 128))
    sweep_pv(i, True)
    lax.fori_loop(0, i, body_pv, 0)
    for p in range(MLA_HEADS // 2):
        a0 = acc_scr[2 * p]
        a1 = acc_scr[2 * p + 1]
        o0 = a0[:, :HEAD_PAD] / a0[:, HEAD_PAD:]
        o1 = a1[:, :HEAD_PAD] / a1[:, HEAD_PAD:]
        o_ref[0, :, HEAD_PAD * p:HEAD_PAD * (p + 1)] = jnp.where(lane < V_DIM, o0, o1).astype(BF16)


def _attn_block(T):
    return min(256, T)


def _attn_call(q, kt, v):
    B, T, _ = q.shape
    TQ = _attn_block(T)
    nq = T // TQ
    return pl.pallas_call(
        functools.partial(_attn_kernel, TQ=TQ),
        grid=(B, nq),
        in_specs=[pl.BlockSpec((1, TQ, QK_PAD), lambda b, i: (b, i, 0)),
                  pl.BlockSpec((1, nq, QK_PAD, TQ), lambda b, i: (b, 0, 0, 0)),
                  pl.BlockSpec((1, T, V_ALL), lambda b, i: (b, 0, 0))],
        out_specs=pl.BlockSpec((1, TQ, V_ALL), lambda b, i: (b, i, 0)),
        out_shape=jax.ShapeDtypeStruct((B, T, V_ALL), BF16),
        scratch_shapes=[pltpu.VMEM((MLA_HEADS, TQ, 128), F32), pltpu.VMEM((MLA_HEADS, TQ, 2 * HEAD_PAD), F32),
                        pltpu.VMEM((MLA_HEADS, nq, TQ, TQ), F32)],
        compiler_params=pltpu.CompilerParams(dimension_semantics=("parallel", "arbitrary"),
                                             vmem_limit_bytes=VMEM_LIMIT),
        name="attn",
    )(q, kt, v)


def _attn_hist_kernel(q_ref, cp_ref, krp_ref, wkv_ref, kn_ref, vn_ref, o_ref, kp_ref, vp_ref):
    T = q_ref.shape[1]
    lane = lax.broadcasted_iota(jnp.int32, (T, HEAD_PAD), 1)
    kk = _dot(cp_ref[0].astype(BF16), wkv_ref[...])
    kp_ref[0] = (kk[:, :QK_PAD] + jnp.concatenate([krp_ref[0]] * MLA_HEADS, axis=1)).astype(BF16)
    vp_ref[0] = kk[:, QK_PAD:].astype(BF16)
    for p in range(MLA_HEADS // 2):
        pair = []
        vs = slice(HEAD_PAD * p, HEAD_PAD * (p + 1))
        for hh in range(2):
            h = 2 * p + hh
            hs = slice(HEAD_PAD * h, HEAD_PAD * (h + 1))
            qh = q_ref[0, :, hs]
            s1 = _dot_nt(qh, kp_ref[0, :, hs])
            s2 = _dot_nt(qh, kn_ref[0, :, hs])
            m = jnp.maximum(jnp.max(s1, axis=-1, keepdims=True), jnp.max(s2, axis=-1, keepdims=True))
            p1 = jnp.exp2(s1 - m)
            p2 = jnp.exp2(s2 - m)
            l = jnp.sum(p1, axis=-1, keepdims=True) + jnp.sum(p2, axis=-1, keepdims=True)
            acc = _dot(p1.astype(BF16), vp_ref[0, :, vs]) + _dot(p2.astype(BF16), vn_ref[0, :, vs])
            pair.append(acc / l)
        o_ref[0, :, vs] = jnp.where(lane < V_DIM, pair[0], pair[1]).astype(BF16)


def _attn_hist_call(q, ckv_past, krp, wkv, kn, vn):
    B, T, _ = q.shape
    P = ckv_past.shape[1]
    blk = lambda t, w: pl.BlockSpec((1, t, w), lambda b: (b, 0, 0))
    return pl.pallas_call(
        _attn_hist_kernel,
        grid=(B,),
        in_specs=[blk(T, QK_PAD), blk(P, KV_RANK), blk(P, HEAD_PAD), _const_spec((KV_RANK, QK_PAD + V_ALL)),
                  blk(T, QK_PAD), blk(T, V_ALL)],
        out_specs=blk(T, V_ALL),
        out_shape=jax.ShapeDtypeStruct((B, T, V_ALL), BF16),
        scratch_shapes=[pltpu.VMEM((1, P, QK_PAD), BF16), pltpu.VMEM((1, P, V_ALL), BF16)],
        compiler_params=pltpu.CompilerParams(dimension_semantics=("parallel",), vmem_limit_bytes=VMEM_LIMIT),
        name="attn_hist",
    )(q, ckv_past, krp, wkv, kn, vn)


MLSTM_BB = 4


def _mlstm_kernel(q_ref, k_ref, v_ref, g_ref, so_ref, s0_ref, m0_ref, h_out, sfin_out, mfin_out,
                  s_scr, m_scr, *, TL, BB):
    t = pl.program_id(1)
    L = CHUNK
    DH = MLSTM_DH

    @pl.when(t == 0)
    def _():
        s_scr[...] = s0_ref[...]
        m_scr[...] = m0_ref[...]

    r_i = lax.broadcasted_iota(jnp.int32, (L, L), 0)
    c_i = lax.broadcasted_iota(jnp.int32, (L, L), 1)
    causal = c_i <= r_i
    tri = jnp.where(causal, 1.0, 0.0).astype(BF16)
    lane = lax.broadcasted_iota(jnp.int32, (L, 128), 1)
    ones_blk = jnp.ones((L, DH), BF16)

    def rep(col):
        return jnp.broadcast_to(col, (L, 128))

    def chunk(c, carry):
        rows = pl.ds(pl.multiple_of(c * L, L), L)
        pairs = [(bb, h) for bb in range(BB) for h in range(MLSTM_HEADS)]
        hsl = lambda h: slice(DH * h, DH * (h + 1))
        Gs, cums, VTs = [], [], []
        for bb in range(BB):
            G = g_ref[bb, rows, :]
            g_hi = G.astype(BF16)
            g_r1 = G - g_hi.astype(F32)
            g_mid = g_r1.astype(BF16)
            g_lo = (g_r1 - g_mid.astype(F32)).astype(BF16)
            cum = _dot(tri, g_hi) + _dot(tri, g_mid) + _dot(tri, g_lo)
            Gs.append(G)
            cums.append(cum)
            VTs.append(jnp.where(lane < MLSTM_HEADS, G, cum).T)
        qk_raw = {p: _dot_nt(q_ref[p[0], rows, hsl(p[1])], k_ref[p[0], rows, hsl(p[1])]) for p in pairs}
        sq = {p: _dot(q_ref[p[0], rows, hsl(p[1])], s_scr[p[0], p[1]].astype(BF16)) for p in pairs}
        gate = {}
        for bb, h in pairs:
            b_t = rep(jnp.sum(jnp.where(lane == MLSTM_HEADS + h, cums[bb], 0.0), axis=1, keepdims=True))
            ig_t = rep(jnp.sum(jnp.where(lane == h, Gs[bb], 0.0), axis=1, keepdims=True))
            brow = VTs[bb][MLSTM_HEADS + h:MLSTM_HEADS + h + 1, :]
            igrow = VTs[bb][h:h + 1, :]
            d = jnp.where(causal, b_t[:, :L] - brow + igrow, NEG_INF)
            gate[bb, h] = (b_t, ig_t, d, rep(jnp.max(d, axis=1, keepdims=True)))
        stab = {}
        for bb, h in pairs:
            b_t, ig_t, d, dmax = gate[bb, h]
            m_prev = m_scr[bb, h, 0:1, :]
            g_t = b_t + m_prev
            mt = jnp.maximum(g_t, dmax)
            stab[bb, h] = (m_prev, mt, jnp.exp(g_t - mt), jnp.exp(d - mt[:, :L]))
        for bb, h in pairs:
            m_prev, mt, inter, w = stab[bb, h]
            vext = jnp.concatenate([v_ref[bb, rows, hsl(h)], ones_blk], axis=1)
            intra = _dot((qk_raw[bb, h] * w).astype(BF16), vext)
            num = inter * sq[bb, h][:, :DH] + intra[:, :DH]
            den = jnp.maximum(jnp.abs(inter * sq[bb, h][:, DH:] + intra[:, DH:]), jnp.exp(-mt))
            so = so_ref[bb, rows, hsl(h)].astype(F32)
            h_out[bb, rows, hsl(h)] = (so * (num / den)).astype(BF16)
        for bb, h in pairs:
            b_t, ig_t, _, _ = gate[bb, h]
            m_prev, mt, _, _ = stab[bb, h]
            mL = mt[L - 1:L, :]
            bL = b_t[L - 1:L, :]
            ws_t = jnp.exp(bL - b_t + ig_t - mL)
            decay = jnp.exp(bL + m_prev - mL)
            wv = jnp.concatenate([ws_t * v_ref[bb, rows, hsl(h)].astype(F32), ws_t], axis=1).astype(BF16)
            kT = k_ref[bb, rows, hsl(h)].astype(F32).T.astype(BF16)
            s_scr[bb, h] = jnp.concatenate([decay, decay], axis=1) * s_scr[bb, h] + _dot(kT, wv)
            m_scr[bb, h] = jnp.broadcast_to(mL, (8, 128))
        return carry

    lax.fori_loop(0, TL // L, chunk, 0)

    @pl.when(t == pl.num_programs(1) - 1)
    def _():
        sfin_out[...] = s_scr[...]
        mfin_out[...] = m_scr[...]


def _mlstm_call(mq, mk, mv, gates, so, s0, m0):
    B, T, _ = mq.shape
    TL = min(512, T)
    BB = int(np.gcd(MLSTM_BB, B))
    seq = lambda w: pl.BlockSpec((BB, TL, w), lambda b, t: (b, t, 0))
    st = pl.BlockSpec((BB, MLSTM_HEADS, MLSTM_DH, 2 * MLSTM_DH), lambda b, t: (b, 0, 0, 0))
    mst = pl.BlockSpec((BB, MLSTM_HEADS, 8, 128), lambda b, t: (b, 0, 0, 0))
    return pl.pallas_call(
        functools.partial(_mlstm_kernel, TL=TL, BB=BB),
        grid=(B // BB, T // TL),
        in_specs=[seq(MLSTM_WIDTH), seq(MLSTM_WIDTH), seq(MLSTM_WIDTH), seq(128), seq(MLSTM_WIDTH), st, mst],
        out_specs=[seq(MLSTM_WIDTH), st, mst],
        out_shape=[jax.ShapeDtypeStruct((B, T, MLSTM_WIDTH), BF16),
                   jax.ShapeDtypeStruct((B, MLSTM_HEADS, MLSTM_DH, 2 * MLSTM_DH), F32),
                   jax.ShapeDtypeStruct((B, MLSTM_HEADS, 8, 128), F32)],
        scratch_shapes=[pltpu.VMEM((BB, MLSTM_HEADS, MLSTM_DH, 2 * MLSTM_DH), F32),
                        pltpu.VMEM((BB, MLSTM_HEADS, 8, 128), F32)],
        compiler_params=pltpu.CompilerParams(dimension_semantics=("parallel", "arbitrary"),
                                             vmem_limit_bytes=VMEM_LIMIT),
        name="mlstm",
    )(mq, mk, mv, gates, so, s0, m0)


def _merge_kernel(x_ref, a_ref, b_ref, gu_ref, gv_ref, wgt_ref, bgt_ref, ws_ref, bsf_ref, wb_ref, wo_ref,
                  g1_ref, b1_ref, x1_out, c_scr, *, L, TM):
    x = x_ref[...]
    xb = x.astype(BF16)
    r_i = lax.broadcasted_iota(jnp.int32, (L, L), 0)
    c_i = lax.broadcasted_iota(jnp.int32, (L, L), 1)
    for g in range(GMLP_GROUPS):
        gs = slice(GMLP_DG * g, GMLP_DG * (g + 1))
        wsg = jnp.where(c_i <= r_i, ws_ref[g], 0.0).astype(BF16)
        for c in range(TM // L):
            rs = slice(L * c, L * (c + 1))
            sp = _dot(wsg, gv_ref[rs, gs]) + bsf_ref[:, gs]
            c_scr[rs, gs] = (gu_ref[rs, gs].astype(F32) * sp).astype(BF16)
    merged = None
    for kb, br in enumerate((a_ref, b_ref, c_scr)):
        cs = slice(D_MODEL * kb, D_MODEL * (kb + 1))
        gate = _sigmoid(_dot(xb, wgt_ref[:, cs]) + bgt_ref[:, cs])
        term = gate * _dot(br[...], wb_ref[kb])
        merged = term if merged is None else merged + term
    y = _dot(merged.astype(BF16), wo_ref[...])
    x1_out[...] = _layernorm(DN_ALPHA * x + y, g1_ref[...], b1_ref[...])


def _merge_call(x2d, a2d, b2d, gu, gv, mw, L):
    N = x2d.shape[0]
    TM = min(512, N)
    row = lambda w: pl.BlockSpec((TM, w), lambda i: (i, 0))
    return pl.pallas_call(
        functools.partial(_merge_kernel, L=L, TM=TM),
        grid=(N // TM,),
        in_specs=[row(D_MODEL), row(V_ALL), row(MLSTM_WIDTH), row(GMLP_WIDTH), row(GMLP_WIDTH),
                  _const_spec((D_MODEL, N_BRANCH * D_MODEL)), _const_spec((1, N_BRANCH * D_MODEL)),
                  _const_spec((GMLP_GROUPS, L, L)), _const_spec((L, GMLP_WIDTH)),
                  _const_spec((N_BRANCH, 512, D_MODEL)), _const_spec((D_MODEL, D_MODEL)),
                  _const_spec((1, D_MODEL)), _const_spec((1, D_MODEL))],
        out_specs=row(D_MODEL),
        out_shape=jax.ShapeDtypeStruct((N, D_MODEL), F32),
        scratch_shapes=[pltpu.VMEM((TM, GMLP_WIDTH), BF16)],
        compiler_params=pltpu.CompilerParams(dimension_semantics=("parallel",), vmem_limit_bytes=VMEM_LIMIT),
        name="merge",
    )(x2d, a2d, b2d, gu, gv, mw["wgt"], mw["bgt"], mw["ws"], mw["bsf"], mw["wb"], mw["wo"], mw["g1"], mw["b1"])


def _route_rows(logits_t, rb_ref):
    s = _sigmoid(logits_t)
    sb = s + rb_ref[...]
    rows = [sb[e:e + 1, :] for e in range(N_EXPERTS)]
    srow = [s[e:e + 1, :] for e in range(N_EXPERTS)]
    gscore = []
    for g in range(N_GROUPS):
        mem = rows[EXPERTS_PER_GROUP * g:EXPERTS_PER_GROUP * (g + 1)]
        best = None
        for a in range(EXPERTS_PER_GROUP):
            for b in range(a + 1, EXPERTS_PER_GROUP):
                pr = mem[a] + mem[b]
                best = pr if best is None else jnp.maximum(best, pr)
        gscore.append(best)
    gmax = functools.reduce(jnp.maximum, gscore)
    taken = None
    gsel = []
    for g in range(N_GROUPS):
        hit = gscore[g] == gmax
        if taken is None:
            gsel.append(hit)
            taken = hit
        else:
            gsel.append(jnp.logical_and(hit, jnp.logical_not(taken)))
            taken = jnp.logical_or(taken, hit)
    sel_w = []
    for e in range(N_EXPERTS):
        g = e // EXPERTS_PER_GROUP
        rank = None
        for o in range(EXPERTS_PER_GROUP * g, EXPERTS_PER_GROUP * (g + 1)):
            if o == e:
                continue
            ahead = (rows[o] >= rows[e]) if o < e else (rows[o] > rows[e])
            ahead = jnp.where(ahead, 1.0, 0.0)
            rank = ahead if rank is None else rank + ahead
        chosen = jnp.logical_and(gsel[g], rank < 1.5)
        sel_w.append(jnp.where(chosen, srow[e], 0.0))
    den = functools.reduce(jnp.add, sel_w)
    gate_rows = [w_ / den for w_ in sel_w]
    return gsel, gate_rows


def _swiglu_hidden(xb, wg, wu):
    hg = _dot(xb, wg)
    return hg * _sigmoid(hg) * _dot(xb, wu)


def _moe_kernel(x_ref, rw_ref, rb_ref, tri_ref, wgs_ref, wus_ref, wds_ref, wg_ref, wu_ref, wd_ref,
                g2_ref, b2_ref, x2_out, xb_scr, col_scr, row_scr, flag_ref, *, TM, HT, CAP):
    j = pl.program_id(1)
    nh = TM // HT
    gate_lanes = EXPERTS_PER_GROUP
    RANK_LANE, GRP_LANE = gate_lanes, gate_lanes + 1

    @pl.when(j == 0)
    def _route():
        xb = x_ref[...].astype(BF16)
        xb_scr[...] = xb
        gsel, gate_rows = _route_rows(_dot_nt(rw_ref[...], xb), rb_ref)
        isg = [jnp.where(m, 1.0, 0.0) for m in gsel]
        grp = functools.reduce(jnp.add, [float(g) * isg[g] for g in range(N_GROUPS)])
        g4 = [functools.reduce(jnp.add, [isg[g] * gate_rows[EXPERTS_PER_GROUP * g + e] for g in range(N_GROUPS)])
              for e in range(EXPERTS_PER_GROUP)]
        ranks = []
        worst = None
        for hf in range(nh):
            hsl = slice(HT * hf, HT * (hf + 1))
            m8 = jnp.concatenate([isg[g][:, hsl] for g in range(N_GROUPS)]
                                 + [jnp.zeros((8 - N_GROUPS, HT), F32)], axis=0).astype(BF16)
            before = _dot(m8, tri_ref[...])
            ranks.append(functools.reduce(jnp.add, [isg[g][:, hsl] * before[g:g + 1, :] for g in range(N_GROUPS)]))
            for g in range(N_GROUPS):
                cnt = jnp.sum(isg[g][:, hsl])
                worst = cnt if worst is None else jnp.maximum(worst, cnt)
        rank = jnp.concatenate(ranks, axis=1)
        flag_ref[0] = (worst > float(CAP)).astype(jnp.int32)
        row_scr[...] = jnp.concatenate([rank, grp, jnp.zeros((6, TM), F32)], axis=0)
        col_scr[...] = jnp.concatenate(g4 + [rank, grp, jnp.zeros((128 - gate_lanes - 2, TM), F32)], axis=0).T
        x2_out[...] = _dot(_swiglu_hidden(xb, wgs_ref[...], wus_ref[...]).astype(BF16), wds_ref[...])

    gf = j.astype(F32)
    overflow = flag_ref[0] != 0

    def gated(h, gcols):
        parts = [h[:, D_EXPERT * e:D_EXPERT * (e + 1)] * gcols[:, e:e + 1] for e in range(EXPERTS_PER_GROUP)]
        return jnp.concatenate(parts, axis=1).astype(BF16)

    @pl.when(jnp.logical_not(overflow))
    def _compact():
        r_iota = lax.broadcasted_iota(jnp.int32, (CAP, HT), 0).astype(F32)
        c_iota = lax.broadcasted_iota(jnp.int32, (HT, CAP), 1).astype(F32)
        xcs, gcs = [], []
        for hf in range(nh):
            hsl = slice(HT * hf, HT * (hf + 1))
            pick = jnp.logical_and(row_scr[1:2, hsl] == gf, row_scr[0:1, hsl] == r_iota)
            P = jnp.where(pick, 1.0, 0.0).astype(BF16)
            xcs.append(_dot(P, xb_scr[hsl, :]).astype(BF16))
            cols = col_scr[hsl, :]
            c_hi = cols.astype(BF16)
            c_lo = (cols - c_hi.astype(F32)).astype(BF16)
            gcs.append(_dot(P, c_hi) + _dot(P, c_lo))
        xc = jnp.concatenate(xcs, axis=0)
        h = gated(_swiglu_hidden(xc, wg_ref[...], wu_ref[...]), jnp.concatenate(gcs, axis=0))
        y = _dot(h, wd_ref[...]).astype(BF16)
        for hf in range(nh):
            hsl = slice(HT * hf, HT * (hf + 1))
            cols = col_scr[hsl, :]
            pick = jnp.logical_and(cols[:, GRP_LANE:GRP_LANE + 1] == gf, cols[:, RANK_LANE:RANK_LANE + 1] == c_iota)
            Pt = jnp.where(pick, 1.0, 0.0).astype(BF16)
            x2_out[hsl, :] += _dot(Pt, y[CAP * hf:CAP * (hf + 1), :])

    @pl.when(overflow)
    def _uncompacted():
        RC = min(256, TM)

        def rows(c, carry):
            rs = pl.ds(pl.multiple_of(c * RC, RC), RC)
            cols = col_scr[rs, :]
            gcols = jnp.where(cols[:, GRP_LANE:GRP_LANE + 1] == gf, cols, 0.0)
            h = gated(_swiglu_hidden(xb_scr[rs, :], wg_ref[...], wu_ref[...]), gcols)
            x2_out[rs, :] += _dot(h, wd_ref[...])
            return carry

        lax.fori_loop(0, TM // RC, rows, 0)

    @pl.when(j == pl.num_programs(1) - 1)
    def _finish():
        x2_out[...] = _layernorm(DN_ALPHA * x_ref[...] + x2_out[...], g2_ref[...], b2_ref[...])


MOE_TM = 1024
MOE_CAP = 160


def _moe_call(x2d, ew):
    N = x2d.shape[0]
    TM = min(MOE_TM, N)
    HT = TM // 2
    CAP = min(MOE_CAP, HT)
    GW = EXPERTS_PER_GROUP * D_EXPERT
    row = pl.BlockSpec((TM, D_MODEL), lambda i, j: (i, 0))
    return pl.pallas_call(
        functools.partial(_moe_kernel, TM=TM, HT=HT, CAP=CAP),
        grid=(N // TM, N_GROUPS),
        in_specs=[row, _const_spec((N_EXPERTS, D_MODEL)), _const_spec((N_EXPERTS, 1)), _const_spec((HT, HT)),
                  _const_spec((D_MODEL, D_SHARED)), _const_spec((D_MODEL, D_SHARED)), _const_spec((D_SHARED, D_MODEL)),
                  pl.BlockSpec((D_MODEL, GW), lambda i, j: (0, j)), pl.BlockSpec((D_MODEL, GW), lambda i, j: (0, j)),
                  pl.BlockSpec((GW, D_MODEL), lambda i, j: (j, 0)),
                  _const_spec((1, D_MODEL)), _const_spec((1, D_MODEL))],
        out_specs=row,
        out_shape=jax.ShapeDtypeStruct((N, D_MODEL), F32),
        scratch_shapes=[pltpu.VMEM((TM, D_MODEL), BF16), pltpu.VMEM((TM, 128), F32), pltpu.VMEM((8, TM), F32),
                        pltpu.SMEM((1,), jnp.int32)],
        compiler_params=pltpu.CompilerParams(dimension_semantics=("parallel", "arbitrary"),
                                             vmem_limit_bytes=VMEM_LIMIT),
        name="moe",
    )(x2d, ew["rwt"], ew["rb"], jnp.triu(jnp.ones((HT, HT), BF16), 1), ew["wgs"], ew["wus"], ew["wds"],
      ew["wg"], ew["wu"], ew["wd"], ew["g2"], ew["b2"])


def _rot_cols(w):
    half = w.shape[-1] // 2
    return jnp.concatenate([-w[..., half:], w[..., :half]], axis=-1)


def _prep_layer(l, w_in, b_in, q_norm_g, kv_norm_g, w_uq, w_ukv, gmlp_ln_g, gmlp_ln_b, gmlp_ws, gmlp_bs,
                w_branch, w_out, ln1_g, ln1_b, moe_w_gate, moe_w_up, moe_w_down, shared_w_gate,
                shared_w_up, shared_w_down, ln2_g, ln2_b):
    wi = w_in[l]
    bi = b_in[l][None, :]

    def proj_cols(m):
        rows = m.shape[0]
        zero = lambda n: jnp.zeros((rows, n), m.dtype)
        kr = m[:, O_KR:O_MQ]
        kr128 = jnp.concatenate([zero(KR_LANE), kr, zero(HEAD_PAD - KR_LANE - ROPE_DIM)], axis=1)
        krr128 = jnp.concatenate([zero(KR_LANE), _rot_cols(kr), zero(HEAD_PAD - KR_LANE - ROPE_DIM)], axis=1)
        gates = jnp.concatenate([m[:, O_MI:O_MO], zero(HEAD_PAD - 2 * MLSTM_HEADS)], axis=1)
        return jnp.concatenate([m[:, O_CQ:O_KR], kr128, krr128, m[:, O_MQ:O_MI], gates, m[:, O_MO:O_GT]], axis=1)

    uq = w_uq[l].reshape(Q_RANK, MLA_HEADS, NOPE_DIM + ROPE_DIM)
    zq = lambda n: jnp.zeros((Q_RANK, MLA_HEADS, n), F32)
    pad = HEAD_PAD - NOPE_DIM - ROPE_DIM
    wq_a = jnp.concatenate([uq, zq(pad)], axis=-1).reshape(Q_RANK, QK_PAD)
    wq_b = jnp.concatenate([zq(NOPE_DIM), _rot_cols(uq[..., NOPE_DIM:]), zq(pad)], axis=-1).reshape(Q_RANK, QK_PAD)
    ukv = w_ukv[l].reshape(KV_RANK, MLA_HEADS, NOPE_DIM + V_DIM)
    wk = jnp.concatenate([ukv[..., :NOPE_DIM], jnp.zeros((KV_RANK, MLA_HEADS, HEAD_PAD - NOPE_DIM), F32)],
                         axis=-1).reshape(KV_RANK, QK_PAD)
    wv = ukv[..., NOPE_DIM:].reshape(KV_RANK, V_ALL)
    pw = dict(
        w1=proj_cols(wi).astype(BF16), b1=proj_cols(bi),
        qg=q_norm_g[l][None, :], kvg=kv_norm_g[l][None, :],
        wq=jnp.concatenate([wq_a, wq_b], axis=1).astype(BF16),
        wkv=jnp.concatenate([wk, wv], axis=1).astype(BF16),
        lng=gmlp_ln_g[l][None, :], lnb=gmlp_ln_b[l][None, :],
    )
    mw = dict(
        wgt=wi[:, O_GT:].astype(BF16), bgt=bi[:, O_GT:],
        ws_full=gmlp_ws[l], bs_full=gmlp_bs[l],
        wb=w_branch[l].astype(BF16), wo=w_out[l].astype(BF16),
        g1=ln1_g[l][None, :], b1=ln1_b[l][None, :],
    )
    cat_in = lambda we: jnp.transpose(we[l].astype(BF16), (1, 0, 2)).reshape(D_MODEL, N_EXPERTS * D_EXPERT)
    ew = dict(
        wg=cat_in(moe_w_gate), wu=cat_in(moe_w_up),
        wd=moe_w_down[l].reshape(N_EXPERTS * D_EXPERT, D_MODEL).astype(BF16),
        wgs=shared_w_gate[l].astype(BF16), wus=shared_w_up[l].astype(BF16), wds=shared_w_down[l].astype(BF16),
        g2=ln2_g[l][None, :], b2=ln2_b[l][None, :],
    )
    return pw, mw, ew


def _rope_tables(T, past, rows):
    half = ROPE_DIM // 2
    pos = (past + jnp.arange(T)).astype(F32)
    inv = ROPE_THETA ** (-jnp.arange(half, dtype=F32) / half)
    ang = pos[:, None] * inv[None, :]
    cos = jnp.cos(ang)
    sin = jnp.sin(ang)
    c2 = jnp.concatenate([cos, cos], axis=1)
    s2 = jnp.concatenate([sin, sin], axis=1)
    z = lambda n: jnp.zeros((T, n), F32)
    tail = HEAD_PAD - NOPE_DIM - ROPE_DIM
    tabs = dict(
        cq=jnp.concatenate([jnp.ones((T, NOPE_DIM), F32), c2, z(tail)], axis=1) * Q_SCALE,
        sq=jnp.concatenate([z(NOPE_DIM), s2, z(tail)], axis=1) * Q_SCALE,
        ck=jnp.concatenate([z(KR_LANE), c2, z(tail)], axis=1),
        sk=jnp.concatenate([z(KR_LANE), s2, z(tail)], axis=1),
    )
    if rows > T:
        tabs = {k: jnp.tile(v, (rows // T, 1)) for k, v in tabs.items()}
    return tabs


def _trunk(x, hist, layers, rwt, rb):
    B, T, _ = x.shape
    N = B * T
    past = 0 if hist is None else hist[0].shape[2]
    tabs = _rope_tables(T, past, min(512, N))
    L = GMLP_CHUNK if T % GMLP_CHUNK == 0 else T
    x2d = x.reshape(N, D_MODEL)
    Cs, ns, ms, gvs = [], [], [], []
    stacked = None
    for l, (pw, mw, ew) in enumerate(layers):
        KB = _attn_block(T) if hist is None else None
        outs = _proj_call(x2d, pw, tabs, T, hist is not None, KB, l, len(layers), stacked)
        q, k, v, _, _, mq, mk, mv, gates, so, gu, gv = outs[:12]
        stacked = (outs[CKV_OUT], outs[KR_OUT])
        seq = lambda a: a.reshape(B, T, a.shape[-1])
        if hist is None:
            a_out = _attn_call(seq(q), k.reshape(B, T // KB, QK_PAD, KB), seq(v))
            s0 = jnp.zeros((B, MLSTM_HEADS, MLSTM_DH, 2 * MLSTM_DH), F32)
            m0 = jnp.zeros((B, MLSTM_HEADS, 8, 128), F32)
        else:
            ckv_past, kr_past, c0, n0, m0_in = hist
            P = ckv_past.shape[2]
            krp = jnp.pad(kr_past[l], ((0, 0), (0, 0), (KR_LANE, HEAD_PAD - KR_LANE - ROPE_DIM)))
            a_out = _attn_hist_call(seq(q), ckv_past[l], krp, pw["wkv"], seq(k), seq(v))
            s0 = jnp.concatenate(
                [jnp.swapaxes(c0[l], -1, -2),
                 jnp.broadcast_to(n0[l][:, :, :, None], (B, MLSTM_HEADS, MLSTM_DH, MLSTM_DH))], axis=3)
            m0 = jnp.broadcast_to(m0_in[l][:, :, None, None], (B, MLSTM_HEADS, 8, 128))
        b_out, sfin, mfin = _mlstm_call(seq(mq), seq(mk), seq(mv), seq(gates), seq(so), s0, m0)
        mwl = dict(mw)
        mwl["ws"] = mw["ws_full"][:, :L, :L]
        mwl["bsf"] = jnp.repeat(mw["bs_full"][:, :L].T, GMLP_DG, axis=1)
        x1 = _merge_call(x2d, a_out.reshape(N, V_ALL), b_out.reshape(N, MLSTM_WIDTH), gu, gv, mwl, L)
        x2d = _moe_call(x1, dict(ew, rwt=rwt, rb=rb))
        Cs.append(jnp.swapaxes(sfin[:, :, :, :MLSTM_DH], -1, -2))
        ns.append(sfin[:, :, :, MLSTM_DH])
        ms.append(mfin[:, :, 0, 0])
        if hist is not None:
            gvs.append(outs[12].reshape(B, T, GMLP_WIDTH))
    depth = len(layers)
    res = [x2d.reshape(B, T, D_MODEL), stacked[0].reshape(depth, B, T, KV_RANK),
           stacked[1].reshape(depth, B, T, ROPE_DIM), jnp.stack(Cs), jnp.stack(ns), jnp.stack(ms)]
    if hist is not None:
        res.append(jnp.stack(gvs))
    return res


def kernel(x_prompt, x_sample, cache_mla_ckv, cache_mla_krope, state_mlstm_c, state_mlstm_n, state_mlstm_m,
           w_in, b_in, q_norm_g, kv_norm_g, w_uq, w_ukv, gmlp_ln_g, gmlp_ln_b, gmlp_ws, gmlp_bs, w_branch,
           w_out, ln1_g, ln1_b, router_w, router_b, moe_w_gate, moe_w_up, moe_w_down, shared_w_gate,
           shared_w_up, shared_w_down, ln2_g, ln2_b):
    depth = w_in.shape[0]
    layers = [_prep_layer(l, w_in, b_in, q_norm_g, kv_norm_g, w_uq, w_ukv, gmlp_ln_g, gmlp_ln_b, gmlp_ws,
                          gmlp_bs, w_branch, w_out, ln1_g, ln1_b, moe_w_gate, moe_w_up, moe_w_down,
                          shared_w_gate, shared_w_up, shared_w_down, ln2_g, ln2_b) for l in range(depth)]
    rwt = router_w.T.astype(BF16)
    rb = router_b[:, None]
    yp, p_ckv, p_kr, p_c, p_n, p_m = _trunk(x_prompt, None, layers, rwt, rb)
    ys, s_ckv, s_kr, s_c, s_n, s_m, s_gv = _trunk(
        x_sample, (cache_mla_ckv, cache_mla_krope, state_mlstm_c, state_mlstm_n, state_mlstm_m), layers, rwt, rb)
    return (yp, ys, p_ckv, p_kr, p_c, p_n, p_m, s_ckv, s_kr, s_c, s_n, s_m, s_gv)
```

```python
import functools

import jax
import jax.numpy as jnp
import numpy as np
from jax import lax
from jax.experimental import pallas as pl
from jax.experimental.pallas import tpu as pltpu

F32 = jnp.float32
BF16 = jnp.bfloat16

D_MODEL = 1024
CHUNK = 64
MLA_HEADS = 8
Q_RANK = 256
KV_RANK = 256
NOPE_DIM = 64
ROPE_DIM = 32
V_DIM = 64
ROPE_THETA = 10000.0
ATTN_SCALE = (NOPE_DIM + ROPE_DIM) ** -0.5
Q_SCALE = ATTN_SCALE * float(np.log2(np.e))
MLSTM_HEADS = 4
MLSTM_DH = 128
MLSTM_WIDTH = MLSTM_HEADS * MLSTM_DH
GMLP_GROUPS = 4
GMLP_DG = 128
GMLP_WIDTH = GMLP_GROUPS * GMLP_DG
GMLP_CHUNK = 128
N_BRANCH = 3
N_EXPERTS = 16
N_GROUPS = 4
EXPERTS_PER_GROUP = N_EXPERTS // N_GROUPS
D_EXPERT = 256
D_SHARED = 256
DEPTH = 4
DN_ALPHA = (2 * DEPTH) ** 0.25
EPS = 1e-5

HEAD_PAD = 128
QK_PAD = MLA_HEADS * HEAD_PAD
V_ALL = MLA_HEADS * V_DIM
P_CQ, P_CKV, P_KR, P_KRR, P_MQ, P_MK, P_MV, P_G, P_MO, P_GU, P_GV = (
    0, 256, 512, 640, 768, 1280, 1792, 2304, 2432, 2944, 3456)
D_PROJ = 3968
O_CQ, O_CKV, O_KR, O_MQ, O_MK, O_MV, O_MI, O_MF, O_MO, O_GU, O_GV, O_GT = (
    0, 256, 512, 544, 1056, 1568, 2080, 2084, 2088, 2600, 3112, 3624)
KR_LANE = NOPE_DIM
VMEM_LIMIT = 56 * 1024 * 1024
NEG_INF = float("-inf")


def _dot(a, b):
    return jnp.dot(a, b, preferred_element_type=F32)


def _dot_nt(a, b):
    return lax.dot_general(a, b, (((1,), (1,)), ((), ())), preferred_element_type=F32)


def _const_spec(shape):
    nd = len(shape)
    return pl.BlockSpec(shape, lambda *_: (0,) * nd, pipeline_mode=pl.Buffered(1))


def _layernorm(x, g, b):
    mu = jnp.mean(x, axis=-1, keepdims=True)
    xc = x - mu
    var = jnp.mean(xc * xc, axis=-1, keepdims=True)
    return xc * lax.rsqrt(var + EPS) * g + b


def _rmsnorm(x, g):
    return x * lax.rsqrt(jnp.mean(x * x, axis=-1, keepdims=True) + EPS) * g


def _gelu(x):
    return 0.5 * x * (1.0 + jnp.tanh(np.sqrt(2.0 / np.pi).astype(np.float32) * (x + 0.044715 * (x * x * x))))


def _sigmoid(x):
    return 1.0 / (1.0 + jnp.exp(-x))


def _log_sigmoid(x):
    return jnp.minimum(x, 0.0) - jnp.log(1.0 + jnp.exp(-jnp.abs(x)))


PROJ_N_IN = 13


def _proj_kernel(*refs, key_block, n_carried):
    (x_ref, w1_ref, b1_ref, qg_ref, kvg_ref, wq_ref, wkv_ref, cq_ref, sq_ref, ck_ref, sk_ref,
     lng_ref, lnb_ref) = refs[:PROJ_N_IN]
    (q_out, k_out, v_out, ckv_out, kr_out, mq_out, mk_out, mv_out, g_out, so_out, gu_out,
     gv_out, *rest) = refs[PROJ_N_IN + n_carried:]
    xb = x_ref[...].astype(BF16)

    def z(lo, hi):
        return _dot(xb, w1_ref[:, lo:hi]) + b1_ref[:, lo:hi]


    zu = z(P_GU, D_PROJ)
    gu_out[...] = _gelu(zu[:, :GMLP_WIDTH]).astype(BF16)
    vrows = _layernorm(_gelu(zu[:, GMLP_WIDTH:]), lng_ref[...], lnb_ref[...])
    gv_out[...] = vrows.astype(BF16)
    if rest:
        rest[0][...] = vrows
    so_out[...] = _sigmoid(z(P_MO, P_GU)).astype(BF16)

    zc = z(P_CQ, P_MQ)
    cqn = _rmsnorm(zc[:, P_CQ:P_CKV], qg_ref[...])
    ckvn = _rmsnorm(zc[:, P_CKV:P_KR], kvg_ref[...])
    kr = zc[:, P_KR:P_KRR] * ck_ref[...] + zc[:, P_KRR:P_MQ] * sk_ref[...]
    ckv_out[...] = ckvn
    kr_out[...] = kr[:, KR_LANE:KR_LANE + ROPE_DIM]
    qq = _dot(cqn.astype(BF16), wq_ref[...])
    cos8 = jnp.concatenate([cq_ref[...]] * MLA_HEADS, axis=1)
    sin8 = jnp.concatenate([sq_ref[...]] * MLA_HEADS, axis=1)
    q_out[...] = (qq[:, :QK_PAD] * cos8 + qq[:, QK_PAD:] * sin8).astype(BF16)
    kk = _dot(ckvn.astype(BF16), wkv_ref[...])
    kr8 = jnp.concatenate([kr] * MLA_HEADS, axis=1)
    kval = kk[:, :QK_PAD] + kr8
    if key_block is None:
        k_out[...] = kval.astype(BF16)
    else:
        for c in range(kval.shape[0] // key_block):
            k_out[c] = kval[key_block * c:key_block * (c + 1), :].T.astype(BF16)
    v_out[...] = kk[:, QK_PAD:].astype(BF16)

    zm = z(P_MQ, P_MO)
    mq_out[...] = zm[:, 0:512].astype(BF16)
    mk_out[...] = (zm[:, 512:1024] * (MLSTM_DH ** -0.5)).astype(BF16)
    mv_out[...] = zm[:, 1024:1536].astype(BF16)
    zg = zm[:, 1536:1664]
    lane = lax.broadcasted_iota(jnp.int32, zg.shape, 1)
    g_out[...] = jnp.where(lane < MLSTM_HEADS, zg, _log_sigmoid(zg))


CKV_OUT, KR_OUT = 3, 4


def _proj_call(x2d, pw, tabs, T, want_vrows, key_block, layer, depth, carried):
    N = x2d.shape[0]
    TM = min(512, N)
    nt = max(1, T // TM)
    grid = (N // TM,)
    row = lambda w: pl.BlockSpec((TM, w), lambda i: (i, 0))
    tab = pl.BlockSpec((TM, HEAD_PAD), lambda i: (i % nt, 0))
    in_specs = [
        row(D_MODEL),
        _const_spec((D_MODEL, D_PROJ)), _const_spec((1, D_PROJ)),
        _const_spec((1, Q_RANK)), _const_spec((1, KV_RANK)),
        _const_spec((Q_RANK, 2 * QK_PAD)), _const_spec((KV_RANK, QK_PAD + V_ALL)),
        tab, tab, tab, tab,
        _const_spec((1, GMLP_WIDTH)), _const_spec((1, GMLP_WIDTH)),
    ]
    out_shapes = [
        ((N, QK_PAD), BF16), ((N, QK_PAD), BF16), ((N, V_ALL), BF16), ((N, KV_RANK), F32),
        ((N, HEAD_PAD), F32), ((N, MLSTM_WIDTH), BF16), ((N, MLSTM_WIDTH), BF16), ((N, MLSTM_WIDTH), BF16),
        ((N, HEAD_PAD), F32), ((N, MLSTM_WIDTH), BF16), ((N, GMLP_WIDTH), BF16), ((N, GMLP_WIDTH), BF16),
    ]
    if want_vrows:
        out_shapes.append(((N, GMLP_WIDTH), F32))
    out_specs = [row(s[1]) for s, _ in out_shapes]
    if key_block is not None:
        out_shapes[1] = ((N // key_block, QK_PAD, key_block), BF16)
        out_specs[1] = pl.BlockSpec((TM // key_block, QK_PAD, key_block), lambda i: (i, 0, 0))
    for pos, width in ((CKV_OUT, KV_RANK), (KR_OUT, ROPE_DIM)):
        out_shapes[pos] = ((depth, N, width), F32)
        out_specs[pos] = pl.BlockSpec((None, TM, width), lambda i: (layer, i, 0))
    args = [x2d, pw["w1"], pw["b1"], pw["qg"], pw["kvg"], pw["wq"], pw["wkv"],
            tabs["cq"], tabs["sq"], tabs["ck"], tabs["sk"], pw["lng"], pw["lnb"]]
    aliases = {}
    if carried is not None:
        in_specs += [pl.BlockSpec(memory_space=pl.ANY)] * 2
        aliases = {PROJ_N_IN: CKV_OUT, PROJ_N_IN + 1: KR_OUT}
        args += list(carried)
    return pl.pallas_call(
        functools.partial(_proj_kernel, key_block=key_block, n_carried=len(aliases)),
        grid=grid,
        in_specs=in_specs,
        out_specs=out_specs,
        out_shape=[jax.ShapeDtypeStruct(s, d) for s, d in out_shapes],
        input_output_aliases=aliases,
        compiler_params=pltpu.CompilerParams(dimension_semantics=("parallel",), vmem_limit_bytes=VMEM_LIMIT),
        name="proj",
    )(*args)


def _attn_kernel(q_ref, kt_ref, v_ref, o_ref, mx_scr, acc_scr, s_scr, *, TQ):
    i = pl.program_id(1)
    ones_blk = jnp.ones((TQ, HEAD_PAD), BF16)
    rc = lax.broadcasted_iota(jnp.int32, (TQ, TQ), 0) // CHUNK
    cc = lax.broadcasted_iota(jnp.int32, (TQ, TQ), 1) // CHUNK
    visible = cc <= rc
    lane = lax.broadcasted_iota(jnp.int32, (TQ, HEAD_PAD), 1)
    nlane = TQ // 128

    def fold(a, op):
        r = a[:, 0:128]
        for t in range(1, nlane):
            r = op(r, a[:, 128 * t:128 * (t + 1)])
        return r

    hsl = lambda h: slice(HEAD_PAD * h, HEAD_PAD * (h + 1))
    heads = range(MLA_HEADS)

    def sweep_max(j, masked, first):
        ss = [_dot(q_ref[0, :, hsl(h)], kt_ref[0, j, hsl(h), :]) for h in heads]
        if masked:
            ss = [jnp.where(visible, s, NEG_INF) for s in ss]
        for h in heads:
            s_scr[h, j] = ss[h]
        for h in heads:
            mx = fold(ss[h], jnp.maximum)
            mx_scr[h] = mx if first else jnp.maximum(mx_scr[h], mx)

    def sweep_pv(j, first):
        r0 = pl.multiple_of(j * TQ, TQ)
        ps = [jnp.exp2(s_scr[h, j] - jnp.concatenate([mx_scr[h]] * nlane, axis=1)).astype(BF16) for h in heads]
        pvs = []
        for p in range(MLA_HEADS // 2):
            vs = slice(HEAD_PAD * p, HEAD_PAD * (p + 1))
            vext = jnp.concatenate([v_ref[0, pl.ds(r0, TQ), vs], ones_blk], axis=1)
            pv2 = _dot(jnp.concatenate([ps[2 * p], ps[2 * p + 1]], axis=0), vext)
            pvs += [pv2[:TQ], pv2[TQ:]]
        for h in heads:
            acc_scr[h] = pvs[h] if first else acc_scr[h] + pvs[h]

    def body_max(j, c):
        sweep_max(j, False, False)
        return c

    def body_pv(j, c):
        sweep_pv(j, False)
        return c

    sweep_max(i, True, True)
    lax.fori_loop(0, i, body_max, 0)
    for h in heads:
        mx_scr[h] = jnp.broadcast_to(jnp.max(mx_scr[h], axis=-1, keepdims=True), (TQ, 128))
    sweep_pv(i, True)
    lax.fori_loop(0, i, body_pv, 0)
    for p in range(MLA_HEADS // 2):
        a0 = acc_scr[2 * p]
        a1 = acc_scr[2 * p + 1]
        o0 = a0[:, :HEAD_PAD] / a0[:, HEAD_PAD:]
        o1 = a1[:, :HEAD_PAD] / a1[:, HEAD_PAD:]
        o_ref[0, :, HEAD_PAD * p:HEAD_PAD * (p + 1)] = jnp.where(lane < V_DIM, o0, o1).astype(BF16)


def _attn_block(T):
    return min(256, T)


def _attn_call(q, kt, v):
    B, T, _ = q.shape
    TQ = _attn_block(T)
    nq = T // TQ
    return pl.pallas_call(
        functools.partial(_attn_kernel, TQ=TQ),
        grid=(B, nq),
        in_specs=[pl.BlockSpec((1, TQ, QK_PAD), lambda b, i: (b, i, 0)),
                  pl.BlockSpec((1, nq, QK_PAD, TQ), lambda b, i: (b, 0, 0, 0)),
                  pl.BlockSpec((1, T, V_ALL), lambda b, i: (b, 0, 0))],
        out_specs=pl.BlockSpec((1, TQ, V_ALL), lambda b, i: (b, i, 0)),
        out_shape=jax.ShapeDtypeStruct((B, T, V_ALL), BF16),
        scratch_shapes=[pltpu.VMEM((MLA_HEADS, TQ, 128), F32), pltpu.VMEM((MLA_HEADS, TQ, 2 * HEAD_PAD), F32),
                        pltpu.VMEM((MLA_HEADS, nq, TQ, TQ), F32)],
        compiler_params=pltpu.CompilerParams(dimension_semantics=("parallel", "arbitrary"),
                                             vmem_limit_bytes=VMEM_LIMIT),
        name="attn",
    )(q, kt, v)


def _attn_hist_kernel(q_ref, cp_ref, krp_ref, wkv_ref, kn_ref, vn_ref, o_ref, kp_ref, vp_ref):
    T = q_ref.shape[1]
    lane = lax.broadcasted_iota(jnp.int32, (T, HEAD_PAD), 1)
    kk = _dot(cp_ref[0].astype(BF16), wkv_ref[...])
    kp_ref[0] = (kk[:, :QK_PAD] + jnp.concatenate([krp_ref[0]] * MLA_HEADS, axis=1)).astype(BF16)
    vp_ref[0] = kk[:, QK_PAD:].astype(BF16)
    for p in range(MLA_HEADS // 2):
        pair = []
        vs = slice(HEAD_PAD * p, HEAD_PAD * (p + 1))
        for hh in range(2):
            h = 2 * p + hh
            hs = slice(HEAD_PAD * h, HEAD_PAD * (h + 1))
            qh = q_ref[0, :, hs]
            s1 = _dot_nt(qh, kp_ref[0, :, hs])
            s2 = _dot_nt(qh, kn_ref[0, :, hs])
            m = jnp.maximum(jnp.max(s1, axis=-1, keepdims=True), jnp.max(s2, axis=-1, keepdims=True))
            p1 = jnp.exp2(s1 - m)
            p2 = jnp.exp2(s2 - m)
            l = jnp.sum(p1, axis=-1, keepdims=True) + jnp.sum(p2, axis=-1, keepdims=True)
            acc = _dot(p1.astype(BF16), vp_ref[0, :, vs]) + _dot(p2.astype(BF16), vn_ref[0, :, vs])
            pair.append(acc / l)
        o_ref[0, :, vs] = jnp.where(lane < V_DIM, pair[0], pair[1]).astype(BF16)


def _attn_hist_call(q, ckv_past, krp, wkv, kn, vn):
    B, T, _ = q.shape
    P = ckv_past.shape[1]
    blk = lambda t, w: pl.BlockSpec((1, t, w), lambda b: (b, 0, 0))
    return pl.pallas_call(
        _attn_hist_kernel,
        grid=(B,),
        in_specs=[blk(T, QK_PAD), blk(P, KV_RANK), blk(P, HEAD_PAD), _const_spec((KV_RANK, QK_PAD + V_ALL)),
                  blk(T, QK_PAD), blk(T, V_ALL)],
        out_specs=blk(T, V_ALL),
        out_shape=jax.ShapeDtypeStruct((B, T, V_ALL), BF16),
        scratch_shapes=[pltpu.VMEM((1, P, QK_PAD), BF16), pltpu.VMEM((1, P, V_ALL), BF16)],
        compiler_params=pltpu.CompilerParams(dimension_semantics=("parallel",), vmem_limit_bytes=VMEM_LIMIT),
        name="attn_hist",
    )(q, ckv_past, krp, wkv, kn, vn)


MLSTM_BB = 4


def _mlstm_kernel(q_ref, k_ref, v_ref, g_ref, so_ref, s0_ref, m0_ref, h_out, sfin_out, mfin_out,
                  s_scr, m_scr, *, TL, BB):
    t = pl.program_id(1)
    L = CHUNK
    DH = MLSTM_DH

    @pl.when(t == 0)
    def _():
        s_scr[...] = s0_ref[...]
        m_scr[...] = m0_ref[...]

    r_i = lax.broadcasted_iota(jnp.int32, (L, L), 0)
    c_i = lax.broadcasted_iota(jnp.int32, (L, L), 1)
    causal = c_i <= r_i
    tri = jnp.where(causal, 1.0, 0.0).astype(BF16)
    lane = lax.broadcasted_iota(jnp.int32, (L, 128), 1)
    ones_blk = jnp.ones((L, DH), BF16)

    def rep(col):
        return jnp.broadcast_to(col, (L, 128))

    def chunk(c, carry):
        rows = pl.ds(pl.multiple_of(c * L, L), L)
        pairs = [(bb, h) for bb in range(BB) for h in range(MLSTM_HEADS)]
        hsl = lambda h: slice(DH * h, DH * (h + 1))
        Gs, cums, VTs = [], [], []
        for bb in range(BB):
            G = g_ref[bb, rows, :]
            g_hi = G.astype(BF16)
            g_r1 = G - g_hi.astype(F32)
            g_mid = g_r1.astype(BF16)
            g_lo = (g_r1 - g_mid.astype(F32)).astype(BF16)
            cum = _dot(tri, g_hi) + _dot(tri, g_mid) + _dot(tri, g_lo)
            Gs.append(G)
            cums.append(cum)
            VTs.append(jnp.where(lane < MLSTM_HEADS, G, cum).T)
        qk_raw = {p: _dot_nt(q_ref[p[0], rows, hsl(p[1])], k_ref[p[0], rows, hsl(p[1])]) for p in pairs}
        sq = {p: _dot(q_ref[p[0], rows, hsl(p[1])], s_scr[p[0], p[1]].astype(BF16)) for p in pairs}
        gate = {}
        for bb, h in pairs:
            b_t = rep(jnp.sum(jnp.where(lane == MLSTM_HEADS + h, cums[bb], 0.0), axis=1, keepdims=True))
            ig_t = rep(jnp.sum(jnp.where(lane == h, Gs[bb], 0.0), axis=1, keepdims=True))
            brow = VTs[bb][MLSTM_HEADS + h:MLSTM_HEADS + h + 1, :]
            igrow = VTs[bb][h:h + 1, :]
            d = jnp.where(causal, b_t[:, :L] - brow + igrow, NEG_INF)
            gate[bb, h] = (b_t, ig_t, d, rep(jnp.max(d, axis=1, keepdims=True)))
        stab = {}
        for bb, h in pairs:
            b_t, ig_t, d, dmax = gate[bb, h]
            m_prev = m_scr[bb, h, 0:1, :]
            g_t = b_t + m_prev
            mt = jnp.maximum(g_t, dmax)
            stab[bb, h] = (m_prev, mt, jnp.exp(g_t - mt), jnp.exp(d - mt[:, :L]))
        for bb, h in pairs:
            m_prev, mt, inter, w = stab[bb, h]
            vext = jnp.concatenate([v_ref[bb, rows, hsl(h)], ones_blk], axis=1)
            intra = _dot((qk_raw[bb, h] * w).astype(BF16), vext)
            num = inter * sq[bb, h][:, :DH] + intra[:, :DH]
            den = jnp.maximum(jnp.abs(inter * sq[bb, h][:, DH:] + intra[:, DH:]), jnp.exp(-mt))
            so = so_ref[bb, rows, hsl(h)].astype(F32)
            h_out[bb, rows, hsl(h)] = (so * (num / den)).astype(BF16)
        for bb, h in pairs:
            b_t, ig_t, _, _ = gate[bb, h]
            m_prev, mt, _, _ = stab[bb, h]
            mL = mt[L - 1:L, :]
            bL = b_t[L - 1:L, :]
            ws_t = jnp.exp(bL - b_t + ig_t - mL)
            decay = jnp.exp(bL + m_prev - mL)
            wv = jnp.concatenate([ws_t * v_ref[bb, rows, hsl(h)].astype(F32), ws_t], axis=1).astype(BF16)
            kT = k_ref[bb, rows, hsl(h)].astype(F32).T.astype(BF16)
            s_scr[bb, h] = jnp.concatenate([decay, decay], axis=1) * s_scr[bb, h] + _dot(kT, wv)
            m_scr[bb, h] = jnp.broadcast_to(mL, (8, 128))
        return carry

    lax.fori_loop(0, TL // L, chunk, 0)

    @pl.when(t == pl.num_programs(1) - 1)
    def _():
        sfin_out[...] = s_scr[...]
        mfin_out[...] = m_scr[...]


def _mlstm_call(mq, mk, mv, gates, so, s0, m0):
    B, T, _ = mq.shape
    TL = min(512, T)
    BB = int(np.gcd(MLSTM_BB, B))
    seq = lambda w: pl.BlockSpec((BB, TL, w), lambda b, t: (b, t, 0))
    st = pl.BlockSpec((BB, MLSTM_HEADS, MLSTM_DH, 2 * MLSTM_DH), lambda b, t: (b, 0, 0, 0))
    mst = pl.BlockSpec((BB, MLSTM_HEADS, 8, 128), lambda b, t: (b, 0, 0, 0))
    return pl.pallas_call(
        functools.partial(_mlstm_kernel, TL=TL, BB=BB),
        grid=(B // BB, T // TL),
        in_specs=[seq(MLSTM_WIDTH), seq(MLSTM_WIDTH), seq(MLSTM_WIDTH), seq(128), seq(MLSTM_WIDTH), st, mst],
        out_specs=[seq(MLSTM_WIDTH), st, mst],
        out_shape=[jax.ShapeDtypeStruct((B, T, MLSTM_WIDTH), BF16),
                   jax.ShapeDtypeStruct((B, MLSTM_HEADS, MLSTM_DH, 2 * MLSTM_DH), F32),
                   jax.ShapeDtypeStruct((B, MLSTM_HEADS, 8, 128), F32)],
        scratch_shapes=[pltpu.VMEM((BB, MLSTM_HEADS, MLSTM_DH, 2 * MLSTM_DH), F32),
                        pltpu.VMEM((BB, MLSTM_HEADS, 8, 128), F32)],
        compiler_params=pltpu.CompilerParams(dimension_semantics=("parallel", "arbitrary"),
                                             vmem_limit_bytes=VMEM_LIMIT),
        name="mlstm",
    )(mq, mk, mv, gates, so, s0, m0)


def _merge_kernel(x_ref, a_ref, b_ref, gu_ref, gv_ref, wgt_ref, bgt_ref, ws_ref, bsf_ref, wb_ref, wo_ref,
                  g1_ref, b1_ref, x1_out, c_scr, *, L, TM):
    x = x_ref[...]
    xb = x.astype(BF16)
    r_i = lax.broadcasted_iota(jnp.int32, (L, L), 0)
    c_i = lax.broadcasted_iota(jnp.int32, (L, L), 1)
    for g in range(GMLP_GROUPS):
        gs = slice(GMLP_DG * g, GMLP_DG * (g + 1))
        wsg = jnp.where(c_i <= r_i, ws_ref[g], 0.0).astype(BF16)
        for c in range(TM // L):
            rs = slice(L * c, L * (c + 1))
            sp = _dot(wsg, gv_ref[rs, gs]) + bsf_ref[:, gs]
            c_scr[rs, gs] = (gu_ref[rs, gs].astype(F32) * sp).astype(BF16)
    merged = None
    for kb, br in enumerate((a_ref, b_ref, c_scr)):
        cs = slice(D_MODEL * kb, D_MODEL * (kb + 1))
        gate = _sigmoid(_dot(xb, wgt_ref[:, cs]) + bgt_ref[:, cs])
        term = gate * _dot(br[...], wb_ref[kb])
        merged = term if merged is None else merged + term
    mb = merged.astype(BF16)
    RH = min(256, TM)
    for r in range(TM // RH):
        rs = slice(RH * r, RH * (r + 1))
        y = _dot(mb[rs, :], wo_ref[...])
        x1_out[rs, :] = _layernorm(DN_ALPHA * x[rs, :] + y, g1_ref[...], b1_ref[...])


def _merge_call(x2d, a2d, b2d, gu, gv, mw, L):
    N = x2d.shape[0]
    TM = min(512, N)
    row = lambda w: pl.BlockSpec((TM, w), lambda i: (i, 0))
    return pl.pallas_call(
        functools.partial(_merge_kernel, L=L, TM=TM),
        grid=(N // TM,),
        in_specs=[row(D_MODEL), row(V_ALL), row(MLSTM_WIDTH), row(GMLP_WIDTH), row(GMLP_WIDTH),
                  _const_spec((D_MODEL, N_BRANCH * D_MODEL)), _const_spec((1, N_BRANCH * D_MODEL)),
                  _const_spec((GMLP_GROUPS, L, L)), _const_spec((L, GMLP_WIDTH)),
                  _const_spec((N_BRANCH, 512, D_MODEL)), _const_spec((D_MODEL, D_MODEL)),
                  _const_spec((1, D_MODEL)), _const_spec((1, D_MODEL))],
        out_specs=row(D_MODEL),
        out_shape=jax.ShapeDtypeStruct((N, D_MODEL), F32),
        scratch_shapes=[pltpu.VMEM((TM, GMLP_WIDTH), BF16)],
        compiler_params=pltpu.CompilerParams(dimension_semantics=("parallel",), vmem_limit_bytes=VMEM_LIMIT),
        name="merge",
    )(x2d, a2d, b2d, gu, gv, mw["wgt"], mw["bgt"], mw["ws"], mw["bsf"], mw["wb"], mw["wo"], mw["g1"], mw["b1"])


def _route_rows(logits_t, rb_ref):
    s = _sigmoid(logits_t)
    sb = s + rb_ref[...]
    rows = [sb[e:e + 1, :] for e in range(N_EXPERTS)]
    srow = [s[e:e + 1, :] for e in range(N_EXPERTS)]
    gscore = []
    for g in range(N_GROUPS):
        mem = rows[EXPERTS_PER_GROUP * g:EXPERTS_PER_GROUP * (g + 1)]
        best = None
        for a in range(EXPERTS_PER_GROUP):
            for b in range(a + 1, EXPERTS_PER_GROUP):
                pr = mem[a] + mem[b]
                best = pr if best is None else jnp.maximum(best, pr)
        gscore.append(best)
    gmax = functools.reduce(jnp.maximum, gscore)
    taken = None
    gsel = []
    for g in range(N_GROUPS):
        hit = gscore[g] == gmax
        if taken is None:
            gsel.append(hit)
            taken = hit
        else:
            gsel.append(jnp.logical_and(hit, jnp.logical_not(taken)))
            taken = jnp.logical_or(taken, hit)
    sel_w = []
    for e in range(N_EXPERTS):
        g = e // EXPERTS_PER_GROUP
        rank = None
        for o in range(EXPERTS_PER_GROUP * g, EXPERTS_PER_GROUP * (g + 1)):
            if o == e:
                continue
            ahead = (rows[o] >= rows[e]) if o < e else (rows[o] > rows[e])
            ahead = jnp.where(ahead, 1.0, 0.0)
            rank = ahead if rank is None else rank + ahead
        chosen = jnp.logical_and(gsel[g], rank < 1.5)
        sel_w.append(jnp.where(chosen, srow[e], 0.0))
    den = functools.reduce(jnp.add, sel_w)
    gate_rows = [w_ / den for w_ in sel_w]
    return gsel, gate_rows


def _swiglu_hidden(xb, wg, wu):
    hg = _dot(xb, wg)
    return hg * _sigmoid(hg) * _dot(xb, wu)


def _moe_kernel(x_ref, rw_ref, rb_ref, tri_ref, wgs_ref, wus_ref, wds_ref, wg_ref, wu_ref, wd_ref,
                g2_ref, b2_ref, x2_out, xb_scr, col_scr, row_scr, flag_ref, *, TM, HT, CAP):
    j = pl.program_id(1)
    nh = TM // HT
    gate_lanes = EXPERTS_PER_GROUP
    RANK_LANE, GRP_LANE = gate_lanes, gate_lanes + 1

    @pl.when(j == 0)
    def _route():
        xb = x_ref[...].astype(BF16)
        xb_scr[...] = xb
        gsel, gate_rows = _route_rows(_dot_nt(rw_ref[...], xb), rb_ref)
        isg = [jnp.where(m, 1.0, 0.0) for m in gsel]
        grp = functools.reduce(jnp.add, [float(g) * isg[g] for g in range(N_GROUPS)])
        g4 = [functools.reduce(jnp.add, [isg[g] * gate_rows[EXPERTS_PER_GROUP * g + e] for g in range(N_GROUPS)])
              for e in range(EXPERTS_PER_GROUP)]
        ranks = []
        worst = None
        for hf in range(nh):
            hsl = slice(HT * hf, HT * (hf + 1))
            m8 = jnp.concatenate([isg[g][:, hsl] for g in range(N_GROUPS)]
                                 + [jnp.zeros((8 - N_GROUPS, HT), F32)], axis=0).astype(BF16)
            before = _dot(m8, tri_ref[...])
            ranks.append(functools.reduce(jnp.add, [isg[g][:, hsl] * before[g:g + 1, :] for g in range(N_GROUPS)]))
            for g in range(N_GROUPS):
                cnt = jnp.sum(isg[g][:, hsl])
                worst = cnt if worst is None else jnp.maximum(worst, cnt)
        rank = jnp.concatenate(ranks, axis=1)
        flag_ref[0] = (worst > float(CAP)).astype(jnp.int32)
        row_scr[...] = jnp.concatenate([rank, grp, jnp.zeros((6, TM), F32)], axis=0)
        col_scr[...] = jnp.concatenate(g4 + [rank, grp, jnp.zeros((128 - gate_lanes - 2, TM), F32)], axis=0).T
        x2_out[...] = _dot(_swiglu_hidden(xb, wgs_ref[...], wus_ref[...]).astype(BF16), wds_ref[...])

    gf = j.astype(F32)
    overflow = flag_ref[0] != 0

    def gated(h, gcols):
        parts = [h[:, D_EXPERT * e:D_EXPERT * (e + 1)] * gcols[:, e:e + 1] for e in range(EXPERTS_PER_GROUP)]
        return jnp.concatenate(parts, axis=1).astype(BF16)

    @pl.when(jnp.logical_not(overflow))
    def _compact():
        r_iota = lax.broadcasted_iota(jnp.int32, (CAP, HT), 0).astype(F32)
        c_iota = lax.broadcasted_iota(jnp.int32, (HT, CAP), 1).astype(F32)
        xcs, gcs = [], []
        for hf in range(nh):
            hsl = slice(HT * hf, HT * (hf + 1))
            pick = jnp.logical_and(row_scr[1:2, hsl] == gf, row_scr[0:1, hsl] == r_iota)
            P = jnp.where(pick, 1.0, 0.0).astype(BF16)
            xcs.append(_dot(P, xb_scr[hsl, :]).astype(BF16))
            cols = col_scr[hsl, :]
            c_hi = cols.astype(BF16)
            c_lo = (cols - c_hi.astype(F32)).astype(BF16)
            gcs.append(_dot(P, c_hi) + _dot(P, c_lo))
        xc = jnp.concatenate(xcs, axis=0)
        h = gated(_swiglu_hidden(xc, wg_ref[...], wu_ref[...]), jnp.concatenate(gcs, axis=0))
        y = _dot(h, wd_ref[...]).astype(BF16)
        for hf in range(nh):
            hsl = slice(HT * hf, HT * (hf + 1))
            cols = col_scr[hsl, :]
            pick = jnp.logical_and(cols[:, GRP_LANE:GRP_LANE + 1] == gf, cols[:, RANK_LANE:RANK_LANE + 1] == c_iota)
            Pt = jnp.where(pick, 1.0, 0.0).astype(BF16)
            x2_out[hsl, :] += _dot(Pt, y[CAP * hf:CAP * (hf + 1), :])

    @pl.when(overflow)
    def _uncompacted():
        RC = min(256, TM)

        def rows(c, carry):
            rs = pl.ds(pl.multiple_of(c * RC, RC), RC)
            cols = col_scr[rs, :]
            gcols = jnp.where(cols[:, GRP_LANE:GRP_LANE + 1] == gf, cols, 0.0)
            h = gated(_swiglu_hidden(xb_scr[rs, :], wg_ref[...], wu_ref[...]), gcols)
            x2_out[rs, :] += _dot(h, wd_ref[...])
            return carry

        lax.fori_loop(0, TM // RC, rows, 0)

    @pl.when(j == pl.num_programs(1) - 1)
    def _finish():
        x2_out[...] = _layernorm(DN_ALPHA * x_ref[...] + x2_out[...], g2_ref[...], b2_ref[...])


MOE_TM = 1024
MOE_CAP = 160


def _moe_call(x2d, ew):
    N = x2d.shape[0]
    TM = min(MOE_TM, N)
    HT = TM // 2
    CAP = min(MOE_CAP, HT)
    GW = EXPERTS_PER_GROUP * D_EXPERT
    row = pl.BlockSpec((TM, D_MODEL), lambda i, j: (i, 0))
    return pl.pallas_call(
        functools.partial(_moe_kernel, TM=TM, HT=HT, CAP=CAP),
        grid=(N // TM, N_GROUPS),
        in_specs=[row, _const_spec((N_EXPERTS, D_MODEL)), _const_spec((N_EXPERTS, 1)), _const_spec((HT, HT)),
                  _const_spec((D_MODEL, D_SHARED)), _const_spec((D_MODEL, D_SHARED)), _const_spec((D_SHARED, D_MODEL)),
                  pl.BlockSpec((D_MODEL, GW), lambda i, j: (0, j)), pl.BlockSpec((D_MODEL, GW), lambda i, j: (0, j)),
                  pl.BlockSpec((GW, D_MODEL), lambda i, j: (j, 0)),
                  _const_spec((1, D_MODEL)), _const_spec((1, D_MODEL))],
        out_specs=row,
        out_shape=jax.ShapeDtypeStruct((N, D_MODEL), F32),
        scratch_shapes=[pltpu.VMEM((TM, D_MODEL), BF16), pltpu.VMEM((TM, 128), F32), pltpu.VMEM((8, TM), F32),
                        pltpu.SMEM((1,), jnp.int32)],
        compiler_params=pltpu.CompilerParams(dimension_semantics=("parallel", "arbitrary"),
                                             vmem_limit_bytes=VMEM_LIMIT),
        name="moe",
    )(x2d, ew["rwt"], ew["rb"], jnp.triu(jnp.ones((HT, HT), BF16), 1), ew["wgs"], ew["wus"], ew["wds"],
      ew["wg"], ew["wu"], ew["wd"], ew["g2"], ew["b2"])


def _rot_cols(w):
    half = w.shape[-1] // 2
    return jnp.concatenate([-w[..., half:], w[..., :half]], axis=-1)


def _prep_layer(l, w_in, b_in, q_norm_g, kv_norm_g, w_uq, w_ukv, gmlp_ln_g, gmlp_ln_b, gmlp_ws, gmlp_bs,
                w_branch, w_out, ln1_g, ln1_b, moe_w_gate, moe_w_up, moe_w_down, shared_w_gate,
                shared_w_up, shared_w_down, ln2_g, ln2_b):
    wi = w_in[l]
    bi = b_in[l][None, :]

    def proj_cols(m):
        rows = m.shape[0]
        zero = lambda n: jnp.zeros((rows, n), m.dtype)
        kr = m[:, O_KR:O_MQ]
        kr128 = jnp.concatenate([zero(KR_LANE), kr, zero(HEAD_PAD - KR_LANE - ROPE_DIM)], axis=1)
        krr128 = jnp.concatenate([zero(KR_LANE), _rot_cols(kr), zero(HEAD_PAD - KR_LANE - ROPE_DIM)], axis=1)
        gates = jnp.concatenate([m[:, O_MI:O_MO], zero(HEAD_PAD - 2 * MLSTM_HEADS)], axis=1)
        return jnp.concatenate([m[:, O_CQ:O_KR], kr128, krr128, m[:, O_MQ:O_MI], gates, m[:, O_MO:O_GT]], axis=1)

    uq = w_uq[l].reshape(Q_RANK, MLA_HEADS, NOPE_DIM + ROPE_DIM)
    zq = lambda n: jnp.zeros((Q_RANK, MLA_HEADS, n), F32)
    pad = HEAD_PAD - NOPE_DIM - ROPE_DIM
    wq_a = jnp.concatenate([uq, zq(pad)], axis=-1).reshape(Q_RANK, QK_PAD)
    wq_b = jnp.concatenate([zq(NOPE_DIM), _rot_cols(uq[..., NOPE_DIM:]), zq(pad)], axis=-1).reshape(Q_RANK, QK_PAD)
    ukv = w_ukv[l].reshape(KV_RANK, MLA_HEADS, NOPE_DIM + V_DIM)
    wk = jnp.concatenate([ukv[..., :NOPE_DIM], jnp.zeros((KV_RANK, MLA_HEADS, HEAD_PAD - NOPE_DIM), F32)],
                         axis=-1).reshape(KV_RANK, QK_PAD)
    wv = ukv[..., NOPE_DIM:].reshape(KV_RANK, V_ALL)
    pw = dict(
        w1=proj_cols(wi).astype(BF16), b1=proj_cols(bi),
        qg=q_norm_g[l][None, :], kvg=kv_norm_g[l][None, :],
        wq=jnp.concatenate([wq_a, wq_b], axis=1).astype(BF16),
        wkv=jnp.concatenate([wk, wv], axis=1).astype(BF16),
        lng=gmlp_ln_g[l][None, :], lnb=gmlp_ln_b[l][None, :],
    )
    mw = dict(
        wgt=wi[:, O_GT:].astype(BF16), bgt=bi[:, O_GT:],
        ws_full=gmlp_ws[l], bs_full=gmlp_bs[l],
        wb=w_branch[l].astype(BF16), wo=w_out[l].astype(BF16),
        g1=ln1_g[l][None, :], b1=ln1_b[l][None, :],
    )
    cat_in = lambda we: jnp.transpose(we[l].astype(BF16), (1, 0, 2)).reshape(D_MODEL, N_EXPERTS * D_EXPERT)
    ew = dict(
        wg=cat_in(moe_w_gate), wu=cat_in(moe_w_up),
        wd=moe_w_down[l].reshape(N_EXPERTS * D_EXPERT, D_MODEL).astype(BF16),
        wgs=shared_w_gate[l].astype(BF16), wus=shared_w_up[l].astype(BF16), wds=shared_w_down[l].astype(BF16),
        g2=ln2_g[l][None, :], b2=ln2_b[l][None, :],
    )
    return pw, mw, ew


def _rope_tables(T, past, rows):
    half = ROPE_DIM // 2
    pos = (past + jnp.arange(T)).astype(F32)
    inv = ROPE_THETA ** (-jnp.arange(half, dtype=F32) / half)
    ang = pos[:, None] * inv[None, :]
    cos = jnp.cos(ang)
    sin = jnp.sin(ang)
    c2 = jnp.concatenate([cos, cos], axis=1)
    s2 = jnp.concatenate([sin, sin], axis=1)
    z = lambda n: jnp.zeros((T, n), F32)
    tail = HEAD_PAD - NOPE_DIM - ROPE_DIM
    tabs = dict(
        cq=jnp.concatenate([jnp.ones((T, NOPE_DIM), F32), c2, z(tail)], axis=1) * Q_SCALE,
        sq=jnp.concatenate([z(NOPE_DIM), s2, z(tail)], axis=1) * Q_SCALE,
        ck=jnp.concatenate([z(KR_LANE), c2, z(tail)], axis=1),
        sk=jnp.concatenate([z(KR_LANE), s2, z(tail)], axis=1),
    )
    if rows > T:
        tabs = {k: jnp.tile(v, (rows // T, 1)) for k, v in tabs.items()}
    return tabs


def _trunk(x, hist, layers, rwt, rb):
    B, T, _ = x.shape
    N = B * T
    past = 0 if hist is None else hist[0].shape[2]
    tabs = _rope_tables(T, past, min(512, N))
    L = GMLP_CHUNK if T % GMLP_CHUNK == 0 else T
    x2d = x.reshape(N, D_MODEL)
    Cs, ns, ms, gvs = [], [], [], []
    stacked = None
    for l, (pw, mw, ew) in enumerate(layers):
        KB = _attn_block(T) if hist is None else None
        outs = _proj_call(x2d, pw, tabs, T, hist is not None, KB, l, len(layers), stacked)
        q, k, v, _, _, mq, mk, mv, gates, so, gu, gv = outs[:12]
        stacked = (outs[CKV_OUT], outs[KR_OUT])
        seq = lambda a: a.reshape(B, T, a.shape[-1])
        if hist is None:
            a_out = _attn_call(seq(q), k.reshape(B, T // KB, QK_PAD, KB), seq(v))
            s0 = jnp.zeros((B, MLSTM_HEADS, MLSTM_DH, 2 * MLSTM_DH), F32)
            m0 = jnp.zeros((B, MLSTM_HEADS, 8, 128), F32)
        else:
            ckv_past, kr_past, c0, n0, m0_in = hist
            P = ckv_past.shape[2]
            krp = jnp.pad(kr_past[l], ((0, 0), (0, 0), (KR_LANE, HEAD_PAD - KR_LANE - ROPE_DIM)))
            a_out = _attn_hist_call(seq(q), ckv_past[l], krp, pw["wkv"], seq(k), seq(v))
            s0 = jnp.concatenate(
                [jnp.swapaxes(c0[l], -1, -2),
                 jnp.broadcast_to(n0[l][:, :, :, None], (B, MLSTM_HEADS, MLSTM_DH, MLSTM_DH))], axis=3)
            m0 = jnp.broadcast_to(m0_in[l][:, :, None, None], (B, MLSTM_HEADS, 8, 128))
        b_out, sfin, mfin = _mlstm_call(seq(mq), seq(mk), seq(mv), seq(gates), seq(so), s0, m0)
        mwl = dict(mw)
        mwl["ws"] = mw["ws_full"][:, :L, :L]
        mwl["bsf"] = jnp.repeat(mw["bs_full"][:, :L].T, GMLP_DG, axis=1)
        x1 = _merge_call(x2d, a_out.reshape(N, V_ALL), b_out.reshape(N, MLSTM_WIDTH), gu, gv, mwl, L)
        x2d = _moe_call(x1, dict(ew, rwt=rwt, rb=rb))
        Cs.append(jnp.swapaxes(sfin[:, :, :, :MLSTM_DH], -1, -2))
        ns.append(sfin[:, :, :, MLSTM_DH])
        ms.append(mfin[:, :, 0, 0])
        if hist is not None:
            gvs.append(outs[12].reshape(B, T, GMLP_WIDTH))
    depth = len(layers)
    res = [x2d.reshape(B, T, D_MODEL), stacked[0].reshape(depth, B, T, KV_RANK),
           stacked[1].reshape(depth, B, T, ROPE_DIM), jnp.stack(Cs), jnp.stack(ns), jnp.stack(ms)]
    if hist is not None:
        res.append(jnp.stack(gvs))
    return res


def kernel(x_prompt, x_sample, cache_mla_ckv, cache_mla_krope, state_mlstm_c, state_mlstm_n, state_mlstm_m,
           w_in, b_in, q_norm_g, kv_norm_g, w_uq, w_ukv, gmlp_ln_g, gmlp_ln_b, gmlp_ws, gmlp_bs, w_branch,
           w_out, ln1_g, ln1_b, router_w, router_b, moe_w_gate, moe_w_up, moe_w_down, shared_w_gate,
           shared_w_up, shared_w_down, ln2_g, ln2_b):
    depth = w_in.shape[0]
    layers = [_prep_layer(l, w_in, b_in, q_norm_g, kv_norm_g, w_uq, w_ukv, gmlp_ln_g, gmlp_ln_b, gmlp_ws,
                          gmlp_bs, w_branch, w_out, ln1_g, ln1_b, moe_w_gate, moe_w_up, moe_w_down,
                          shared_w_gate, shared_w_up, shared_w_down, ln2_g, ln2_b) for l in range(depth)]
    rwt = router_w.T.astype(BF16)
    rb = router_b[:, None]
    yp, p_ckv, p_kr, p_c, p_n, p_m = _trunk(x_prompt, None, layers, rwt, rb)
    ys, s_ckv, s_kr, s_c, s_n, s_m, s_gv = _trunk(
        x_sample, (cache_mla_ckv, cache_mla_krope, state_mlstm_c, state_mlstm_n, state_mlstm_m), layers, rwt, rb)
    return (yp, ys, p_ckv, p_kr, p_c, p_n, p_m, s_ckv, s_kr, s_c, s_n, s_m, s_gv)
```

```python
import functools

import jax
import jax.numpy as jnp
import numpy as np
from jax import lax
from jax.experimental import pallas as pl
from jax.experimental.pallas import tpu as pltpu

F32 = jnp.float32
BF16 = jnp.bfloat16

D_MODEL = 1024
CHUNK = 64
MLA_HEADS = 8
Q_RANK = 256
KV_RANK = 256
NOPE_DIM = 64
ROPE_DIM = 32
V_DIM = 64
ROPE_THETA = 10000.0
ATTN_SCALE = (NOPE_DIM + ROPE_DIM) ** -0.5
Q_SCALE = ATTN_SCALE * float(np.log2(np.e))
MLSTM_HEADS = 4
MLSTM_DH = 128
MLSTM_WIDTH = MLSTM_HEADS * MLSTM_DH
GMLP_GROUPS = 4
GMLP_DG = 128
GMLP_WIDTH = GMLP_GROUPS * GMLP_DG
GMLP_CHUNK = 128
N_BRANCH = 3
N_EXPERTS = 16
N_GROUPS = 4
EXPERTS_PER_GROUP = N_EXPERTS // N_GROUPS
D_EXPERT = 256
D_SHARED = 256
DEPTH = 4
DN_ALPHA = (2 * DEPTH) ** 0.25
EPS = 1e-5

HEAD_PAD = 128
QK_PAD = MLA_HEADS * HEAD_PAD
V_ALL = MLA_HEADS * V_DIM
P_CQ, P_CKV, P_KR, P_KRR, P_MQ, P_MK, P_MV, P_G, P_MO, P_GU, P_GV = (
    0, 256, 512, 640, 768, 1280, 1792, 2304, 2432, 2944, 3456)
D_PROJ = 3968
O_CQ, O_CKV, O_KR, O_MQ, O_MK, O_MV, O_MI, O_MF, O_MO, O_GU, O_GV, O_GT = (
    0, 256, 512, 544, 1056, 1568, 2080, 2084, 2088, 2600, 3112, 3624)
KR_LANE = NOPE_DIM
VMEM_LIMIT = 56 * 1024 * 1024
NEG_INF = float("-inf")


def _dot(a, b):
    return jnp.dot(a, b, preferred_element_type=F32)


def _dot_nt(a, b):
    return lax.dot_general(a, b, (((1,), (1,)), ((), ())), preferred_element_type=F32)


def _const_spec(shape):
    nd = len(shape)
    return pl.BlockSpec(shape, lambda *_: (0,) * nd, pipeline_mode=pl.Buffered(1))


def _layernorm(x, g, b):
    mu = jnp.mean(x, axis=-1, keepdims=True)
    xc = x - mu
    var = jnp.mean(xc * xc, axis=-1, keepdims=True)
    return xc * lax.rsqrt(var + EPS) * g + b


def _rmsnorm(x, g):
    return x * lax.rsqrt(jnp.mean(x * x, axis=-1, keepdims=True) + EPS) * g


def _gelu(x):
    return 0.5 * x * (1.0 + jnp.tanh(np.sqrt(2.0 / np.pi).astype(np.float32) * (x + 0.044715 * (x * x * x))))


def _sigmoid(x):
    return 1.0 / (1.0 + jnp.exp(-x))


def _log_sigmoid(x):
    return jnp.minimum(x, 0.0) - jnp.log(1.0 + jnp.exp(-jnp.abs(x)))


PROJ_N_IN = 13


def _proj_kernel(*refs, key_block, n_carried):
    (x_ref, w1_ref, b1_ref, qg_ref, kvg_ref, wq_ref, wkv_ref, cq_ref, sq_ref, ck_ref, sk_ref,
     lng_ref, lnb_ref) = refs[:PROJ_N_IN]
    (q_out, k_out, v_out, ckv_out, kr_out, mq_out, mk_out, mv_out, g_out, so_out, gu_out,
     gv_out, *rest) = refs[PROJ_N_IN + n_carried:]
    xb = x_ref[...].astype(BF16)

    def z(lo, hi):
        return _dot(xb, w1_ref[:, lo:hi]) + b1_ref[:, lo:hi]


    zu = z(P_GU, D_PROJ)
    gu_out[...] = _gelu(zu[:, :GMLP_WIDTH]).astype(BF16)
    vrows = _layernorm(_gelu(zu[:, GMLP_WIDTH:]), lng_ref[...], lnb_ref[...])
    gv_out[...] = vrows.astype(BF16)
    if rest:
        rest[0][...] = vrows
    so_out[...] = _sigmoid(z(P_MO, P_GU)).astype(BF16)

    zc = z(P_CQ, P_MQ)
    cqn = _rmsnorm(zc[:, P_CQ:P_CKV], qg_ref[...])
    ckvn = _rmsnorm(zc[:, P_CKV:P_KR], kvg_ref[...])
    kr = zc[:, P_KR:P_KRR] * ck_ref[...] + zc[:, P_KRR:P_MQ] * sk_ref[...]
    ckv_out[...] = ckvn
    kr_out[...] = kr[:, KR_LANE:KR_LANE + ROPE_DIM]
    qq = _dot(cqn.astype(BF16), wq_ref[...])
    cos8 = jnp.concatenate([cq_ref[...]] * MLA_HEADS, axis=1)
    sin8 = jnp.concatenate([sq_ref[...]] * MLA_HEADS, axis=1)
    q_out[...] = (qq[:, :QK_PAD] * cos8 + qq[:, QK_PAD:] * sin8).astype(BF16)
    kk = _dot(ckvn.astype(BF16), wkv_ref[...])
    kr8 = jnp.concatenate([kr] * MLA_HEADS, axis=1)
    kval = kk[:, :QK_PAD] + kr8
    if key_block is None:
        k_out[...] = kval.astype(BF16)
    else:
        for c in range(kval.shape[0] // key_block):
            k_out[c] = kval[key_block * c:key_block * (c + 1), :].T.astype(BF16)
    v_out[...] = kk[:, QK_PAD:].astype(BF16)

    zm = z(P_MQ, P_MO)
    mq_out[...] = zm[:, 0:512].astype(BF16)
    mk_out[...] = (zm[:, 512:1024] * (MLSTM_DH ** -0.5)).astype(BF16)
    mv_out[...] = zm[:, 1024:1536].astype(BF16)
    zg = zm[:, 1536:1664]
    lane = lax.broadcasted_iota(jnp.int32, zg.shape, 1)
    g_out[...] = jnp.where(lane < MLSTM_HEADS, zg, _log_sigmoid(zg))


CKV_OUT, KR_OUT = 3, 4


def _proj_call(x2d, pw, tabs, T, want_vrows, key_block, layer, depth, carried):
    N = x2d.shape[0]
    TM = min(512, N)
    nt = max(1, T // TM)
    grid = (N // TM,)
    row = lambda w: pl.BlockSpec((TM, w), lambda i: (i, 0))
    tab = pl.BlockSpec((TM, HEAD_PAD), lambda i: (i % nt, 0))
    in_specs = [
        row(D_MODEL),
        _const_spec((D_MODEL, D_PROJ)), _const_spec((1, D_PROJ)),
        _const_spec((1, Q_RANK)), _const_spec((1, KV_RANK)),
        _const_spec((Q_RANK, 2 * QK_PAD)), _const_spec((KV_RANK, QK_PAD + V_ALL)),
        tab, tab, tab, tab,
        _const_spec((1, GMLP_WIDTH)), _const_spec((1, GMLP_WIDTH)),
    ]
    out_shapes = [
        ((N, QK_PAD), BF16), ((N, QK_PAD), BF16), ((N, V_ALL), BF16), ((N, KV_RANK), F32),
        ((N, HEAD_PAD), F32), ((N, MLSTM_WIDTH), BF16), ((N, MLSTM_WIDTH), BF16), ((N, MLSTM_WIDTH), BF16),
        ((N, HEAD_PAD), F32), ((N, MLSTM_WIDTH), BF16), ((N, GMLP_WIDTH), BF16), ((N, GMLP_WIDTH), BF16),
    ]
    if want_vrows:
        out_shapes.append(((N, GMLP_WIDTH), F32))
    out_specs = [row(s[1]) for s, _ in out_shapes]
    if key_block is not None:
        out_shapes[1] = ((N // key_block, QK_PAD, key_block), BF16)
        out_specs[1] = pl.BlockSpec((TM // key_block, QK_PAD, key_block), lambda i: (i, 0, 0))
    for pos, width in ((CKV_OUT, KV_RANK), (KR_OUT, ROPE_DIM)):
        out_shapes[pos] = ((depth, N, width), F32)
        out_specs[pos] = pl.BlockSpec((None, TM, width), lambda i: (layer, i, 0))
    args = [x2d, pw["w1"], pw["b1"], pw["qg"], pw["kvg"], pw["wq"], pw["wkv"],
            tabs["cq"], tabs["sq"], tabs["ck"], tabs["sk"], pw["lng"], pw["lnb"]]
    aliases = {}
    if carried is not None:
        in_specs += [pl.BlockSpec(memory_space=pl.ANY)] * 2
        aliases = {PROJ_N_IN: CKV_OUT, PROJ_N_IN + 1: KR_OUT}
        args += list(carried)
    return pl.pallas_call(
        functools.partial(_proj_kernel, key_block=key_block, n_carried=len(aliases)),
        grid=grid,
        in_specs=in_specs,
        out_specs=out_specs,
        out_shape=[jax.ShapeDtypeStruct(s, d) for s, d in out_shapes],
        input_output_aliases=aliases,
        compiler_params=pltpu.CompilerParams(dimension_semantics=("parallel",), vmem_limit_bytes=VMEM_LIMIT),
        name="proj",
    )(*args)


def _attn_kernel(q_ref, kt_ref, v_ref, o_ref, mx_scr, acc_scr, s_scr, *, TQ):
    i = pl.program_id(1)
    ones_blk = jnp.ones((TQ, HEAD_PAD), BF16)
    rc = lax.broadcasted_iota(jnp.int32, (TQ, TQ), 0) // CHUNK
    cc = lax.broadcasted_iota(jnp.int32, (TQ, TQ), 1) // CHUNK
    visible = cc <= rc
    lane = lax.broadcasted_iota(jnp.int32, (TQ, HEAD_PAD), 1)
    nlane = TQ // 128

    def fold(a, op):
        r = a[:, 0:128]
        for t in range(1, nlane):
            r = op(r, a[:, 128 * t:128 * (t + 1)])
        return r

    hsl = lambda h: slice(HEAD_PAD * h, HEAD_PAD * (h + 1))
    heads = range(MLA_HEADS)

    def sweep_max(j, masked, first):
        ss = [_dot(q_ref[0, :, hsl(h)], kt_ref[0, j, hsl(h), :]) for h in heads]
        if masked:
            ss = [jnp.where(visible, s, NEG_INF) for s in ss]
        for h in heads:
            s_scr[h, j] = ss[h]
        for h in heads:
            mx = fold(ss[h], jnp.maximum)
            mx_scr[h] = mx if first else jnp.maximum(mx_scr[h], mx)

    def sweep_pv(j, first):
        r0 = pl.multiple_of(j * TQ, TQ)
        ps = [jnp.exp2(s_scr[h, j] - jnp.concatenate([mx_scr[h]] * nlane, axis=1)).astype(BF16) for h in heads]
        pvs = []
        for p in range(MLA_HEADS // 2):
            vs = slice(HEAD_PAD * p, HEAD_PAD * (p + 1))
            vext = jnp.concatenate([v_ref[0, pl.ds(r0, TQ), vs], ones_blk], axis=1)
            pv2 = _dot(jnp.concatenate([ps[2 * p], ps[2 * p + 1]], axis=0), vext)
            pvs += [pv2[:TQ], pv2[TQ:]]
        for h in heads:
            acc_scr[h] = pvs[h] if first else acc_scr[h] + pvs[h]

    def body_max(j, c):
        sweep_max(j, False, False)
        return c

    def body_pv(j, c):
        sweep_pv(j, False)
        return c

    sweep_max(i, True, True)
    lax.fori_loop(0, i, body_max, 0)
    for h in heads:
        mx_scr[h] = jnp.broadcast_to(jnp.max(mx_scr[h], axis=-1, keepdims=True), (TQ, 128))
    sweep_pv(i, True)
    lax.fori_loop(0, i, body_pv, 0)
    for p in range(MLA_HEADS // 2):
        a0 = acc_scr[2 * p]
        a1 = acc_scr[2 * p + 1]
        o0 = a0[:, :HEAD_PAD] / a0[:, HEAD_PAD:]
        o1 = a1[:, :HEAD_PAD] / a1[:, HEAD_PAD:]
        o_ref[0, :, HEAD_PAD * p:HEAD_PAD * (p + 1)] = jnp.where(lane < V_DIM, o0, o1).astype(BF16)


def _attn_block(T):
    return min(256, T)


def _attn_call(q, kt, v):
    B, T, _ = q.shape
    TQ = _attn_block(T)
    nq = T // TQ
    return pl.pallas_call(
        functools.partial(_attn_kernel, TQ=TQ),
        grid=(B, nq),
        in_specs=[pl.BlockSpec((1, TQ, QK_PAD), lambda b, i: (b, i, 0)),
                  pl.BlockSpec((1, nq, QK_PAD, TQ), lambda b, i: (b, 0, 0, 0)),
                  pl.BlockSpec((1, T, V_ALL), lambda b, i: (b, 0, 0))],
        out_specs=pl.BlockSpec((1, TQ, V_ALL), lambda b, i: (b, i, 0)),
        out_shape=jax.ShapeDtypeStruct((B, T, V_ALL), BF16),
        scratch_shapes=[pltpu.VMEM((MLA_HEADS, TQ, 128), F32), pltpu.VMEM((MLA_HEADS, TQ, 2 * HEAD_PAD), F32),
                        pltpu.VMEM((MLA_HEADS, nq, TQ, TQ), F32)],
        compiler_params=pltpu.CompilerParams(dimension_semantics=("parallel", "arbitrary"),
                                             vmem_limit_bytes=VMEM_LIMIT),
        name="attn",
    )(q, kt, v)


def _attn_hist_kernel(q_ref, cp_ref, krp_ref, wkv_ref, kn_ref, vn_ref, o_ref, kp_ref, vp_ref):
    T = q_ref.shape[1]
    lane = lax.broadcasted_iota(jnp.int32, (T, HEAD_PAD), 1)
    kk = _dot(cp_ref[0].astype(BF16), wkv_ref[...])
    kp_ref[0] = (kk[:, :QK_PAD] + jnp.concatenate([krp_ref[0]] * MLA_HEADS, axis=1)).astype(BF16)
    vp_ref[0] = kk[:, QK_PAD:].astype(BF16)
    for p in range(MLA_HEADS // 2):
        pair = []
        vs = slice(HEAD_PAD * p, HEAD_PAD * (p + 1))
        for hh in range(2):
            h = 2 * p + hh
            hs = slice(HEAD_PAD * h, HEAD_PAD * (h + 1))
            qh = q_ref[0, :, hs]
            s1 = _dot_nt(qh, kp_ref[0, :, hs])
            s2 = _dot_nt(qh, kn_ref[0, :, hs])
            m = jnp.maximum(jnp.max(s1, axis=-1, keepdims=True), jnp.max(s2, axis=-1, keepdims=True))
            p1 = jnp.exp2(s1 - m)
            p2 = jnp.exp2(s2 - m)
            l = jnp.sum(p1, axis=-1, keepdims=True) + jnp.sum(p2, axis=-1, keepdims=True)
            acc = _dot(p1.astype(BF16), vp_ref[0, :, vs]) + _dot(p2.astype(BF16), vn_ref[0, :, vs])
            pair.append(acc / l)
        o_ref[0, :, vs] = jnp.where(lane < V_DIM, pair[0], pair[1]).astype(BF16)


def _attn_hist_call(q, ckv_cache, layer, krp, wkv, kn, vn):
    B, T, _ = q.shape
    P = ckv_cache.shape[2]
    blk = lambda t, w: pl.BlockSpec((1, t, w), lambda b: (b, 0, 0))
    cache_blk = pl.BlockSpec((None, 1, P, KV_RANK), lambda b: (layer, b, 0, 0))
    return pl.pallas_call(
        _attn_hist_kernel,
        grid=(B,),
        in_specs=[blk(T, QK_PAD), cache_blk, blk(P, HEAD_PAD), _const_spec((KV_RANK, QK_PAD + V_ALL)),
                  blk(T, QK_PAD), blk(T, V_ALL)],
        out_specs=blk(T, V_ALL),
        out_shape=jax.ShapeDtypeStruct((B, T, V_ALL), BF16),
        scratch_shapes=[pltpu.VMEM((1, P, QK_PAD), BF16), pltpu.VMEM((1, P, V_ALL), BF16)],
        compiler_params=pltpu.CompilerParams(dimension_semantics=("parallel",), vmem_limit_bytes=VMEM_LIMIT),
        name="attn_hist",
    )(q, ckv_cache, krp, wkv, kn, vn)


MLSTM_BB = 4


def _mlstm_kernel(q_ref, k_ref, v_ref, g_ref, so_ref, s0_ref, m0_ref, h_out, sfin_out, mfin_out,
                  s_scr, m_scr, *, TL, BB):
    t = pl.program_id(1)
    L = CHUNK
    DH = MLSTM_DH

    @pl.when(t == 0)
    def _():
        s_scr[...] = s0_ref[...]
        m_scr[...] = m0_ref[...]

    r_i = lax.broadcasted_iota(jnp.int32, (L, L), 0)
    c_i = lax.broadcasted_iota(jnp.int32, (L, L), 1)
    causal = c_i <= r_i
    tri = jnp.where(causal, 1.0, 0.0).astype(BF16)
    lane = lax.broadcasted_iota(jnp.int32, (L, 128), 1)
    ones_blk = jnp.ones((L, DH), BF16)

    def rep(col):
        return jnp.broadcast_to(col, (L, 128))

    def chunk(c, carry):
        rows = pl.ds(pl.multiple_of(c * L, L), L)
        pairs = [(bb, h) for bb in range(BB) for h in range(MLSTM_HEADS)]
        hsl = lambda h: slice(DH * h, DH * (h + 1))
        Gs, cums, VTs = [], [], []
        for bb in range(BB):
            G = g_ref[bb, rows, :]
            g_hi = G.astype(BF16)
            g_r1 = G - g_hi.astype(F32)
            g_mid = g_r1.astype(BF16)
            g_lo = (g_r1 - g_mid.astype(F32)).astype(BF16)
            cum = _dot(tri, g_hi) + _dot(tri, g_mid) + _dot(tri, g_lo)
            Gs.append(G)
            cums.append(cum)
            VTs.append(jnp.where(lane < MLSTM_HEADS, G, cum).T)
        qk_raw = {p: _dot_nt(q_ref[p[0], rows, hsl(p[1])], k_ref[p[0], rows, hsl(p[1])]) for p in pairs}
        sq = {p: _dot(q_ref[p[0], rows, hsl(p[1])], s_scr[p[0], p[1]].astype(BF16)) for p in pairs}
        gate = {}
        for bb, h in pairs:
            b_t = rep(jnp.sum(jnp.where(lane == MLSTM_HEADS + h, cums[bb], 0.0), axis=1, keepdims=True))
            ig_t = rep(jnp.sum(jnp.where(lane == h, Gs[bb], 0.0), axis=1, keepdims=True))
            brow = VTs[bb][MLSTM_HEADS + h:MLSTM_HEADS + h + 1, :]
            igrow = VTs[bb][h:h + 1, :]
            d = jnp.where(causal, b_t[:, :L] - brow + igrow, NEG_INF)
            gate[bb, h] = (b_t, ig_t, d, rep(jnp.max(d, axis=1, keepdims=True)))
        stab = {}
        for bb, h in pairs:
            b_t, ig_t, d, dmax = gate[bb, h]
            m_prev = m_scr[bb, h, 0:1, :]
            g_t = b_t + m_prev
            mt = jnp.maximum(g_t, dmax)
            stab[bb, h] = (m_prev, mt, jnp.exp(g_t - mt), jnp.exp(d - mt[:, :L]))
        for bb, h in pairs:
            m_prev, mt, inter, w = stab[bb, h]
            vext = jnp.concatenate([v_ref[bb, rows, hsl(h)], ones_blk], axis=1)
            intra = _dot((qk_raw[bb, h] * w).astype(BF16), vext)
            num = inter * sq[bb, h][:, :DH] + intra[:, :DH]
            den = jnp.maximum(jnp.abs(inter * sq[bb, h][:, DH:] + intra[:, DH:]), jnp.exp(-mt))
            so = so_ref[bb, rows, hsl(h)].astype(F32)
            h_out[bb, rows, hsl(h)] = (so * (num / den)).astype(BF16)
        for bb, h in pairs:
            b_t, ig_t, _, _ = gate[bb, h]
            m_prev, mt, _, _ = stab[bb, h]
            mL = mt[L - 1:L, :]
            bL = b_t[L - 1:L, :]
            ws_t = jnp.exp(bL - b_t + ig_t - mL)
            decay = jnp.exp(bL + m_prev - mL)
            wv = jnp.concatenate([ws_t * v_ref[bb, rows, hsl(h)].astype(F32), ws_t], axis=1).astype(BF16)
            kT = k_ref[bb, rows, hsl(h)].astype(F32).T.astype(BF16)
            s_scr[bb, h] = jnp.concatenate([decay, decay], axis=1) * s_scr[bb, h] + _dot(kT, wv)
            m_scr[bb, h] = jnp.broadcast_to(mL, (8, 128))
        return carry

    lax.fori_loop(0, TL // L, chunk, 0)

    @pl.when(t == pl.num_programs(1) - 1)
    def _():
        sfin_out[...] = s_scr[...]
        mfin_out[...] = m_scr[...]


def _mlstm_call(mq, mk, mv, gates, so, s0, m0):
    B, T, _ = mq.shape
    TL = min(512, T)
    BB = int(np.gcd(MLSTM_BB, B))
    seq = lambda w: pl.BlockSpec((BB, TL, w), lambda b, t: (b, t, 0))
    st = pl.BlockSpec((BB, MLSTM_HEADS, MLSTM_DH, 2 * MLSTM_DH), lambda b, t: (b, 0, 0, 0))
    mst = pl.BlockSpec((BB, MLSTM_HEADS, 8, 128), lambda b, t: (b, 0, 0, 0))
    return pl.pallas_call(
        functools.partial(_mlstm_kernel, TL=TL, BB=BB),
        grid=(B // BB, T // TL),
        in_specs=[seq(MLSTM_WIDTH), seq(MLSTM_WIDTH), seq(MLSTM_WIDTH), seq(128), seq(MLSTM_WIDTH), st, mst],
        out_specs=[seq(MLSTM_WIDTH), st, mst],
        out_shape=[jax.ShapeDtypeStruct((B, T, MLSTM_WIDTH), BF16),
                   jax.ShapeDtypeStruct((B, MLSTM_HEADS, MLSTM_DH, 2 * MLSTM_DH), F32),
                   jax.ShapeDtypeStruct((B, MLSTM_HEADS, 8, 128), F32)],
        scratch_shapes=[pltpu.VMEM((BB, MLSTM_HEADS, MLSTM_DH, 2 * MLSTM_DH), F32),
                        pltpu.VMEM((BB, MLSTM_HEADS, 8, 128), F32)],
        compiler_params=pltpu.CompilerParams(dimension_semantics=("parallel", "arbitrary"),
                                             vmem_limit_bytes=VMEM_LIMIT),
        name="mlstm",
    )(mq, mk, mv, gates, so, s0, m0)


def _merge_kernel(x_ref, a_ref, b_ref, gu_ref, gv_ref, wgt_ref, bgt_ref, ws_ref, bsf_ref, wb_ref, wo_ref,
                  g1_ref, b1_ref, x1_out, c_scr, *, L, TM):
    x = x_ref[...]
    xb = x.astype(BF16)
    r_i = lax.broadcasted_iota(jnp.int32, (L, L), 0)
    c_i = lax.broadcasted_iota(jnp.int32, (L, L), 1)
    for g in range(GMLP_GROUPS):
        gs = slice(GMLP_DG * g, GMLP_DG * (g + 1))
        wsg = jnp.where(c_i <= r_i, ws_ref[g], 0.0).astype(BF16)
        for c in range(TM // L):
            rs = slice(L * c, L * (c + 1))
            sp = _dot(wsg, gv_ref[rs, gs]) + bsf_ref[:, gs]
            c_scr[rs, gs] = (gu_ref[rs, gs].astype(F32) * sp).astype(BF16)
    merged = None
    for kb, br in enumerate((a_ref, b_ref, c_scr)):
        cs = slice(D_MODEL * kb, D_MODEL * (kb + 1))
        gate = _sigmoid(_dot(xb, wgt_ref[:, cs]) + bgt_ref[:, cs])
        term = gate * _dot(br[...], wb_ref[kb])
        merged = term if merged is None else merged + term
    mb = merged.astype(BF16)
    RH = min(256, TM)
    for r in range(TM // RH):
        rs = slice(RH * r, RH * (r + 1))
        y = _dot(mb[rs, :], wo_ref[...])
        x1_out[rs, :] = _layernorm(DN_ALPHA * x[rs, :] + y, g1_ref[...], b1_ref[...])


def _merge_call(x2d, a2d, b2d, gu, gv, mw, L):
    N = x2d.shape[0]
    TM = min(512, N)
    row = lambda w: pl.BlockSpec((TM, w), lambda i: (i, 0))
    return pl.pallas_call(
        functools.partial(_merge_kernel, L=L, TM=TM),
        grid=(N // TM,),
        in_specs=[row(D_MODEL), row(V_ALL), row(MLSTM_WIDTH), row(GMLP_WIDTH), row(GMLP_WIDTH),
                  _const_spec((D_MODEL, N_BRANCH * D_MODEL)), _const_spec((1, N_BRANCH * D_MODEL)),
                  _const_spec((GMLP_GROUPS, L, L)), _const_spec((L, GMLP_WIDTH)),
                  _const_spec((N_BRANCH, 512, D_MODEL)), _const_spec((D_MODEL, D_MODEL)),
                  _const_spec((1, D_MODEL)), _const_spec((1, D_MODEL))],
        out_specs=row(D_MODEL),
        out_shape=jax.ShapeDtypeStruct((N, D_MODEL), F32),
        scratch_shapes=[pltpu.VMEM((TM, GMLP_WIDTH), BF16)],
        compiler_params=pltpu.CompilerParams(dimension_semantics=("parallel",), vmem_limit_bytes=VMEM_LIMIT),
        name="merge",
    )(x2d, a2d, b2d, gu, gv, mw["wgt"], mw["bgt"], mw["ws"], mw["bsf"], mw["wb"], mw["wo"], mw["g1"], mw["b1"])


def _route_rows(logits_t, rb_ref):
    s = _sigmoid(logits_t)
    sb = s + rb_ref[...]
    rows = [sb[e:e + 1, :] for e in range(N_EXPERTS)]
    srow = [s[e:e + 1, :] for e in range(N_EXPERTS)]
    gscore = []
    for g in range(N_GROUPS):
        mem = rows[EXPERTS_PER_GROUP * g:EXPERTS_PER_GROUP * (g + 1)]
        best = None
        for a in range(EXPERTS_PER_GROUP):
            for b in range(a + 1, EXPERTS_PER_GROUP):
                pr = mem[a] + mem[b]
                best = pr if best is None else jnp.maximum(best, pr)
        gscore.append(best)
    gmax = functools.reduce(jnp.maximum, gscore)
    taken = None
    gsel = []
    for g in range(N_GROUPS):
        hit = gscore[g] == gmax
        if taken is None:
            gsel.append(hit)
            taken = hit
        else:
            gsel.append(jnp.logical_and(hit, jnp.logical_not(taken)))
            taken = jnp.logical_or(taken, hit)
    sel_w = []
    for e in range(N_EXPERTS):
        g = e // EXPERTS_PER_GROUP
        rank = None
        for o in range(EXPERTS_PER_GROUP * g, EXPERTS_PER_GROUP * (g + 1)):
            if o == e:
                continue
            ahead = (rows[o] >= rows[e]) if o < e else (rows[o] > rows[e])
            ahead = jnp.where(ahead, 1.0, 0.0)
            rank = ahead if rank is None else rank + ahead
        chosen = jnp.logical_and(gsel[g], rank < 1.5)
        sel_w.append(jnp.where(chosen, srow[e], 0.0))
    den = functools.reduce(jnp.add, sel_w)
    gate_rows = [w_ / den for w_ in sel_w]
    return gsel, gate_rows


def _swiglu_hidden(xb, wg, wu):
    hg = _dot(xb, wg)
    return hg * _sigmoid(hg) * _dot(xb, wu)


def _moe_kernel(x_ref, rw_ref, rb_ref, tri_ref, wgs_ref, wus_ref, wds_ref, wg_ref, wu_ref, wd_ref,
                g2_ref, b2_ref, x2_out, xb_scr, col_scr, row_scr, flag_ref, *, TM, HT, CAP):
    j = pl.program_id(1)
    nh = TM // HT
    gate_lanes = EXPERTS_PER_GROUP
    RANK_LANE, GRP_LANE = gate_lanes, gate_lanes + 1

    @pl.when(j == 0)
    def _route():
        xb = x_ref[...].astype(BF16)
        xb_scr[...] = xb
        gsel, gate_rows = _route_rows(_dot_nt(rw_ref[...], xb), rb_ref)
        isg = [jnp.where(m, 1.0, 0.0) for m in gsel]
        grp = functools.reduce(jnp.add, [float(g) * isg[g] for g in range(N_GROUPS)])
        g4 = [functools.reduce(jnp.add, [isg[g] * gate_rows[EXPERTS_PER_GROUP * g + e] for g in range(N_GROUPS)])
              for e in range(EXPERTS_PER_GROUP)]
        ranks = []
        worst = None
        for hf in range(nh):
            hsl = slice(HT * hf, HT * (hf + 1))
            m8 = jnp.concatenate([isg[g][:, hsl] for g in range(N_GROUPS)]
                                 + [jnp.zeros((8 - N_GROUPS, HT), F32)], axis=0).astype(BF16)
            before = _dot(m8, tri_ref[...])
            ranks.append(functools.reduce(jnp.add, [isg[g][:, hsl] * before[g:g + 1, :] for g in range(N_GROUPS)]))
            for g in range(N_GROUPS):
                cnt = jnp.sum(isg[g][:, hsl])
                worst = cnt if worst is None else jnp.maximum(worst, cnt)
        rank = jnp.concatenate(ranks, axis=1)
        flag_ref[0] = (worst > float(CAP)).astype(jnp.int32)
        row_scr[...] = jnp.concatenate([rank, grp, jnp.zeros((6, TM), F32)], axis=0)
        col_scr[...] = jnp.concatenate(g4 + [rank, grp, jnp.zeros((128 - gate_lanes - 2, TM), F32)], axis=0).T
        x2_out[...] = _dot(_swiglu_hidden(xb, wgs_ref[...], wus_ref[...]).astype(BF16), wds_ref[...])

    gf = j.astype(F32)
    overflow = flag_ref[0] != 0

    def gated(h, gcols):
        parts = [h[:, D_EXPERT * e:D_EXPERT * (e + 1)] * gcols[:, e:e + 1] for e in range(EXPERTS_PER_GROUP)]
        return jnp.concatenate(parts, axis=1).astype(BF16)

    @pl.when(jnp.logical_not(overflow))
    def _compact():
        r_iota = lax.broadcasted_iota(jnp.int32, (CAP, HT), 0).astype(F32)
        c_iota = lax.broadcasted_iota(jnp.int32, (HT, CAP), 1).astype(F32)
        xcs, gcs = [], []
        for hf in range(nh):
            hsl = slice(HT * hf, HT * (hf + 1))
            pick = jnp.logical_and(row_scr[1:2, hsl] == gf, row_scr[0:1, hsl] == r_iota)
            P = jnp.where(pick, 1.0, 0.0).astype(BF16)
            xcs.append(_dot(P, xb_scr[hsl, :]).astype(BF16))
            cols = col_scr[hsl, :]
            c_hi = cols.astype(BF16)
            c_lo = (cols - c_hi.astype(F32)).astype(BF16)
            gcs.append(_dot(P, c_hi) + _dot(P, c_lo))
        xc = jnp.concatenate(xcs, axis=0)
        h = gated(_swiglu_hidden(xc, wg_ref[...], wu_ref[...]), jnp.concatenate(gcs, axis=0))
        y = _dot(h, wd_ref[...]).astype(BF16)
        for hf in range(nh):
            hsl = slice(HT * hf, HT * (hf + 1))
            cols = col_scr[hsl, :]
            pick = jnp.logical_and(cols[:, GRP_LANE:GRP_LANE + 1] == gf, cols[:, RANK_LANE:RANK_LANE + 1] == c_iota)
            Pt = jnp.where(pick, 1.0, 0.0).astype(BF16)
            x2_out[hsl, :] += _dot(Pt, y[CAP * hf:CAP * (hf + 1), :])

    @pl.when(overflow)
    def _uncompacted():
        RC = min(256, TM)

        def rows(c, carry):
            rs = pl.ds(pl.multiple_of(c * RC, RC), RC)
            cols = col_scr[rs, :]
            gcols = jnp.where(cols[:, GRP_LANE:GRP_LANE + 1] == gf, cols, 0.0)
            h = gated(_swiglu_hidden(xb_scr[rs, :], wg_ref[...], wu_ref[...]), gcols)
            x2_out[rs, :] += _dot(h, wd_ref[...])
            return carry

        lax.fori_loop(0, TM // RC, rows, 0)

    @pl.when(j == pl.num_programs(1) - 1)
    def _finish():
        x2_out[...] = _layernorm(DN_ALPHA * x_ref[...] + x2_out[...], g2_ref[...], b2_ref[...])


MOE_TM = 1024
MOE_CAP = 160


def _moe_call(x2d, ew):
    N = x2d.shape[0]
    TM = min(MOE_TM, N)
    HT = TM // 2
    CAP = min(MOE_CAP, HT)
    GW = EXPERTS_PER_GROUP * D_EXPERT
    row = pl.BlockSpec((TM, D_MODEL), lambda i, j: (i, 0))
    return pl.pallas_call(
        functools.partial(_moe_kernel, TM=TM, HT=HT, CAP=CAP),
        grid=(N // TM, N_GROUPS),
        in_specs=[row, _const_spec((N_EXPERTS, D_MODEL)), _const_spec((N_EXPERTS, 1)), _const_spec((HT, HT)),
                  _const_spec((D_MODEL, D_SHARED)), _const_spec((D_MODEL, D_SHARED)), _const_spec((D_SHARED, D_MODEL)),
                  pl.BlockSpec((D_MODEL, GW), lambda i, j: (0, j)), pl.BlockSpec((D_MODEL, GW), lambda i, j: (0, j)),
                  pl.BlockSpec((GW, D_MODEL), lambda i, j: (j, 0)),
                  _const_spec((1, D_MODEL)), _const_spec((1, D_MODEL))],
        out_specs=row,
        out_shape=jax.ShapeDtypeStruct((N, D_MODEL), F32),
        scratch_shapes=[pltpu.VMEM((TM, D_MODEL), BF16), pltpu.VMEM((TM, 128), F32), pltpu.VMEM((8, TM), F32),
                        pltpu.SMEM((1,), jnp.int32)],
        compiler_params=pltpu.CompilerParams(dimension_semantics=("parallel", "arbitrary"),
                                             vmem_limit_bytes=VMEM_LIMIT),
        name="moe",
    )(x2d, ew["rwt"], ew["rb"], jnp.triu(jnp.ones((HT, HT), BF16), 1), ew["wgs"], ew["wus"], ew["wds"],
      ew["wg"], ew["wu"], ew["wd"], ew["g2"], ew["b2"])


def _rot_cols(w):
    half = w.shape[-1] // 2
    return jnp.concatenate([-w[..., half:], w[..., :half]], axis=-1)


def _prep_layer(l, w_in, b_in, q_norm_g, kv_norm_g, w_uq, w_ukv, gmlp_ln_g, gmlp_ln_b, gmlp_ws, gmlp_bs,
                w_branch, w_out, ln1_g, ln1_b, moe_w_gate, moe_w_up, moe_w_down, shared_w_gate,
                shared_w_up, shared_w_down, ln2_g, ln2_b):
    wi = w_in[l]
    bi = b_in[l][None, :]

    def proj_cols(m):
        rows = m.shape[0]
        zero = lambda n: jnp.zeros((rows, n), m.dtype)
        kr = m[:, O_KR:O_MQ]
        kr128 = jnp.concatenate([zero(KR_LANE), kr, zero(HEAD_PAD - KR_LANE - ROPE_DIM)], axis=1)
        krr128 = jnp.concatenate([zero(KR_LANE), _rot_cols(kr), zero(HEAD_PAD - KR_LANE - ROPE_DIM)], axis=1)
        gates = jnp.concatenate([m[:, O_MI:O_MO], zero(HEAD_PAD - 2 * MLSTM_HEADS)], axis=1)
        return jnp.concatenate([m[:, O_CQ:O_KR], kr128, krr128, m[:, O_MQ:O_MI], gates, m[:, O_MO:O_GT]], axis=1)

    uq = w_uq[l].reshape(Q_RANK, MLA_HEADS, NOPE_DIM + ROPE_DIM)
    zq = lambda n: jnp.zeros((Q_RANK, MLA_HEADS, n), F32)
    pad = HEAD_PAD - NOPE_DIM - ROPE_DIM
    wq_a = jnp.concatenate([uq, zq(pad)], axis=-1).reshape(Q_RANK, QK_PAD)
    wq_b = jnp.concatenate([zq(NOPE_DIM), _rot_cols(uq[..., NOPE_DIM:]), zq(pad)], axis=-1).reshape(Q_RANK, QK_PAD)
    ukv = w_ukv[l].reshape(KV_RANK, MLA_HEADS, NOPE_DIM + V_DIM)
    wk = jnp.concatenate([ukv[..., :NOPE_DIM], jnp.zeros((KV_RANK, MLA_HEADS, HEAD_PAD - NOPE_DIM), F32)],
                         axis=-1).reshape(KV_RANK, QK_PAD)
    wv = ukv[..., NOPE_DIM:].reshape(KV_RANK, V_ALL)
    pw = dict(
        w1=proj_cols(wi).astype(BF16), b1=proj_cols(bi),
        qg=q_norm_g[l][None, :], kvg=kv_norm_g[l][None, :],
        wq=jnp.concatenate([wq_a, wq_b], axis=1).astype(BF16),
        wkv=jnp.concatenate([wk, wv], axis=1).astype(BF16),
        lng=gmlp_ln_g[l][None, :], lnb=gmlp_ln_b[l][None, :],
    )
    mw = dict(
        wgt=wi[:, O_GT:].astype(BF16), bgt=bi[:, O_GT:],
        ws_full=gmlp_ws[l], bs_full=gmlp_bs[l],
        wb=w_branch[l].astype(BF16), wo=w_out[l].astype(BF16),
        g1=ln1_g[l][None, :], b1=ln1_b[l][None, :],
    )
    cat_in = lambda we: jnp.transpose(we[l].astype(BF16), (1, 0, 2)).reshape(D_MODEL, N_EXPERTS * D_EXPERT)
    ew = dict(
        wg=cat_in(moe_w_gate), wu=cat_in(moe_w_up),
        wd=moe_w_down[l].reshape(N_EXPERTS * D_EXPERT, D_MODEL).astype(BF16),
        wgs=shared_w_gate[l].astype(BF16), wus=shared_w_up[l].astype(BF16), wds=shared_w_down[l].astype(BF16),
        g2=ln2_g[l][None, :], b2=ln2_b[l][None, :],
    )
    return pw, mw, ew


def _rope_tables(T, past, rows):
    half = ROPE_DIM // 2
    pos = (past + jnp.arange(T)).astype(F32)
    inv = ROPE_THETA ** (-jnp.arange(half, dtype=F32) / half)
    ang = pos[:, None] * inv[None, :]
    cos = jnp.cos(ang)
    sin = jnp.sin(ang)
    c2 = jnp.concatenate([cos, cos], axis=1)
    s2 = jnp.concatenate([sin, sin], axis=1)
    z = lambda n: jnp.zeros((T, n), F32)
    tail = HEAD_PAD - NOPE_DIM - ROPE_DIM
    tabs = dict(
        cq=jnp.concatenate([jnp.ones((T, NOPE_DIM), F32), c2, z(tail)], axis=1) * Q_SCALE,
        sq=jnp.concatenate([z(NOPE_DIM), s2, z(tail)], axis=1) * Q_SCALE,
        ck=jnp.concatenate([z(KR_LANE), c2, z(tail)], axis=1),
        sk=jnp.concatenate([z(KR_LANE), s2, z(tail)], axis=1),
    )
    if rows > T:
        tabs = {k: jnp.tile(v, (rows // T, 1)) for k, v in tabs.items()}
    return tabs


def _trunk(x, hist, layers, rwt, rb):
    B, T, _ = x.shape
    N = B * T
    past = 0 if hist is None else hist[0].shape[2]
    tabs = _rope_tables(T, past, min(512, N))
    L = GMLP_CHUNK if T % GMLP_CHUNK == 0 else T
    x2d = x.reshape(N, D_MODEL)
    Cs, ns, ms, gvs = [], [], [], []
    stacked = None
    for l, (pw, mw, ew) in enumerate(layers):
        KB = _attn_block(T) if hist is None else None
        outs = _proj_call(x2d, pw, tabs, T, hist is not None, KB, l, len(layers), stacked)
        q, k, v, _, _, mq, mk, mv, gates, so, gu, gv = outs[:12]
        stacked = (outs[CKV_OUT], outs[KR_OUT])
        seq = lambda a: a.reshape(B, T, a.shape[-1])
        if hist is None:
            a_out = _attn_call(seq(q), k.reshape(B, T // KB, QK_PAD, KB), seq(v))
            s0 = jnp.zeros((B, MLSTM_HEADS, MLSTM_DH, 2 * MLSTM_DH), F32)
            m0 = jnp.zeros((B, MLSTM_HEADS, 8, 128), F32)
        else:
            ckv_past, kr_past, c0, n0, m0_in = hist
            P = ckv_past.shape[2]
            krp = jnp.pad(kr_past[l], ((0, 0), (0, 0), (KR_LANE, HEAD_PAD - KR_LANE - ROPE_DIM)))
            a_out = _attn_hist_call(seq(q), ckv_past, l, krp, pw["wkv"], seq(k), seq(v))
            s0 = jnp.concatenate(
                [jnp.swapaxes(c0[l], -1, -2),
                 jnp.broadcast_to(n0[l][:, :, :, None], (B, MLSTM_HEADS, MLSTM_DH, MLSTM_DH))], axis=3)
            m0 = jnp.broadcast_to(m0_in[l][:, :, None, None], (B, MLSTM_HEADS, 8, 128))
        b_out, sfin, mfin = _mlstm_call(seq(mq), seq(mk), seq(mv), seq(gates), seq(so), s0, m0)
        mwl = dict(mw)
        mwl["ws"] = mw["ws_full"][:, :L, :L]
        mwl["bsf"] = jnp.repeat(mw["bs_full"][:, :L].T, GMLP_DG, axis=1)
        x1 = _merge_call(x2d, a_out.reshape(N, V_ALL), b_out.reshape(N, MLSTM_WIDTH), gu, gv, mwl, L)
        x2d = _moe_call(x1, dict(ew, rwt=rwt, rb=rb))
        Cs.append(jnp.swapaxes(sfin[:, :, :, :MLSTM_DH], -1, -2))
        ns.append(sfin[:, :, :, MLSTM_DH])
        ms.append(mfin[:, :, 0, 0])
        if hist is not None:
            gvs.append(outs[12].reshape(B, T, GMLP_WIDTH))
    depth = len(layers)
    res = [x2d.reshape(B, T, D_MODEL), stacked[0].reshape(depth, B, T, KV_RANK),
           stacked[1].reshape(depth, B, T, ROPE_DIM), jnp.stack(Cs), jnp.stack(ns), jnp.stack(ms)]
    if hist is not None:
        res.append(jnp.stack(gvs))
    return res


def kernel(x_prompt, x_sample, cache_mla_ckv, cache_mla_krope, state_mlstm_c, state_mlstm_n, state_mlstm_m,
           w_in, b_in, q_norm_g, kv_norm_g, w_uq, w_ukv, gmlp_ln_g, gmlp_ln_b, gmlp_ws, gmlp_bs, w_branch,
           w_out, ln1_g, ln1_b, router_w, router_b, moe_w_gate, moe_w_up, moe_w_down, shared_w_gate,
           shared_w_up, shared_w_down, ln2_g, ln2_b):
    depth = w_in.shape[0]
    layers = [_prep_layer(l, w_in, b_in, q_norm_g, kv_norm_g, w_uq, w_ukv, gmlp_ln_g, gmlp_ln_b, gmlp_ws,
                          gmlp_bs, w_branch, w_out, ln1_g, ln1_b, moe_w_gate, moe_w_up, moe_w_down,
                          shared_w_gate, shared_w_up, shared_w_down, ln2_g, ln2_b) for l in range(depth)]
    rwt = router_w.T.astype(BF16)
    rb = router_b[:, None]
    yp, p_ckv, p_kr, p_c, p_n, p_m = _trunk(x_prompt, None, layers, rwt, rb)
    ys, s_ckv, s_kr, s_c, s_n, s_m, s_gv = _trunk(
        x_sample, (cache_mla_ckv, cache_mla_krope, state_mlstm_c, state_mlstm_n, state_mlstm_m), layers, rwt, rb)
    return (yp, ys, p_ckv, p_kr, p_c, p_n, p_m, s_ckv, s_kr, s_c, s_n, s_m, s_gv)
```

```python
import functools

import jax
import jax.numpy as jnp
import numpy as np
from jax import lax
from jax.experimental import pallas as pl
from jax.experimental.pallas import tpu as pltpu

F32 = jnp.float32
BF16 = jnp.bfloat16

D_MODEL = 1024
CHUNK = 64
MLA_HEADS = 8
Q_RANK = 256
KV_RANK = 256
NOPE_DIM = 64
ROPE_DIM = 32
V_DIM = 64
ROPE_THETA = 10000.0
ATTN_SCALE = (NOPE_DIM + ROPE_DIM) ** -0.5
Q_SCALE = ATTN_SCALE * float(np.log2(np.e))
MLSTM_HEADS = 4
MLSTM_DH = 128
MLSTM_WIDTH = MLSTM_HEADS * MLSTM_DH
GMLP_GROUPS = 4
GMLP_DG = 128
GMLP_WIDTH = GMLP_GROUPS * GMLP_DG
GMLP_CHUNK = 128
N_BRANCH = 3
N_EXPERTS = 16
N_GROUPS = 4
EXPERTS_PER_GROUP = N_EXPERTS // N_GROUPS
D_EXPERT = 256
D_SHARED = 256
DEPTH = 4
DN_ALPHA = (2 * DEPTH) ** 0.25
EPS = 1e-5

HEAD_PAD = 128
QK_PAD = MLA_HEADS * HEAD_PAD
V_ALL = MLA_HEADS * V_DIM
P_CQ, P_CKV, P_KR, P_KRR, P_MQ, P_MK, P_MV, P_G, P_MO, P_GU, P_GV = (
    0, 256, 512, 640, 768, 1280, 1792, 2304, 2432, 2944, 3456)
D_PROJ = 3968
O_CQ, O_CKV, O_KR, O_MQ, O_MK, O_MV, O_MI, O_MF, O_MO, O_GU, O_GV, O_GT = (
    0, 256, 512, 544, 1056, 1568, 2080, 2084, 2088, 2600, 3112, 3624)
KR_LANE = NOPE_DIM
VMEM_LIMIT = 56 * 1024 * 1024
TOKEN_TILE = 512
ATTN_BLOCK = 256
MLSTM_STEP = 512
ROW_SPLIT = 256
NEG_INF = float("-inf")


def _dot(a, b):
    return jnp.dot(a, b, preferred_element_type=F32)


def _dot_nt(a, b):
    return lax.dot_general(a, b, (((1,), (1,)), ((), ())), preferred_element_type=F32)


def _const_spec(shape):
    nd = len(shape)
    return pl.BlockSpec(shape, lambda *_: (0,) * nd, pipeline_mode=pl.Buffered(1))


def _layernorm(x, g, b):
    mu = jnp.mean(x, axis=-1, keepdims=True)
    xc = x - mu
    var = jnp.mean(xc * xc, axis=-1, keepdims=True)
    return xc * lax.rsqrt(var + EPS) * g + b


def _rmsnorm(x, g):
    return x * lax.rsqrt(jnp.mean(x * x, axis=-1, keepdims=True) + EPS) * g


def _gelu(x):
    return 0.5 * x * (1.0 + jnp.tanh(np.sqrt(2.0 / np.pi).astype(np.float32) * (x + 0.044715 * (x * x * x))))


def _sigmoid(x):
    return 1.0 / (1.0 + jnp.exp(-x))


def _log_sigmoid(x):
    return jnp.minimum(x, 0.0) - jnp.log(1.0 + jnp.exp(-jnp.abs(x)))


PROJ_N_IN = 13


def _proj_kernel(*refs, key_block, n_carried):
    (x_ref, w1_ref, b1_ref, qg_ref, kvg_ref, wq_ref, wkv_ref, cq_ref, sq_ref, ck_ref, sk_ref,
     lng_ref, lnb_ref) = refs[:PROJ_N_IN]
    (q_out, k_out, v_out, ckv_out, kr_out, mq_out, mk_out, mv_out, g_out, so_out, gu_out,
     gv_out, *rest) = refs[PROJ_N_IN + n_carried:]
    xb = x_ref[...].astype(BF16)

    def z(lo, hi):
        return _dot(xb, w1_ref[:, lo:hi]) + b1_ref[:, lo:hi]


    zu = z(P_GU, D_PROJ)
    gu_out[...] = _gelu(zu[:, :GMLP_WIDTH]).astype(BF16)
    vrows = _layernorm(_gelu(zu[:, GMLP_WIDTH:]), lng_ref[...], lnb_ref[...])
    gv_out[...] = vrows.astype(BF16)
    if rest:
        rest[0][...] = vrows
    so_out[...] = _sigmoid(z(P_MO, P_GU)).astype(BF16)

    zc = z(P_CQ, P_MQ)
    cqn = _rmsnorm(zc[:, P_CQ:P_CKV], qg_ref[...])
    ckvn = _rmsnorm(zc[:, P_CKV:P_KR], kvg_ref[...])
    kr = zc[:, P_KR:P_KRR] * ck_ref[...] + zc[:, P_KRR:P_MQ] * sk_ref[...]
    ckv_out[...] = ckvn
    kr_out[...] = kr[:, KR_LANE:KR_LANE + ROPE_DIM]
    qq = _dot(cqn.astype(BF16), wq_ref[...])
    cos8 = jnp.concatenate([cq_ref[...]] * MLA_HEADS, axis=1)
    sin8 = jnp.concatenate([sq_ref[...]] * MLA_HEADS, axis=1)
    q_out[...] = (qq[:, :QK_PAD] * cos8 + qq[:, QK_PAD:] * sin8).astype(BF16)
    kk = _dot(ckvn.astype(BF16), wkv_ref[...])
    kr8 = jnp.concatenate([kr] * MLA_HEADS, axis=1)
    kval = kk[:, :QK_PAD] + kr8
    if key_block is None:
        k_out[...] = kval.astype(BF16)
    else:
        for c in range(kval.shape[0] // key_block):
            k_out[c] = kval[key_block * c:key_block * (c + 1), :].T.astype(BF16)
    v_out[...] = kk[:, QK_PAD:].astype(BF16)

    zm = z(P_MQ, P_MO)
    mq_out[...] = zm[:, 0:512].astype(BF16)
    mk_out[...] = (zm[:, 512:1024] * (MLSTM_DH ** -0.5)).astype(BF16)
    mv_out[...] = zm[:, 1024:1536].astype(BF16)
    zg = zm[:, 1536:1664]
    lane = lax.broadcasted_iota(jnp.int32, zg.shape, 1)
    g_out[...] = jnp.where(lane < MLSTM_HEADS, zg, _log_sigmoid(zg))


CKV_OUT, KR_OUT = 3, 4


def _proj_call(x2d, pw, tabs, T, want_vrows, key_block, layer, depth, carried):
    N = x2d.shape[0]
    TM = min(TOKEN_TILE, N)
    nt = max(1, T // TM)
    grid = (N // TM,)
    row = lambda w: pl.BlockSpec((TM, w), lambda i: (i, 0))
    tab = pl.BlockSpec((TM, HEAD_PAD), lambda i: (i % nt, 0))
    in_specs = [
        row(D_MODEL),
        _const_spec((D_MODEL, D_PROJ)), _const_spec((1, D_PROJ)),
        _const_spec((1, Q_RANK)), _const_spec((1, KV_RANK)),
        _const_spec((Q_RANK, 2 * QK_PAD)), _const_spec((KV_RANK, QK_PAD + V_ALL)),
        tab, tab, tab, tab,
        _const_spec((1, GMLP_WIDTH)), _const_spec((1, GMLP_WIDTH)),
    ]
    out_shapes = [
        ((N, QK_PAD), BF16), ((N, QK_PAD), BF16), ((N, V_ALL), BF16), ((N, KV_RANK), F32),
        ((N, HEAD_PAD), F32), ((N, MLSTM_WIDTH), BF16), ((N, MLSTM_WIDTH), BF16), ((N, MLSTM_WIDTH), BF16),
        ((N, HEAD_PAD), F32), ((N, MLSTM_WIDTH), BF16), ((N, GMLP_WIDTH), BF16), ((N, GMLP_WIDTH), BF16),
    ]
    if want_vrows:
        out_shapes.append(((N, GMLP_WIDTH), F32))
    out_specs = [row(s[1]) for s, _ in out_shapes]
    if key_block is not None:
        out_shapes[1] = ((N // key_block, QK_PAD, key_block), BF16)
        out_specs[1] = pl.BlockSpec((TM // key_block, QK_PAD, key_block), lambda i: (i, 0, 0))
    for pos, width in ((CKV_OUT, KV_RANK), (KR_OUT, ROPE_DIM)):
        out_shapes[pos] = ((depth, N, width), F32)
        out_specs[pos] = pl.BlockSpec((None, TM, width), lambda i: (layer, i, 0))
    args = [x2d, pw["w1"], pw["b1"], pw["qg"], pw["kvg"], pw["wq"], pw["wkv"],
            tabs["cq"], tabs["sq"], tabs["ck"], tabs["sk"], pw["lng"], pw["lnb"]]
    aliases = {}
    if carried is not None:
        in_specs += [pl.BlockSpec(memory_space=pl.ANY)] * 2
        aliases = {PROJ_N_IN: CKV_OUT, PROJ_N_IN + 1: KR_OUT}
        args += list(carried)
    return pl.pallas_call(
        functools.partial(_proj_kernel, key_block=key_block, n_carried=len(aliases)),
        grid=grid,
        in_specs=in_specs,
        out_specs=out_specs,
        out_shape=[jax.ShapeDtypeStruct(s, d) for s, d in out_shapes],
        input_output_aliases=aliases,
        compiler_params=pltpu.CompilerParams(dimension_semantics=("parallel",), vmem_limit_bytes=VMEM_LIMIT),
        name="proj",
    )(*args)


def _attn_kernel(q_ref, kt_ref, v_ref, o_ref, mx_scr, acc_scr, s_scr, *, TQ):
    i = pl.program_id(1)
    ones_blk = jnp.ones((TQ, HEAD_PAD), BF16)
    rc = lax.broadcasted_iota(jnp.int32, (TQ, TQ), 0) // CHUNK
    cc = lax.broadcasted_iota(jnp.int32, (TQ, TQ), 1) // CHUNK
    visible = cc <= rc
    lane = lax.broadcasted_iota(jnp.int32, (TQ, HEAD_PAD), 1)
    nlane = TQ // 128

    def fold(a, op):
        r = a[:, 0:128]
        for t in range(1, nlane):
            r = op(r, a[:, 128 * t:128 * (t + 1)])
        return r

    hsl = lambda h: slice(HEAD_PAD * h, HEAD_PAD * (h + 1))
    heads = range(MLA_HEADS)

    def sweep_max(j, masked, first):
        ss = [_dot(q_ref[0, :, hsl(h)], kt_ref[0, j, hsl(h), :]) for h in heads]
        if masked:
            ss = [jnp.where(visible, s, NEG_INF) for s in ss]
        for h in heads:
            s_scr[h, j] = ss[h]
        for h in heads:
            mx = fold(ss[h], jnp.maximum)
            mx_scr[h] = mx if first else jnp.maximum(mx_scr[h], mx)

    def sweep_pv(j, first):
        r0 = pl.multiple_of(j * TQ, TQ)
        ps = [jnp.exp2(s_scr[h, j] - jnp.concatenate([mx_scr[h]] * nlane, axis=1)).astype(BF16) for h in heads]
        pvs = []
        for p in range(MLA_HEADS // 2):
            vs = slice(HEAD_PAD * p, HEAD_PAD * (p + 1))
            vext = jnp.concatenate([v_ref[0, pl.ds(r0, TQ), vs], ones_blk], axis=1)
            pv2 = _dot(jnp.concatenate([ps[2 * p], ps[2 * p + 1]], axis=0), vext)
            pvs += [pv2[:TQ], pv2[TQ:]]
        for h in heads:
            acc_scr[h] = pvs[h] if first else acc_scr[h] + pvs[h]

    def body_max(j, c):
        sweep_max(j, False, False)
        return c

    def body_pv(j, c):
        sweep_pv(j, False)
        return c

    sweep_max(i, True, True)
    lax.fori_loop(0, i, body_max, 0)
    for h in heads:
        mx_scr[h] = jnp.broadcast_to(jnp.max(mx_scr[h], axis=-1, keepdims=True), (TQ, 128))
    sweep_pv(i, True)
    lax.fori_loop(0, i, body_pv, 0)
    for p in range(MLA_HEADS // 2):
        a0 = acc_scr[2 * p]
        a1 = acc_scr[2 * p + 1]
        o0 = a0[:, :HEAD_PAD] / a0[:, HEAD_PAD:]
        o1 = a1[:, :HEAD_PAD] / a1[:, HEAD_PAD:]
        o_ref[0, :, HEAD_PAD * p:HEAD_PAD * (p + 1)] = jnp.where(lane < V_DIM, o0, o1).astype(BF16)


def _attn_block(T):
    return min(ATTN_BLOCK, T)


def _attn_call(q, kt, v):
    B, T, _ = q.shape
    TQ = _attn_block(T)
    nq = T // TQ
    return pl.pallas_call(
        functools.partial(_attn_kernel, TQ=TQ),
        grid=(B, nq),
        in_specs=[pl.BlockSpec((1, TQ, QK_PAD), lambda b, i: (b, i, 0)),
                  pl.BlockSpec((1, nq, QK_PAD, TQ), lambda b, i: (b, 0, 0, 0)),
                  pl.BlockSpec((1, T, V_ALL), lambda b, i: (b, 0, 0))],
        out_specs=pl.BlockSpec((1, TQ, V_ALL), lambda b, i: (b, i, 0)),
        out_shape=jax.ShapeDtypeStruct((B, T, V_ALL), BF16),
        scratch_shapes=[pltpu.VMEM((MLA_HEADS, TQ, 128), F32), pltpu.VMEM((MLA_HEADS, TQ, 2 * HEAD_PAD), F32),
                        pltpu.VMEM((MLA_HEADS, nq, TQ, TQ), F32)],
        compiler_params=pltpu.CompilerParams(dimension_semantics=("parallel", "arbitrary"),
                                             vmem_limit_bytes=VMEM_LIMIT),
        name="attn",
    )(q, kt, v)


def _attn_hist_kernel(q_ref, cp_ref, krp_ref, wkv_ref, kn_ref, vn_ref, o_ref, kp_ref, vp_ref):
    T = q_ref.shape[1]
    lane = lax.broadcasted_iota(jnp.int32, (T, HEAD_PAD), 1)
    kk = _dot(cp_ref[0].astype(BF16), wkv_ref[...])
    kp_ref[0] = (kk[:, :QK_PAD] + jnp.concatenate([krp_ref[0]] * MLA_HEADS, axis=1)).astype(BF16)
    vp_ref[0] = kk[:, QK_PAD:].astype(BF16)
    for p in range(MLA_HEADS // 2):
        pair = []
        vs = slice(HEAD_PAD * p, HEAD_PAD * (p + 1))
        for hh in range(2):
            h = 2 * p + hh
            hs = slice(HEAD_PAD * h, HEAD_PAD * (h + 1))
            qh = q_ref[0, :, hs]
            s1 = _dot_nt(qh, kp_ref[0, :, hs])
            s2 = _dot_nt(qh, kn_ref[0, :, hs])
            m = jnp.maximum(jnp.max(s1, axis=-1, keepdims=True), jnp.max(s2, axis=-1, keepdims=True))
            p1 = jnp.exp2(s1 - m)
            p2 = jnp.exp2(s2 - m)
            l = jnp.sum(p1, axis=-1, keepdims=True) + jnp.sum(p2, axis=-1, keepdims=True)
            acc = _dot(p1.astype(BF16), vp_ref[0, :, vs]) + _dot(p2.astype(BF16), vn_ref[0, :, vs])
            pair.append(acc / l)
        o_ref[0, :, vs] = jnp.where(lane < V_DIM, pair[0], pair[1]).astype(BF16)


def _attn_hist_call(q, ckv_cache, layer, krp, wkv, kn, vn):
    B, T, _ = q.shape
    P = ckv_cache.shape[2]
    blk = lambda t, w: pl.BlockSpec((1, t, w), lambda b: (b, 0, 0))
    cache_blk = pl.BlockSpec((None, 1, P, KV_RANK), lambda b: (layer, b, 0, 0))
    return pl.pallas_call(
        _attn_hist_kernel,
        grid=(B,),
        in_specs=[blk(T, QK_PAD), cache_blk, blk(P, HEAD_PAD), _const_spec((KV_RANK, QK_PAD + V_ALL)),
                  blk(T, QK_PAD), blk(T, V_ALL)],
        out_specs=blk(T, V_ALL),
        out_shape=jax.ShapeDtypeStruct((B, T, V_ALL), BF16),
        scratch_shapes=[pltpu.VMEM((1, P, QK_PAD), BF16), pltpu.VMEM((1, P, V_ALL), BF16)],
        compiler_params=pltpu.CompilerParams(dimension_semantics=("parallel",), vmem_limit_bytes=VMEM_LIMIT),
        name="attn_hist",
    )(q, ckv_cache, krp, wkv, kn, vn)


MLSTM_BB = 4


def _mlstm_kernel(q_ref, k_ref, v_ref, g_ref, so_ref, s0_ref, m0_ref, h_out, sfin_out, mfin_out,
                  s_scr, m_scr, *, TL, BB):
    t = pl.program_id(1)
    L = CHUNK
    DH = MLSTM_DH

    @pl.when(t == 0)
    def _():
        s_scr[...] = s0_ref[...]
        m_scr[...] = m0_ref[...]

    r_i = lax.broadcasted_iota(jnp.int32, (L, L), 0)
    c_i = lax.broadcasted_iota(jnp.int32, (L, L), 1)
    causal = c_i <= r_i
    tri = jnp.where(causal, 1.0, 0.0).astype(BF16)
    lane = lax.broadcasted_iota(jnp.int32, (L, 128), 1)
    ones_blk = jnp.ones((L, DH), BF16)

    def rep(col):
        return jnp.broadcast_to(col, (L, 128))

    def chunk(c, carry):
        rows = pl.ds(pl.multiple_of(c * L, L), L)
        pairs = [(bb, h) for bb in range(BB) for h in range(MLSTM_HEADS)]
        hsl = lambda h: slice(DH * h, DH * (h + 1))
        Gs, cums, VTs = [], [], []
        for bb in range(BB):
            G = g_ref[bb, rows, :]
            g_hi = G.astype(BF16)
            g_r1 = G - g_hi.astype(F32)
            g_mid = g_r1.astype(BF16)
            g_lo = (g_r1 - g_mid.astype(F32)).astype(BF16)
            cum = _dot(tri, g_hi) + _dot(tri, g_mid) + _dot(tri, g_lo)
            Gs.append(G)
            cums.append(cum)
            VTs.append(jnp.where(lane < MLSTM_HEADS, G, cum).T)
        qk_raw = {p: _dot_nt(q_ref[p[0], rows, hsl(p[1])], k_ref[p[0], rows, hsl(p[1])]) for p in pairs}
        sq = {p: _dot(q_ref[p[0], rows, hsl(p[1])], s_scr[p[0], p[1]].astype(BF16)) for p in pairs}
        gate = {}
        for bb, h in pairs:
            b_t = rep(jnp.sum(jnp.where(lane == MLSTM_HEADS + h, cums[bb], 0.0), axis=1, keepdims=True))
            ig_t = rep(jnp.sum(jnp.where(lane == h, Gs[bb], 0.0), axis=1, keepdims=True))
            brow = VTs[bb][MLSTM_HEADS + h:MLSTM_HEADS + h + 1, :]
            igrow = VTs[bb][h:h + 1, :]
            d = jnp.where(causal, b_t[:, :L] - brow + igrow, NEG_INF)
            gate[bb, h] = (b_t, ig_t, d, rep(jnp.max(d, axis=1, keepdims=True)))
        stab = {}
        for bb, h in pairs:
            b_t, ig_t, d, dmax = gate[bb, h]
            m_prev = m_scr[bb, h, 0:1, :]
            g_t = b_t + m_prev
            mt = jnp.maximum(g_t, dmax)
            stab[bb, h] = (m_prev, mt, jnp.exp(g_t - mt), jnp.exp(d - mt[:, :L]))
        for bb, h in pairs:
            m_prev, mt, inter, w = stab[bb, h]
            vext = jnp.concatenate([v_ref[bb, rows, hsl(h)], ones_blk], axis=1)
            intra = _dot((qk_raw[bb, h] * w).astype(BF16), vext)
            num = inter * sq[bb, h][:, :DH] + intra[:, :DH]
            den = jnp.maximum(jnp.abs(inter * sq[bb, h][:, DH:] + intra[:, DH:]), jnp.exp(-mt))
            so = so_ref[bb, rows, hsl(h)].astype(F32)
            h_out[bb, rows, hsl(h)] = (so * (num / den)).astype(BF16)
        for bb, h in pairs:
            b_t, ig_t, _, _ = gate[bb, h]
            m_prev, mt, _, _ = stab[bb, h]
            mL = mt[L - 1:L, :]
            bL = b_t[L - 1:L, :]
            ws_t = jnp.exp(bL - b_t + ig_t - mL)
            decay = jnp.exp(bL + m_prev - mL)
            wv = jnp.concatenate([ws_t * v_ref[bb, rows, hsl(h)].astype(F32), ws_t], axis=1).astype(BF16)
            kT = k_ref[bb, rows, hsl(h)].astype(F32).T.astype(BF16)
            s_scr[bb, h] = jnp.concatenate([decay, decay], axis=1) * s_scr[bb, h] + _dot(kT, wv)
            m_scr[bb, h] = jnp.broadcast_to(mL, (8, 128))
        return carry

    lax.fori_loop(0, TL // L, chunk, 0)

    @pl.when(t == pl.num_programs(1) - 1)
    def _():
        sfin_out[...] = s_scr[...]
        mfin_out[...] = m_scr[...]


def _mlstm_call(mq, mk, mv, gates, so, s0, m0):
    B, T, _ = mq.shape
    TL = min(MLSTM_STEP, T)
    BB = int(np.gcd(MLSTM_BB, B))
    seq = lambda w: pl.BlockSpec((BB, TL, w), lambda b, t: (b, t, 0))
    st = pl.BlockSpec((BB, MLSTM_HEADS, MLSTM_DH, 2 * MLSTM_DH), lambda b, t: (b, 0, 0, 0))
    mst = pl.BlockSpec((BB, MLSTM_HEADS, 8, 128), lambda b, t: (b, 0, 0, 0))
    return pl.pallas_call(
        functools.partial(_mlstm_kernel, TL=TL, BB=BB),
        grid=(B // BB, T // TL),
        in_specs=[seq(MLSTM_WIDTH), seq(MLSTM_WIDTH), seq(MLSTM_WIDTH), seq(128), seq(MLSTM_WIDTH), st, mst],
        out_specs=[seq(MLSTM_WIDTH), st, mst],
        out_shape=[jax.ShapeDtypeStruct((B, T, MLSTM_WIDTH), BF16),
                   jax.ShapeDtypeStruct((B, MLSTM_HEADS, MLSTM_DH, 2 * MLSTM_DH), F32),
                   jax.ShapeDtypeStruct((B, MLSTM_HEADS, 8, 128), F32)],
        scratch_shapes=[pltpu.VMEM((BB, MLSTM_HEADS, MLSTM_DH, 2 * MLSTM_DH), F32),
                        pltpu.VMEM((BB, MLSTM_HEADS, 8, 128), F32)],
        compiler_params=pltpu.CompilerParams(dimension_semantics=("parallel", "arbitrary"),
                                             vmem_limit_bytes=VMEM_LIMIT),
        name="mlstm",
    )(mq, mk, mv, gates, so, s0, m0)


def _merge_kernel(x_ref, a_ref, b_ref, gu_ref, gv_ref, wgt_ref, bgt_ref, ws_ref, bsf_ref, wb_ref, wo_ref,
                  g1_ref, b1_ref, x1_out, c_scr, *, L, TM):
    x = x_ref[...]
    xb = x.astype(BF16)
    r_i = lax.broadcasted_iota(jnp.int32, (L, L), 0)
    c_i = lax.broadcasted_iota(jnp.int32, (L, L), 1)
    for g in range(GMLP_GROUPS):
        gs = slice(GMLP_DG * g, GMLP_DG * (g + 1))
        wsg = jnp.where(c_i <= r_i, ws_ref[g], 0.0).astype(BF16)
        for c in range(TM // L):
            rs = slice(L * c, L * (c + 1))
            sp = _dot(wsg, gv_ref[rs, gs]) + bsf_ref[:, gs]
            c_scr[rs, gs] = (gu_ref[rs, gs].astype(F32) * sp).astype(BF16)
    merged = None
    for kb, br in enumerate((a_ref, b_ref, c_scr)):
        cs = slice(D_MODEL * kb, D_MODEL * (kb + 1))
        gate = _sigmoid(_dot(xb, wgt_ref[:, cs]) + bgt_ref[:, cs])
        term = gate * _dot(br[...], wb_ref[kb])
        merged = term if merged is None else merged + term
    mb = merged.astype(BF16)
    RH = min(ROW_SPLIT, TM)
    for r in range(TM // RH):
        rs = slice(RH * r, RH * (r + 1))
        y = _dot(mb[rs, :], wo_ref[...])
        x1_out[rs, :] = _layernorm(DN_ALPHA * x[rs, :] + y, g1_ref[...], b1_ref[...])


def _merge_call(x2d, a2d, b2d, gu, gv, mw, L):
    N = x2d.shape[0]
    TM = min(2 * TOKEN_TILE, N)
    row = lambda w: pl.BlockSpec((TM, w), lambda i: (i, 0))
    return pl.pallas_call(
        functools.partial(_merge_kernel, L=L, TM=TM),
        grid=(N // TM,),
        in_specs=[row(D_MODEL), row(V_ALL), row(MLSTM_WIDTH), row(GMLP_WIDTH), row(GMLP_WIDTH),
                  _const_spec((D_MODEL, N_BRANCH * D_MODEL)), _const_spec((1, N_BRANCH * D_MODEL)),
                  _const_spec((GMLP_GROUPS, L, L)), _const_spec((L, GMLP_WIDTH)),
                  _const_spec((N_BRANCH, 512, D_MODEL)), _const_spec((D_MODEL, D_MODEL)),
                  _const_spec((1, D_MODEL)), _const_spec((1, D_MODEL))],
        out_specs=row(D_MODEL),
        out_shape=jax.ShapeDtypeStruct((N, D_MODEL), F32),
        scratch_shapes=[pltpu.VMEM((TM, GMLP_WIDTH), BF16)],
        compiler_params=pltpu.CompilerParams(dimension_semantics=("parallel",), vmem_limit_bytes=VMEM_LIMIT),
        name="merge",
    )(x2d, a2d, b2d, gu, gv, mw["wgt"], mw["bgt"], mw["ws"], mw["bsf"], mw["wb"], mw["wo"], mw["g1"], mw["b1"])


def _route_rows(logits_t, rb_ref):
    s = _sigmoid(logits_t)
    sb = s + rb_ref[...]
    rows = [sb[e:e + 1, :] for e in range(N_EXPERTS)]
    srow = [s[e:e + 1, :] for e in range(N_EXPERTS)]
    gscore = []
    for g in range(N_GROUPS):
        mem = rows[EXPERTS_PER_GROUP * g:EXPERTS_PER_GROUP * (g + 1)]
        best = None
        for a in range(EXPERTS_PER_GROUP):
            for b in range(a + 1, EXPERTS_PER_GROUP):
                pr = mem[a] + mem[b]
                best = pr if best is None else jnp.maximum(best, pr)
        gscore.append(best)
    gmax = functools.reduce(jnp.maximum, gscore)
    taken = None
    gsel = []
    for g in range(N_GROUPS):
        hit = gscore[g] == gmax
        if taken is None:
            gsel.append(hit)
            taken = hit
        else:
            gsel.append(jnp.logical_and(hit, jnp.logical_not(taken)))
            taken = jnp.logical_or(taken, hit)
    sel_w = []
    for e in range(N_EXPERTS):
        g = e // EXPERTS_PER_GROUP
        rank = None
        for o in range(EXPERTS_PER_GROUP * g, EXPERTS_PER_GROUP * (g + 1)):
            if o == e:
                continue
            ahead = (rows[o] >= rows[e]) if o < e else (rows[o] > rows[e])
            ahead = jnp.where(ahead, 1.0, 0.0)
            rank = ahead if rank is None else rank + ahead
        chosen = jnp.logical_and(gsel[g], rank < 1.5)
        sel_w.append(jnp.where(chosen, srow[e], 0.0))
    den = functools.reduce(jnp.add, sel_w)
    gate_rows = [w_ / den for w_ in sel_w]
    return gsel, gate_rows


def _swiglu_hidden(xb, wg, wu):
    hg = _dot(xb, wg)
    return hg * _sigmoid(hg) * _dot(xb, wu)


def _moe_kernel(x_ref, rw_ref, rb_ref, tri_ref, wgs_ref, wus_ref, wds_ref, wg_ref, wu_ref, wd_ref,
                g2_ref, b2_ref, x2_out, xb_scr, col_scr, row_scr, y_scr, flag_ref, *, TM, HT, CAP):
    j = pl.program_id(1)
    nh = TM // HT
    gate_lanes = EXPERTS_PER_GROUP
    RANK_LANE, GRP_LANE = gate_lanes, gate_lanes + 1

    @pl.when(j == 0)
    def _route():
        xb = x_ref[...].astype(BF16)
        xb_scr[...] = xb
        gsel, gate_rows = _route_rows(_dot_nt(rw_ref[...], xb), rb_ref)
        isg = [jnp.where(m, 1.0, 0.0) for m in gsel]
        grp = functools.reduce(jnp.add, [float(g) * isg[g] for g in range(N_GROUPS)])
        g4 = [functools.reduce(jnp.add, [isg[g] * gate_rows[EXPERTS_PER_GROUP * g + e] for g in range(N_GROUPS)])
              for e in range(EXPERTS_PER_GROUP)]
        ranks = []
        worst = None
        for hf in range(nh):
            hsl = slice(HT * hf, HT * (hf + 1))
            m8 = jnp.concatenate([isg[g][:, hsl] for g in range(N_GROUPS)]
                                 + [jnp.zeros((8 - N_GROUPS, HT), F32)], axis=0).astype(BF16)
            before = _dot(m8, tri_ref[...])
            ranks.append(functools.reduce(jnp.add, [isg[g][:, hsl] * before[g:g + 1, :] for g in range(N_GROUPS)]))
            for g in range(N_GROUPS):
                cnt = jnp.sum(isg[g][:, hsl])
                worst = cnt if worst is None else jnp.maximum(worst, cnt)
        rank = jnp.concatenate(ranks, axis=1)
        flag_ref[0] = (worst > float(CAP)).astype(jnp.int32)
        row_scr[...] = jnp.concatenate([rank, grp, jnp.zeros((6, TM), F32)], axis=0)
        col_scr[...] = jnp.concatenate(g4 + [rank, grp, jnp.zeros((128 - gate_lanes - 2, TM), F32)], axis=0).T
        x2_out[...] = _dot(_swiglu_hidden(xb, wgs_ref[...], wus_ref[...]).astype(BF16), wds_ref[...])

    gf = j.astype(F32)
    overflow = flag_ref[0] != 0

    def gated(h, gcols):
        parts = [h[:, D_EXPERT * e:D_EXPERT * (e + 1)] * gcols[:, e:e + 1] for e in range(EXPERTS_PER_GROUP)]
        return jnp.concatenate(parts, axis=1).astype(BF16)

    last = j == pl.num_programs(1) - 1

    def compact_group():
        r_iota = lax.broadcasted_iota(jnp.int32, (CAP, HT), 0).astype(F32)
        xcs, gcs = [], []
        for hf in range(nh):
            hsl = slice(HT * hf, HT * (hf + 1))
            pick = jnp.logical_and(row_scr[1:2, hsl] == gf, row_scr[0:1, hsl] == r_iota)
            P = jnp.where(pick, 1.0, 0.0).astype(BF16)
            xcs.append(_dot(P, xb_scr[hsl, :]).astype(BF16))
            cols = col_scr[hsl, :]
            c_hi = cols.astype(BF16)
            c_lo = (cols - c_hi.astype(F32)).astype(BF16)
            gcs.append(_dot(P, c_hi) + _dot(P, c_lo))
        xc = jnp.concatenate(xcs, axis=0)
        h = gated(_swiglu_hidden(xc, wg_ref[...], wu_ref[...]), jnp.concatenate(gcs, axis=0))
        return _dot(h, wd_ref[...]).astype(BF16)

    @pl.when(jnp.logical_and(jnp.logical_not(overflow), jnp.logical_not(last)))
    def _compact():
        y_scr[j] = compact_group()

    @pl.when(jnp.logical_and(jnp.logical_not(overflow), last))
    def _compact_last():
        ys = [y_scr[g] for g in range(N_GROUPS - 1)] + [compact_group()]
        c_iota = lax.broadcasted_iota(jnp.int32, (HT, N_GROUPS * CAP), 1).astype(F32)
        for hf in range(nh):
            hsl = slice(HT * hf, HT * (hf + 1))
            cols = col_scr[hsl, :]
            slot = cols[:, GRP_LANE:GRP_LANE + 1] * float(CAP) + cols[:, RANK_LANE:RANK_LANE + 1]
            Pt = jnp.where(slot == c_iota, 1.0, 0.0).astype(BF16)
            y_half = jnp.concatenate([y[CAP * hf:CAP * (hf + 1), :] for y in ys], axis=0)
            moe = x2_out[hsl, :] + _dot(Pt, y_half)
            x2_out[hsl, :] = _layernorm(DN_ALPHA * x_ref[hsl, :] + moe, g2_ref[...], b2_ref[...])

    @pl.when(overflow)
    def _uncompacted():
        RC = min(ROW_SPLIT, TM)

        def rows(c, carry):
            rs = pl.ds(pl.multiple_of(c * RC, RC), RC)
            cols = col_scr[rs, :]
            gcols = jnp.where(cols[:, GRP_LANE:GRP_LANE + 1] == gf, cols, 0.0)
            h = gated(_swiglu_hidden(xb_scr[rs, :], wg_ref[...], wu_ref[...]), gcols)
            x2_out[rs, :] += _dot(h, wd_ref[...])
            return carry

        lax.fori_loop(0, TM // RC, rows, 0)

    @pl.when(jnp.logical_and(overflow, last))
    def _finish_uncompacted():
        x2_out[...] = _layernorm(DN_ALPHA * x_ref[...] + x2_out[...], g2_ref[...], b2_ref[...])


MOE_TM = 1024
MOE_CAP = 160


def _moe_call(x2d, ew):
    N = x2d.shape[0]
    TM = min(MOE_TM, N)
    HT = TM // 2
    CAP = min(MOE_CAP, HT)
    GW = EXPERTS_PER_GROUP * D_EXPERT
    row = pl.BlockSpec((TM, D_MODEL), lambda i, j: (i, 0))
    return pl.pallas_call(
        functools.partial(_moe_kernel, TM=TM, HT=HT, CAP=CAP),
        grid=(N // TM, N_GROUPS),
        in_specs=[row, _const_spec((N_EXPERTS, D_MODEL)), _const_spec((N_EXPERTS, 1)), _const_spec((HT, HT)),
                  _const_spec((D_MODEL, D_SHARED)), _const_spec((D_MODEL, D_SHARED)), _const_spec((D_SHARED, D_MODEL)),
                  pl.BlockSpec((D_MODEL, GW), lambda i, j: (0, j)), pl.BlockSpec((D_MODEL, GW), lambda i, j: (0, j)),
                  pl.BlockSpec((GW, D_MODEL), lambda i, j: (j, 0)),
                  _const_spec((1, D_MODEL)), _const_spec((1, D_MODEL))],
        out_specs=row,
        out_shape=jax.ShapeDtypeStruct((N, D_MODEL), F32),
        scratch_shapes=[pltpu.VMEM((TM, D_MODEL), BF16), pltpu.VMEM((TM, 128), F32), pltpu.VMEM((8, TM), F32),
                        pltpu.VMEM((N_GROUPS - 1, 2 * CAP, D_MODEL), BF16), pltpu.SMEM((1,), jnp.int32)],
        compiler_params=pltpu.CompilerParams(dimension_semantics=("parallel", "arbitrary"),
                                             vmem_limit_bytes=VMEM_LIMIT),
        name="moe",
    )(x2d, ew["rwt"], ew["rb"], jnp.triu(jnp.ones((HT, HT), BF16), 1), ew["wgs"], ew["wus"], ew["wds"],
      ew["wg"], ew["wu"], ew["wd"], ew["g2"], ew["b2"])


def _rot_cols(w):
    half = w.shape[-1] // 2
    return jnp.concatenate([-w[..., half:], w[..., :half]], axis=-1)


def _prep_layer(l, w_in, b_in, q_norm_g, kv_norm_g, w_uq, w_ukv, gmlp_ln_g, gmlp_ln_b, gmlp_ws, gmlp_bs,
                w_branch, w_out, ln1_g, ln1_b, moe_w_gate, moe_w_up, moe_w_down, shared_w_gate,
                shared_w_up, shared_w_down, ln2_g, ln2_b):
    wi = w_in[l]
    bi = b_in[l][None, :]

    def proj_cols(m):
        rows = m.shape[0]
        zero = lambda n: jnp.zeros((rows, n), m.dtype)
        kr = m[:, O_KR:O_MQ]
        kr128 = jnp.concatenate([zero(KR_LANE), kr, zero(HEAD_PAD - KR_LANE - ROPE_DIM)], axis=1)
        krr128 = jnp.concatenate([zero(KR_LANE), _rot_cols(kr), zero(HEAD_PAD - KR_LANE - ROPE_DIM)], axis=1)
        gates = jnp.concatenate([m[:, O_MI:O_MO], zero(HEAD_PAD - 2 * MLSTM_HEADS)], axis=1)
        return jnp.concatenate([m[:, O_CQ:O_KR], kr128, krr128, m[:, O_MQ:O_MI], gates, m[:, O_MO:O_GT]], axis=1)

    uq = w_uq[l].reshape(Q_RANK, MLA_HEADS, NOPE_DIM + ROPE_DIM)
    zq = lambda n: jnp.zeros((Q_RANK, MLA_HEADS, n), F32)
    pad = HEAD_PAD - NOPE_DIM - ROPE_DIM
    wq_a = jnp.concatenate([uq, zq(pad)], axis=-1).reshape(Q_RANK, QK_PAD)
    wq_b = jnp.concatenate([zq(NOPE_DIM), _rot_cols(uq[..., NOPE_DIM:]), zq(pad)], axis=-1).reshape(Q_RANK, QK_PAD)
    ukv = w_ukv[l].reshape(KV_RANK, MLA_HEADS, NOPE_DIM + V_DIM)
    wk = jnp.concatenate([ukv[..., :NOPE_DIM], jnp.zeros((KV_RANK, MLA_HEADS, HEAD_PAD - NOPE_DIM), F32)],
                         axis=-1).reshape(KV_RANK, QK_PAD)
    wv = ukv[..., NOPE_DIM:].reshape(KV_RANK, V_ALL)
    pw = dict(
        w1=proj_cols(wi).astype(BF16), b1=proj_cols(bi),
        qg=q_norm_g[l][None, :], kvg=kv_norm_g[l][None, :],
        wq=jnp.concatenate([wq_a, wq_b], axis=1).astype(BF16),
        wkv=jnp.concatenate([wk, wv], axis=1).astype(BF16),
        lng=gmlp_ln_g[l][None, :], lnb=gmlp_ln_b[l][None, :],
    )
    mw = dict(
        wgt=wi[:, O_GT:].astype(BF16), bgt=bi[:, O_GT:],
        ws_full=gmlp_ws[l], bs_full=gmlp_bs[l],
        wb=w_branch[l].astype(BF16), wo=w_out[l].astype(BF16),
        g1=ln1_g[l][None, :], b1=ln1_b[l][None, :],
    )
    cat_in = lambda we: jnp.transpose(we[l].astype(BF16), (1, 0, 2)).reshape(D_MODEL, N_EXPERTS * D_EXPERT)
    ew = dict(
        wg=cat_in(moe_w_gate), wu=cat_in(moe_w_up),
        wd=moe_w_down[l].reshape(N_EXPERTS * D_EXPERT, D_MODEL).astype(BF16),
        wgs=shared_w_gate[l].astype(BF16), wus=shared_w_up[l].astype(BF16), wds=shared_w_down[l].astype(BF16),
        g2=ln2_g[l][None, :], b2=ln2_b[l][None, :],
    )
    return pw, mw, ew


def _rope_tables(T, past, rows):
    half = ROPE_DIM // 2
    pos = (past + jnp.arange(T)).astype(F32)
    inv = ROPE_THETA ** (-jnp.arange(half, dtype=F32) / half)
    ang = pos[:, None] * inv[None, :]
    cos = jnp.cos(ang)
    sin = jnp.sin(ang)
    c2 = jnp.concatenate([cos, cos], axis=1)
    s2 = jnp.concatenate([sin, sin], axis=1)
    z = lambda n: jnp.zeros((T, n), F32)
    tail = HEAD_PAD - NOPE_DIM - ROPE_DIM
    tabs = dict(
        cq=jnp.concatenate([jnp.ones((T, NOPE_DIM), F32), c2, z(tail)], axis=1) * Q_SCALE,
        sq=jnp.concatenate([z(NOPE_DIM), s2, z(tail)], axis=1) * Q_SCALE,
        ck=jnp.concatenate([z(KR_LANE), c2, z(tail)], axis=1),
        sk=jnp.concatenate([z(KR_LANE), s2, z(tail)], axis=1),
    )
    if rows > T:
        tabs = {k: jnp.tile(v, (rows // T, 1)) for k, v in tabs.items()}
    return tabs


def _trunk(x, hist, layers, rwt, rb):
    B, T, _ = x.shape
    N = B * T
    past = 0 if hist is None else hist[0].shape[2]
    tabs = _rope_tables(T, past, min(TOKEN_TILE, N))
    L = GMLP_CHUNK if T % GMLP_CHUNK == 0 else T
    x2d = x.reshape(N, D_MODEL)
    Cs, ns, ms, gvs = [], [], [], []
    stacked = None
    for l, (pw, mw, ew) in enumerate(layers):
        KB = _attn_block(T) if hist is None else None
        outs = _proj_call(x2d, pw, tabs, T, hist is not None, KB, l, len(layers), stacked)
        q, k, v, _, _, mq, mk, mv, gates, so, gu, gv = outs[:12]
        stacked = (outs[CKV_OUT], outs[KR_OUT])
        seq = lambda a: a.reshape(B, T, a.shape[-1])
        if hist is None:
            a_out = _attn_call(seq(q), k.reshape(B, T // KB, QK_PAD, KB), seq(v))
            s0 = jnp.zeros((B, MLSTM_HEADS, MLSTM_DH, 2 * MLSTM_DH), F32)
            m0 = jnp.zeros((B, MLSTM_HEADS, 8, 128), F32)
        else:
            ckv_past, kr_past, c0, n0, m0_in = hist
            P = ckv_past.shape[2]
            krp = jnp.pad(kr_past[l], ((0, 0), (0, 0), (KR_LANE, HEAD_PAD - KR_LANE - ROPE_DIM)))
            a_out = _attn_hist_call(seq(q), ckv_past, l, krp, pw["wkv"], seq(k), seq(v))
            s0 = jnp.concatenate(
                [jnp.swapaxes(c0[l], -1, -2),
                 jnp.broadcast_to(n0[l][:, :, :, None], (B, MLSTM_HEADS, MLSTM_DH, MLSTM_DH))], axis=3)
            m0 = jnp.broadcast_to(m0_in[l][:, :, None, None], (B, MLSTM_HEADS, 8, 128))
        b_out, sfin, mfin = _mlstm_call(seq(mq), seq(mk), seq(mv), seq(gates), seq(so), s0, m0)
        mwl = dict(mw)
        mwl["ws"] = mw["ws_full"][:, :L, :L]
        mwl["bsf"] = jnp.repeat(mw["bs_full"][:, :L].T, GMLP_DG, axis=1)
        x1 = _merge_call(x2d, a_out.reshape(N, V_ALL), b_out.reshape(N, MLSTM_WIDTH), gu, gv, mwl, L)
        x2d = _moe_call(x1, dict(ew, rwt=rwt, rb=rb))
        Cs.append(jnp.swapaxes(sfin[:, :, :, :MLSTM_DH], -1, -2))
        ns.append(sfin[:, :, :, MLSTM_DH])
        ms.append(mfin[:, :, 0, 0])
        if hist is not None:
            gvs.append(outs[12].reshape(B, T, GMLP_WIDTH))
    depth = len(layers)
    res = [x2d.reshape(B, T, D_MODEL), stacked[0].reshape(depth, B, T, KV_RANK),
           stacked[1].reshape(depth, B, T, ROPE_DIM), jnp.stack(Cs), jnp.stack(ns), jnp.stack(ms)]
    if hist is not None:
        res.append(jnp.stack(gvs))
    return res


def kernel(x_prompt, x_sample, cache_mla_ckv, cache_mla_krope, state_mlstm_c, state_mlstm_n, state_mlstm_m,
           w_in, b_in, q_norm_g, kv_norm_g, w_uq, w_ukv, gmlp_ln_g, gmlp_ln_b, gmlp_ws, gmlp_bs, w_branch,
           w_out, ln1_g, ln1_b, router_w, router_b, moe_w_gate, moe_w_up, moe_w_down, shared_w_gate,
           shared_w_up, shared_w_down, ln2_g, ln2_b):
    depth = w_in.shape[0]
    layers = [_prep_layer(l, w_in, b_in, q_norm_g, kv_norm_g, w_uq, w_ukv, gmlp_ln_g, gmlp_ln_b, gmlp_ws,
                          gmlp_bs, w_branch, w_out, ln1_g, ln1_b, moe_w_gate, moe_w_up, moe_w_down,
                          shared_w_gate, shared_w_up, shared_w_down, ln2_g, ln2_b) for l in range(depth)]
    rwt = router_w.T.astype(BF16)
    rb = router_b[:, None]
    yp, p_ckv, p_kr, p_c, p_n, p_m = _trunk(x_prompt, None, layers, rwt, rb)
    ys, s_ckv, s_kr, s_c, s_n, s_m, s_gv = _trunk(
        x_sample, (cache_mla_ckv, cache_mla_krope, state_mlstm_c, state_mlstm_n, state_mlstm_m), layers, rwt, rb)
    return (yp, ys, p_ckv, p_kr, p_c, p_n, p_m, s_ckv, s_kr, s_c, s_n, s_m, s_gv)
```
